```python
import math
import jax
import jax.numpy as jnp
from jax import lax
import numpy as np

D_MODEL = 1024
BATCH = 2
SEQ = 16384
DEPTH = 2

CHUNK = 64
Q_BLOCK = 128
NORM_EPS = 1e-6
ROPE_THETA = 10000.0

DIFF_HEADS = 4
DIFF_QK_DIM = 64
DIFF_V_DIM = 128
MLA_HEADS = 4
MLA_Q_LORA = 384
MLA_KV_LORA = 256
MLA_NOPE = 64
MLA_ROPE = 32
MLA_V_DIM = 128
GDN_HEADS = 4
GDN_K_DIM = 128
GDN_V_DIM = 128
GDN_CONV = 4
FFN_HIDDEN = ((8 * D_MODEL // 3 + 255) // 256) * 256

N_BRANCHES = 3
IN_WIDTHS = (
    DIFF_HEADS * 2 * DIFF_QK_DIM,
    DIFF_HEADS * 2 * DIFF_QK_DIM,
    DIFF_HEADS * DIFF_V_DIM,
    MLA_Q_LORA,
    MLA_KV_LORA + MLA_ROPE,
    GDN_HEADS * GDN_K_DIM,
    GDN_HEADS * GDN_K_DIM,
    GDN_HEADS * GDN_V_DIM,
    GDN_HEADS * GDN_V_DIM,
    GDN_HEADS,
    GDN_HEADS,
    N_BRANCHES * D_MODEL,
)
IN_COLS = sum(IN_WIDTHS)
IN_OFFSETS = tuple(int(o) for o in np.cumsum(IN_WIDTHS)[:-1])

kernel_name = 'hybrid_diff_mla_gdn_streaming_block'


def _rms_norm(x, w):
    xf = x.astype(jnp.float32)
    y = xf * lax.rsqrt(jnp.mean(xf * xf, axis=-1, keepdims=True) + NORM_EPS)
    return (y * w.astype(jnp.float32)).astype(x.dtype)


def _l2_norm(x):
    xf = x.astype(jnp.float32)
    return xf * lax.rsqrt(jnp.sum(xf * xf, axis=-1, keepdims=True) + NORM_EPS)


def _rope(t, pos):
    half = t.shape[-1] // 2
    inv_freq = ROPE_THETA ** (-jnp.arange(half, dtype=jnp.float32) / half)
    ang = pos.astype(jnp.float32)[:, :, None, None] * inv_freq
    cos, sin = jnp.cos(ang), jnp.sin(ang)
    tf = t.astype(jnp.float32)
    t1, t2 = tf[..., :half], tf[..., half:]
    return jnp.concatenate([t1 * cos - t2 * sin, t2 * cos + t1 * sin], axis=-1).astype(t.dtype)


def _causal_dwconv(x, w):
    k_size, seq = w.shape[0], x.shape[1]
    xp = jnp.pad(x, ((0, 0), (k_size - 1, 0), (0, 0)))
    y = xp[:, 0:seq] * w[0]
    for i in range(1, k_size):
        y = y + xp[:, i:i + seq] * w[i]
    return y


def _chunk_causal_softmax(scores, q_start):
    n_keys = scores.shape[-1]
    q_chunk = (q_start + jnp.arange(Q_BLOCK)) // CHUNK
    k_chunk = jnp.arange(n_keys) // CHUNK
    allowed = k_chunk[None, :] <= q_chunk[:, None]
    return jax.nn.softmax(jnp.where(allowed, scores.astype(jnp.float32), -jnp.inf), axis=-1)


def _sweep_query_blocks(body, *qs):
    b, h, s = qs[0].shape[:3]
    nb = s // Q_BLOCK
    blocks = tuple(jnp.moveaxis(q.reshape(b, h, nb, Q_BLOCK, q.shape[-1]), 2, 0) for q in qs)
    out = lax.map(lambda a: body(a[0] * Q_BLOCK, *a[1]), (jnp.arange(nb), blocks))
    return jnp.moveaxis(out, 0, 2).reshape(b, h, s, out.shape[-1])


def _diff_attention(q, k, v, qn_w, kn_w, lam_p, subln_w, lambda_init):
    b, s = v.shape[:2]
    q = _rms_norm(q.reshape(b, s, DIFF_HEADS, 2, DIFF_QK_DIM), qn_w).transpose(0, 2, 3, 1, 4)
    k = _rms_norm(k.reshape(b, s, DIFF_HEADS, 2, DIFF_QK_DIM), kn_w).transpose(0, 2, 3, 1, 4)
    v = v.reshape(b, s, DIFF_HEADS, DIFF_V_DIM).transpose(0, 2, 1, 3)
    lp = lam_p.astype(jnp.float32)
    lam = jnp.exp(jnp.sum(lp[0] * lp[1])) - jnp.exp(jnp.sum(lp[2] * lp[3])) + lambda_init
    k1, k2 = k[:, :, 0], k[:, :, 1]
    scale = DIFF_QK_DIM ** -0.5

    def body(q_start, q1, q2):
        s1 = jnp.einsum('bhqd,bhkd->bhqk', q1, k1, preferred_element_type=jnp.float32) * scale
        s2 = jnp.einsum('bhqd,bhkd->bhqk', q2, k2, preferred_element_type=jnp.float32) * scale
        p = _chunk_causal_softmax(s1, q_start) - lam * _chunk_causal_softmax(s2, q_start)
        return jnp.einsum('bhqk,bhkd->bhqd', p.astype(v.dtype), v)

    o = _sweep_query_blocks(body, q[:, :, 0], q[:, :, 1])
    o = _rms_norm(o, subln_w) * (1.0 - lambda_init)
    return o.transpose(0, 2, 1, 3).reshape(b, s, DIFF_HEADS * DIFF_V_DIM)


def _mla(q_lat, kv_lat, pos, qa_w, q_up, kva_w, kv_up, qn_w, kn_w):
    b, s = q_lat.shape[:2]
    qk_dim = MLA_NOPE + MLA_ROPE
    q = (_rms_norm(q_lat, qa_w) @ q_up).reshape(b, s, MLA_HEADS, qk_dim)
    c_kv, k_rope = kv_lat[..., :MLA_KV_LORA], kv_lat[..., MLA_KV_LORA:]
    kv = (_rms_norm(c_kv, kva_w) @ kv_up).reshape(b, s, MLA_HEADS, MLA_NOPE + MLA_V_DIM)
    k_nope, v = kv[..., :MLA_NOPE], kv[..., MLA_NOPE:]
    k = jnp.concatenate(
        [k_nope, jnp.broadcast_to(k_rope[:, :, None, :], (b, s, MLA_HEADS, MLA_ROPE))], axis=-1)
    q = _rms_norm(q, qn_w)
    k = _rms_norm(k, kn_w)
    q = jnp.concatenate([q[..., :MLA_NOPE], _rope(q[..., MLA_NOPE:], pos)], axis=-1).transpose(0, 2, 1, 3)
    k = jnp.concatenate([k[..., :MLA_NOPE], _rope(k[..., MLA_NOPE:], pos)], axis=-1).transpose(0, 2, 1, 3)
    v = v.transpose(0, 2, 1, 3)
    scale = qk_dim ** -0.5

    def body(q_start, qb):
        sc = jnp.einsum('bhqd,bhkd->bhqk', qb, k, preferred_element_type=jnp.float32) * scale
        p = _chunk_causal_softmax(sc, q_start)
        return jnp.einsum('bhqk,bhkd->bhqd', p.astype(v.dtype), v)

    o = _sweep_query_blocks(body, q)
    return o.transpose(0, 2, 1, 3).reshape(b, s, MLA_HEADS * MLA_V_DIM)


def _chunk_gated_delta_rule(q, k, v, g, beta):
    b, s, h, dk = q.shape
    dv = v.shape[-1]
    n = s // CHUNK

    def chunks(t):
        t = t.reshape(b, n, CHUNK, h, *t.shape[3:])
        return jnp.moveaxis(jnp.swapaxes(t, 2, 3), 1, 0)

    q, k, v, g, beta = (chunks(t) for t in (q, k, v, g, beta))
    g = jnp.cumsum(g, axis=-1)
    idx = jnp.arange(CHUNK)
    incl = idx[:, None] >= idx[None, :]
    strict = idx[:, None] > idx[None, :]
    decay = jnp.exp(jnp.where(incl, g[..., :, None] - g[..., None, :], -jnp.inf))
    k_beta = k * beta[..., None]
    lower = jnp.where(strict, jnp.einsum('nbhid,nbhjd->nbhij', k_beta, k) * decay, 0.0)
    eye = jnp.eye(CHUNK, dtype=jnp.float32)
    t_mat = lax.linalg.triangular_solve(lower + eye, jnp.broadcast_to(eye, lower.shape),
                                        left_side=True, lower=True, unit_diagonal=True)
    u = t_mat @ (v * beta[..., None])
    w = t_mat @ (k_beta * jnp.exp(g)[..., None])
    attn = jnp.where(incl, jnp.einsum('nbhid,nbhjd->nbhij', q, k) * decay, 0.0)

    def step(state, inp):
        q_n, k_n, u_n, w_n, g_n, a_n = inp
        v_new = u_n - jnp.einsum('bhcd,bhde->bhce', w_n, state)
        o = (jnp.einsum('bhcd,bhde->bhce', q_n * jnp.exp(g_n)[..., None], state)
             + jnp.einsum('bhij,bhje->bhie', a_n, v_new))
        g_last = g_n[..., -1]
        state = (state * jnp.exp(g_last)[..., None, None]
                 + jnp.einsum('bhcd,bhce->bhde', k_n * jnp.exp(g_last[..., None] - g_n)[..., None], v_new))
        return state, o

    state0 = jnp.zeros((b, h, dk, dv), jnp.float32)
    _, o = lax.scan(step, state0, (q, k, u, w, g, attn))
    return jnp.swapaxes(jnp.moveaxis(o, 0, 1), 2, 3).reshape(b, s, h, dv)


def _gated_deltanet(q, k, v, z, beta_logit, a, conv_w, a_log, dt_bias, onorm_w):
    b, s = q.shape[:2]
    qk_w = GDN_HEADS * GDN_K_DIM
    qkv = jax.nn.silu(_causal_dwconv(jnp.concatenate([q, k, v], axis=-1), conv_w))
    q, k, v = jnp.split(qkv, [qk_w, 2 * qk_w], axis=-1)
    q = _l2_norm(q.reshape(b, s, GDN_HEADS, GDN_K_DIM)) * GDN_K_DIM ** -0.5
    k = _l2_norm(k.reshape(b, s, GDN_HEADS, GDN_K_DIM))
    v = v.reshape(b, s, GDN_HEADS, GDN_V_DIM).astype(jnp.float32)
    beta = jax.nn.sigmoid(beta_logit.astype(jnp.float32))
    g = -jnp.exp(a_log.astype(jnp.float32)) * jax.nn.softplus(a.astype(jnp.float32) + dt_bias.astype(jnp.float32))
    o = _chunk_gated_delta_rule(q, k, v, g, beta)
    o = _rms_norm(o, onorm_w) * jax.nn.silu(z.reshape(b, s, GDN_HEADS, GDN_V_DIM).astype(jnp.float32))
    return o.reshape(b, s, GDN_HEADS * GDN_V_DIM).astype(z.dtype)


def _swiglu(h, w_gu, w_down):
    gate, up = jnp.split(h @ w_gu, 2, axis=-1)
    return (jax.nn.silu(gate) * up) @ w_down


def setup_inputs(seed: int = 0) -> dict:
    key = jax.random.key(seed)
    keys = list(jax.random.split(key, 40))
    f32 = jnp.float32

    def normal(shape, fan_in, gain=1.0):
        return jax.random.normal(keys.pop(), shape, f32) * (gain * fan_in ** -0.5)

    def gain(d):
        return 1.0 + 0.1 * jax.random.normal(keys.pop(), (DEPTH, d), f32)

    offsets = jax.random.randint(keys.pop(), (BATCH, 1), 0, 64) * CHUNK
    positions = (offsets + jnp.arange(SEQ)[None, :]).astype(jnp.int32)
    dt = jnp.exp(jax.random.uniform(keys.pop(), (DEPTH, GDN_HEADS), f32,
                                    minval=math.log(1e-3), maxval=math.log(1e-1)))
    return {
        'x': jax.random.normal(keys.pop(), (BATCH, SEQ, D_MODEL), f32),
        'c': jax.random.normal(keys.pop(), (BATCH, D_MODEL), f32),
        'positions': positions,
        'ada_w': normal((DEPTH, D_MODEL, 6 * D_MODEL), D_MODEL, 0.5),
        'ada_b': 0.02 * jax.random.normal(keys.pop(), (DEPTH, 6 * D_MODEL), f32),
        'norm1_w': gain(D_MODEL),
        'w_in': normal((DEPTH, D_MODEL, IN_COLS), D_MODEL),
        'diff_qnorm_w': gain(DIFF_QK_DIM),
        'diff_knorm_w': gain(DIFF_QK_DIM),
        'diff_lambda': 0.1 * jax.random.normal(keys.pop(), (DEPTH, 4, DIFF_QK_DIM), f32),
        'diff_subln_w': gain(DIFF_V_DIM),
        'w_o_diff': normal((DEPTH, DIFF_HEADS * DIFF_V_DIM, D_MODEL), DIFF_HEADS * DIFF_V_DIM),
        'mla_qa_norm_w': gain(MLA_Q_LORA),
        'mla_q_up': normal((DEPTH, MLA_Q_LORA, MLA_HEADS * (MLA_NOPE + MLA_ROPE)), MLA_Q_LORA),
        'mla_kva_norm_w': gain(MLA_KV_LORA),
        'mla_kv_up': normal((DEPTH, MLA_KV_LORA, MLA_HEADS * (MLA_NOPE + MLA_V_DIM)), MLA_KV_LORA),
        'mla_qnorm_w': gain(MLA_NOPE + MLA_ROPE),
        'mla_knorm_w': gain(MLA_NOPE + MLA_ROPE),
        'w_o_mla': normal((DEPTH, MLA_HEADS * MLA_V_DIM, D_MODEL), MLA_HEADS * MLA_V_DIM),
        'gdn_conv_w': normal((DEPTH, GDN_CONV, GDN_HEADS * (2 * GDN_K_DIM + GDN_V_DIM)), GDN_CONV),
        'gdn_a_log': jnp.log(jax.random.uniform(keys.pop(), (DEPTH, GDN_HEADS), f32, minval=1.0, maxval=16.0)),
        'gdn_dt_bias': dt + jnp.log(-jnp.expm1(-dt)),
        'gdn_onorm_w': gain(GDN_V_DIM),
        'w_o_gdn': normal((DEPTH, GDN_HEADS * GDN_V_DIM, D_MODEL), GDN_HEADS * GDN_V_DIM),
        'w_out': normal((DEPTH, D_MODEL, D_MODEL), D_MODEL),
        'norm2_w': gain(D_MODEL),
        'ffn_w_gu': normal((DEPTH, D_MODEL, 2 * FFN_HIDDEN), D_MODEL),
        'ffn_w_down': normal((DEPTH, FFN_HIDDEN, D_MODEL), FFN_HIDDEN),
    }


def reference(x, c, positions, ada_w, ada_b, norm1_w, w_in, diff_qnorm_w, diff_knorm_w,
              diff_lambda, diff_subln_w, w_o_diff, mla_qa_norm_w, mla_q_up, mla_kva_norm_w,
              mla_kv_up, mla_qnorm_w, mla_knorm_w, w_o_mla, gdn_conv_w, gdn_a_log, gdn_dt_bias,
              gdn_onorm_w, w_o_gdn, w_out, norm2_w, ffn_w_gu, ffn_w_down):
    for l in range(DEPTH):
        lambda_init = 0.8 - 0.6 * math.exp(-0.3 * l)
        mod = (c @ ada_w[l] + ada_b[l])[:, None, :]
        sh1, sc1, g1, sh2, sc2, g2 = jnp.split(mod, 6, axis=-1)

        h = _rms_norm(x, norm1_w[l]) * (1.0 + sc1) + sh1
        (a_q, a_k, a_v, b_q, b_kv, c_q, c_k, c_v, c_z, c_b, c_a,
         gate_logits) = jnp.split(h @ w_in[l], IN_OFFSETS, axis=-1)
        y_a = _diff_attention(a_q, a_k, a_v, diff_qnorm_w[l], diff_knorm_w[l], diff_lambda[l],
                              diff_subln_w[l], lambda_init) @ w_o_diff[l]
        y_b = _mla(b_q, b_kv, positions, mla_qa_norm_w[l], mla_q_up[l], mla_kva_norm_w[l],
                   mla_kv_up[l], mla_qnorm_w[l], mla_knorm_w[l]) @ w_o_mla[l]
        y_c = _gated_deltanet(c_q, c_k, c_v, c_z, c_b, c_a, gdn_conv_w[l], gdn_a_log[l],
                              gdn_dt_bias[l], gdn_onorm_w[l]) @ w_o_gdn[l]
        gate_a, gate_b, gate_c = jnp.split(jax.nn.sigmoid(gate_logits), N_BRANCHES, axis=-1)
        merged = gate_a * y_a + gate_b * y_b + gate_c * y_c
        x = x + g1 * (merged @ w_out[l])

        h = _rms_norm(x, norm2_w[l]) * (1.0 + sc2) + sh2
        x = x + g2 * _swiglu(h, ffn_w_gu[l], ffn_w_down[l])
    return x
```

```python
import functools
import math

import jax
import jax.numpy as jnp
import numpy as np
from jax import lax
from jax.experimental import pallas as pl
from jax.experimental.pallas import tpu as pltpu

F32 = jnp.float32
BF16 = jnp.bfloat16

CHUNK = 64
NORM_EPS = 1e-6
ROPE_THETA = 10000.0
DIFF_HEADS = 4
DIFF_QK_DIM = 64
DIFF_V_DIM = 128
MLA_HEADS = 4
MLA_Q_LORA = 384
MLA_KV_LORA = 256
MLA_NOPE = 64
MLA_ROPE = 32
MLA_V_DIM = 128
GDN_HEADS = 4
GDN_K_DIM = 128
GDN_V_DIM = 128
GDN_CONV = 4
N_BRANCHES = 3

LANES = 128
SUBLANES = 8
HEAD_W = 128
LOG2E = 1.4426950408889634
NEG_BIG = -1e30
VMEM_LIMIT = 56 * 1024 * 1024


def _cparams(*sem):
    return pltpu.CompilerParams(dimension_semantics=sem, vmem_limit_bytes=VMEM_LIMIT)


def _mm(a, b):
    return jnp.dot(a.astype(BF16), b.astype(BF16), preferred_element_type=F32)


def _mm_nt(a, b):
    return lax.dot_general(a.astype(BF16), b.astype(BF16), (((1,), (1,)), ((), ())),
                           preferred_element_type=F32)


def _mod_kernel(c_ref, w_ref, b_ref, o_ref):
    o_ref[...] = jnp.dot(c_ref[...], w_ref[...], preferred_element_type=F32) + b_ref[...]


def _ada_mod(c_pad, ada_w, ada_b):
    depth, d, n6 = ada_w.shape
    tn = 1536
    return pl.pallas_call(
        _mod_kernel,
        grid=(depth, n6 // tn),
        in_specs=[pl.BlockSpec((SUBLANES, d), lambda l, j: (0, 0)),
                  pl.BlockSpec((None, d, tn), lambda l, j: (l, 0, j)),
                  pl.BlockSpec((None, 1, tn), lambda l, j: (l, 0, j))],
        out_specs=pl.BlockSpec((None, SUBLANES, tn), lambda l, j: (l, 0, j)),
        out_shape=jax.ShapeDtypeStruct((depth, SUBLANES, n6), F32),
        compiler_params=_cparams("parallel", "parallel"),
        name="ada_mod",
    )(c_pad, ada_w, ada_b.reshape(depth, 1, n6))


def _norm_mod(x, nw, mod, shift_row, scale_row):
    r = lax.rsqrt(jnp.mean(x * x, axis=-1, keepdims=True) + NORM_EPS)
    h = x * r * nw
    return h * (1.0 + mod[scale_row:scale_row + 1, :]) + mod[shift_row:shift_row + 1, :]


def _in_proj_kernel(x_ref, nw_ref, mod_ref, w_ref, o_ref, h_ref):
    @pl.when(pl.program_id(1) == 0)
    def _():
        h_ref[...] = _norm_mod(x_ref[...], nw_ref[...], mod_ref[...], 0, 1).astype(BF16)

    o_ref[...] = jnp.dot(h_ref[...], w_ref[...], preferred_element_type=F32).astype(o_ref.dtype)


def _in_proj(x2, nw, mod, w, seq, tm, tn):
    n, d = x2.shape
    cols = w.shape[1]
    tiles_per_batch = seq // tm
    return pl.pallas_call(
        _in_proj_kernel,
        grid=(n // tm, cols // tn),
        in_specs=[pl.BlockSpec((tm, d), lambda i, j: (i, 0)),
                  pl.BlockSpec((1, d), lambda i, j: (0, 0)),
                  pl.BlockSpec((None, 6, d), lambda i, j: (i // tiles_per_batch, 0, 0)),
                  pl.BlockSpec((d, tn), lambda i, j: (0, j))],
        out_specs=pl.BlockSpec((tm, tn), lambda i, j: (i, j)),
        out_shape=jax.ShapeDtypeStruct((n, cols), BF16),
        scratch_shapes=[pltpu.VMEM((tm, d), BF16)],
        compiler_params=_cparams("parallel", "arbitrary"),
        name="in_proj",
    )(x2, nw, mod, w)


def _half_rms(t, w):
    lane = lax.broadcasted_iota(jnp.int32, t.shape, 1)
    lo = lane < DIFF_QK_DIM
    sq = t * t
    s_lo = jnp.sum(jnp.where(lo, sq, 0.0), axis=-1, keepdims=True)
    s_hi = jnp.sum(jnp.where(lo, 0.0, sq), axis=-1, keepdims=True)
    r = lax.rsqrt(jnp.where(lo, s_lo, s_hi) * (1.0 / DIFF_QK_DIM) + NORM_EPS)
    return t * r * w


def _diff_prep_kernel(q_ref, k_ref, qw_ref, kw_ref, qo_ref, ko_ref):
    qscale = (DIFF_QK_DIM ** -0.5) * LOG2E
    for h in range(DIFF_HEADS):
        sl = slice(h * HEAD_W, (h + 1) * HEAD_W)
        qo_ref[:, sl] = (_half_rms(q_ref[:, sl].astype(F32), qw_ref[...]) * qscale).astype(BF16)
        ko_ref[:, sl] = _half_rms(k_ref[:, sl].astype(F32), kw_ref[...]).astype(BF16)


def _diff_prep(proj, qw2, kw2, col_q, col_k, tm):
    n = proj.shape[0]
    w = DIFF_HEADS * HEAD_W
    return pl.pallas_call(
        _diff_prep_kernel,
        grid=(n // tm,),
        in_specs=[pl.BlockSpec((tm, w), lambda i: (i, col_q // w)),
                  pl.BlockSpec((tm, w), lambda i: (i, col_k // w)),
                  pl.BlockSpec((1, HEAD_W), lambda i: (0, 0)),
                  pl.BlockSpec((1, HEAD_W), lambda i: (0, 0))],
        out_specs=[pl.BlockSpec((tm, w), lambda i: (i, 0)),
                   pl.BlockSpec((tm, w), lambda i: (i, 0))],
        out_shape=[jax.ShapeDtypeStruct((n, w), BF16), jax.ShapeDtypeStruct((n, w), BF16)],
        compiler_params=_cparams("parallel"),
        name="diff_prep",
    )(proj, proj, qw2, kw2)


def _rope_tile(t, cos_t, sin_lo, sin_hi):
    half = MLA_ROPE // 2
    return (t * cos_t + pltpu.roll(t, HEAD_W - half, 1) * sin_lo + pltpu.roll(t, half, 1) * sin_hi)


def _mla_prep_kernel(q_ref, kv_ref, pos_ref, qa_ref, qup_ref, kva_ref, wk_ref, wv_ref,
                     qn_ref, kn_ref, invf_ref, qo_ref, ko_ref, vo_ref):
    qk_dim = MLA_NOPE + MLA_ROPE
    qscale = (qk_dim ** -0.5) * LOG2E
    half = MLA_ROPE // 2

    xq = q_ref[...].astype(F32)
    rq = lax.rsqrt(jnp.mean(xq * xq, axis=-1, keepdims=True) + NORM_EPS)
    q = _mm(xq * rq * qa_ref[...], qup_ref[...])

    xkv = kv_ref[...].astype(F32)
    lane_kv = lax.broadcasted_iota(jnp.int32, xkv.shape, 1)
    is_lat = lane_kv < MLA_KV_LORA
    ssq = jnp.sum(jnp.where(is_lat, xkv * xkv, 0.0), axis=-1, keepdims=True)
    rkv = lax.rsqrt(ssq * (1.0 / MLA_KV_LORA) + NORM_EPS)
    lhs = jnp.where(is_lat, xkv * rkv * kva_ref[...], xkv).astype(BF16)
    k = jnp.dot(lhs, wk_ref[...], preferred_element_type=F32)
    vo_ref[...] = jnp.dot(lhs, wv_ref[...], preferred_element_type=F32).astype(BF16)

    ang = pos_ref[...].astype(F32) * invf_ref[...]
    lane = lax.broadcasted_iota(jnp.int32, ang.shape, 1)
    in_lo = (lane >= MLA_NOPE) & (lane < MLA_NOPE + half)
    in_hi = (lane >= MLA_NOPE + half) & (lane < qk_dim)
    cos_a, sin_a = jnp.cos(ang), jnp.sin(ang)
    cos_t = jnp.where(in_lo | in_hi, cos_a, 1.0)
    sin_lo = jnp.where(in_lo, -sin_a, 0.0)
    sin_hi = jnp.where(in_hi, sin_a, 0.0)

    for h in range(MLA_HEADS):
        sl = slice(h * HEAD_W, (h + 1) * HEAD_W)
        qh = q[:, sl]
        qh = qh * lax.rsqrt(jnp.sum(qh * qh, axis=-1, keepdims=True) * (1.0 / qk_dim) + NORM_EPS)
        qh = _rope_tile(qh * qn_ref[...], cos_t, sin_lo, sin_hi)
        qo_ref[:, sl] = (qh * qscale).astype(BF16)
        kh = k[:, sl]
        kh = kh * lax.rsqrt(jnp.sum(kh * kh, axis=-1, keepdims=True) * (1.0 / qk_dim) + NORM_EPS)
        kh = _rope_tile(kh * kn_ref[...], cos_t, sin_lo, sin_hi)
        ko_ref[:, sl] = kh.astype(BF16)


def _mla_prep(proj, pos2, qa_w, qup, kva_w, wk, wv, qn_w, kn_w, invf, col_q, col_kv, tm):
    n = proj.shape[0]
    w = MLA_HEADS * HEAD_W
    lat = MLA_Q_LORA
    const = lambda i: (0, 0)
    out = jax.ShapeDtypeStruct((n, w), BF16)
    return pl.pallas_call(
        _mla_prep_kernel,
        grid=(n // tm,),
        in_specs=[pl.BlockSpec((tm, lat), lambda i: (i, col_q // lat)),
                  pl.BlockSpec((tm, lat), lambda i: (i, col_kv // lat)),
                  pl.BlockSpec((tm, 1), lambda i: (i, 0)),
                  pl.BlockSpec((1, lat), const),
                  pl.BlockSpec((lat, w), const),
                  pl.BlockSpec((1, lat), const),
                  pl.BlockSpec((lat, w), const),
                  pl.BlockSpec((lat, w), const),
                  pl.BlockSpec((1, HEAD_W), const),
                  pl.BlockSpec((1, HEAD_W), const),
                  pl.BlockSpec((1, HEAD_W), const)],
        out_specs=[pl.BlockSpec((tm, w), lambda i: (i, 0))] * 3,
        out_shape=[out, out, out],
        compiler_params=_cparams("parallel"),
        name="mla_prep",
    )(proj, proj, pos2, qa_w, qup, kva_w, wk, wv, qn_w, kn_w, invf)


def _attn_kernel(*refs, tq, tk, n_maps, lambda_init):
    if n_maps == 2:
        q_ref, k_ref, v_ref, lam_ref, sw_ref, o_ref, m_ref, l_ref, acc_ref = refs
    else:
        q_ref, k_ref, v_ref, o_ref, m_ref, l_ref, acc_ref = refs
    i = pl.program_id(2)
    rows = n_maps * tq
    q = q_ref[...]
    if n_maps == 2:
        lane = lax.broadcasted_iota(jnp.int32, q.shape, 1)
        zero = jnp.zeros_like(q)
        qq = jnp.concatenate([jnp.where(lane < DIFF_QK_DIM, q, zero),
                              jnp.where(lane >= DIFF_QK_DIM, q, zero)], axis=0)
    else:
        qq = q

    m_ref[...] = jnp.full(m_ref.shape, NEG_BIG, F32)
    l_ref[...] = jnp.zeros(l_ref.shape, F32)
    acc_ref[...] = jnp.zeros(acc_ref.shape, F32)

    def step(j, masked):
        start = pl.multiple_of(j * tk, tk)
        k = k_ref[pl.ds(start, tk), :]
        v = v_ref[pl.ds(start, tk), :]
        s = lax.dot_general(qq, k, (((1,), (1,)), ((), ())), preferred_element_type=F32)
        if masked:
            r_idx = lax.broadcasted_iota(jnp.int32, (rows, tk), 0)
            c_idx = lax.broadcasted_iota(jnp.int32, (rows, tk), 1)
            q_chunk = (i * tq + r_idx % tq) // CHUNK
            k_chunk = (j * tk + c_idx) // CHUNK
            s = jnp.where(k_chunk <= q_chunk, s, NEG_BIG)
        m_prev = m_ref[...]
        m_new = jnp.maximum(m_prev, jnp.max(s, axis=-1, keepdims=True))
        alpha = jnp.exp2(m_prev - m_new)
        p = jnp.exp2(s - m_new)
        l_ref[...] = alpha * l_ref[...] + jnp.sum(p, axis=-1, keepdims=True)
        acc_ref[...] = alpha * acc_ref[...] + jnp.dot(p.astype(BF16), v, preferred_element_type=F32)
        m_ref[...] = m_new

    n_full = (i * tq) // tk

    def body(j, carry):
        step(j, False)
        return carry

    lax.fori_loop(0, n_full, body, 0)
    for d in range(max(tq // tk, 1)):
        step(n_full + d, True)

    o = acc_ref[...] / l_ref[...]
    if n_maps == 2:
        lp = lam_ref[...]
        lam = (jnp.exp(jnp.sum(lp[0:1] * lp[1:2], axis=-1, keepdims=True))
               - jnp.exp(jnp.sum(lp[2:3] * lp[3:4], axis=-1, keepdims=True)) + lambda_init)
        o = o[:tq] - lam * o[tq:]
        r = lax.rsqrt(jnp.mean(o * o, axis=-1, keepdims=True) + NORM_EPS)
        o = o * r * sw_ref[...] * (1.0 - lambda_init)
    o_ref[...] = o.astype(o_ref.dtype)


def _attention(q, k, v, extra, *, heads, batch, seq, tq, tk, n_maps, v_col=0, lambda_init=0.0):
    q3, k3, v3 = (t.reshape(batch, seq, t.shape[1]) for t in (q, k, v))
    rows = n_maps * tq
    vb = v_col // HEAD_W
    in_specs = [pl.BlockSpec((None, tq, HEAD_W), lambda b, h, i: (b, i, h)),
                pl.BlockSpec((None, seq, HEAD_W), lambda b, h, i: (b, 0, h)),
                pl.BlockSpec((None, seq, HEAD_W), lambda b, h, i: (b, 0, vb + h))]
    for e in extra:
        in_specs.append(pl.BlockSpec(e.shape, lambda b, h, i: (0, 0)))
    out = pl.pallas_call(
        functools.partial(_attn_kernel, tq=tq, tk=tk, n_maps=n_maps, lambda_init=lambda_init),
        grid=(batch, heads, seq // tq),
        in_specs=in_specs,
        out_specs=pl.BlockSpec((None, tq, HEAD_W), lambda b, h, i: (b, i, h)),
        out_shape=jax.ShapeDtypeStruct((batch, seq, heads * HEAD_W), BF16),
        scratch_shapes=[pltpu.VMEM((rows, 1), F32), pltpu.VMEM((rows, 1), F32),
                        pltpu.VMEM((rows, HEAD_W), F32)],
        compiler_params=_cparams("parallel", "parallel", "arbitrary"),
        name="diff_attn" if n_maps == 2 else "mla_attn",
    )(q3, k3, v3, *extra)
    return out.reshape(batch * seq, heads * HEAD_W)


def _gdn_prep_kernel(x_ref, halo_ref, ba_ref, cw_ref, ea_ref, dtb_ref, qkv_ref, gb_ref, *, tiles_per_batch):
    i = pl.program_id(0)
    x = x_ref[...].astype(F32)
    halo = halo_ref[...].astype(F32)
    halo = jnp.where(i % tiles_per_batch == 0, 0.0, halo)
    cw = cw_ref[...]
    row8 = lax.broadcasted_iota(jnp.int32, halo.shape, 0)
    y = x * cw[GDN_CONV - 1:GDN_CONV]
    for s in range(1, GDN_CONV):
        xs = pltpu.roll(x, s, 0)
        hs = pltpu.roll(halo, s, 0)
        first = jnp.where(row8 < s, hs, xs[:SUBLANES])
        xs = jnp.concatenate([first, xs[SUBLANES:]], axis=0)
        y = y + xs * cw[GDN_CONV - 1 - s:GDN_CONV - s]
    y = y * jax.nn.sigmoid(y)
    nqk = 2 * GDN_HEADS
    for h in range(3 * GDN_HEADS):
        sl = slice(h * HEAD_W, (h + 1) * HEAD_W)
        t = y[:, sl]
        if h < nqk:
            t = t * lax.rsqrt(jnp.sum(t * t, axis=-1, keepdims=True) + NORM_EPS)
            if h < GDN_HEADS:
                t = t * (GDN_K_DIM ** -0.5)
        qkv_ref[:, sl] = t
    ba = ba_ref[...].astype(F32)
    lane = lax.broadcasted_iota(jnp.int32, ba.shape, 1)
    sp_in = ba + dtb_ref[...]
    softplus = jnp.maximum(sp_in, 0.0) + jnp.log1p(jnp.exp(-jnp.abs(sp_in)))
    gb_ref[...] = jnp.where(lane < GDN_HEADS, jax.nn.sigmoid(ba), -jnp.exp(ea_ref[...]) * softplus)


def _gdn_prep(proj, conv_w, exp_a, dt_b, col_qkv, col_ba, seq, tm):
    n = proj.shape[0]
    w = 3 * GDN_HEADS * HEAD_W
    tiles_per_batch = seq // tm
    hb = tm // SUBLANES
    return pl.pallas_call(
        functools.partial(_gdn_prep_kernel, tiles_per_batch=tiles_per_batch),
        grid=(n // tm,),
        in_specs=[pl.BlockSpec((tm, w), lambda i: (i, col_qkv // w)),
                  pl.BlockSpec((SUBLANES, w), lambda i: (jnp.maximum(i * hb - 1, 0), col_qkv // w)),
                  pl.BlockSpec((tm, LANES), lambda i: (i, col_ba // LANES)),
                  pl.BlockSpec((GDN_CONV, w), lambda i: (0, 0)),
                  pl.BlockSpec((1, LANES), lambda i: (0, 0)),
                  pl.BlockSpec((1, LANES), lambda i: (0, 0))],
        out_specs=[pl.BlockSpec((tm, w), lambda i: (i, 0)),
                   pl.BlockSpec((tm, LANES), lambda i: (i, 0))],
        out_shape=[jax.ShapeDtypeStruct((n, w), F32), jax.ShapeDtypeStruct((n, LANES), F32)],
        compiler_params=_cparams("parallel"),
        name="gdn_prep",
    )(proj, proj, proj, conv_w, exp_a, dt_b)


def _unit_lower_inverse(lmat, row, col):
    eye = (row == col).astype(F32)
    same2 = (row // 2) == (col // 2)
    t = eye - jnp.where(same2, lmat, 0.0)
    b = 2
    while b < CHUNK:
        off = ((row // (2 * b)) == (col // (2 * b))) & ((row // b) % 2 == 1) & ((col // b) % 2 == 0)
        c = jnp.where(off, lmat, 0.0)
        t = t - _mm(t, _mm(c, t))
        b *= 2
    return t


def _gdn_kernel(qkv_ref, gb_ref, z_ref, ow_ref, o_ref, state_ref, *, tc):
    @pl.when(pl.program_id(1) == 0)
    def _():
        state_ref[...] = jnp.zeros(state_ref.shape, F32)

    hw = GDN_HEADS * HEAD_W
    gb = gb_ref[...]
    r_in_chunk = lax.broadcasted_iota(jnp.int32, gb.shape, 0) % CHUNK
    gc = gb
    sh = 1
    while sh < CHUNK:
        gc = gc + jnp.where(r_in_chunk >= sh, pltpu.roll(gc, sh, 0), 0.0)
        sh *= 2
    gc_t = gc.T

    row = lax.broadcasted_iota(jnp.int32, (CHUNK, CHUNK), 0)
    col = lax.broadcasted_iota(jnp.int32, (CHUNK, CHUNK), 1)
    incl = row >= col
    strict = row > col

    for c in range(tc // CHUNK):
        rs = slice(c * CHUNK, (c + 1) * CHUNK)
        for h in range(GDN_HEADS):
            q = qkv_ref[rs, h * HEAD_W:(h + 1) * HEAD_W]
            k = qkv_ref[rs, hw + h * HEAD_W:hw + (h + 1) * HEAD_W]
            v = qkv_ref[rs, 2 * hw + h * HEAD_W:2 * hw + (h + 1) * HEAD_W]
            beta = gb[rs, h:h + 1]
            g_col = gc[rs, GDN_HEADS + h:GDN_HEADS + h + 1]
            g_row = gc_t[GDN_HEADS + h:GDN_HEADS + h + 1, rs]
            g_last = g_col[CHUNK - 1:CHUNK, :]
            decay = jnp.exp(jnp.where(incl, g_col - g_row, NEG_BIG))
            k_beta = k * beta
            lmat = jnp.where(strict, _mm_nt(k_beta, k) * decay, 0.0)
            t_mat = _unit_lower_inverse(lmat, row, col)
            uw = _mm(t_mat, jnp.concatenate([v * beta, k_beta * jnp.exp(g_col)], axis=1))
            u, w = uw[:, :HEAD_W], uw[:, HEAD_W:]
            attn = jnp.where(incl, _mm_nt(q, k) * decay, 0.0)
            state = state_ref[h]
            v_new = u - _mm(w, state)
            o = _mm(q * jnp.exp(g_col), state) + _mm(attn, v_new)
            k_dec = (k * jnp.exp(g_last - g_col)).T
            state_ref[h] = state * jnp.exp(g_last) + _mm(k_dec, v_new)
            o = o * lax.rsqrt(jnp.mean(o * o, axis=-1, keepdims=True) + NORM_EPS) * ow_ref[...]
            z = z_ref[rs, h * HEAD_W:(h + 1) * HEAD_W].astype(F32)
            o_ref[rs, h * HEAD_W:(h + 1) * HEAD_W] = (o * (z * jax.nn.sigmoid(z))).astype(o_ref.dtype)


def _gdn(qkv, gb, proj, onorm_w, col_z, batch, seq, tc):
    hw = GDN_HEADS * HEAD_W
    nblk = seq // tc
    return pl.pallas_call(
        functools.partial(_gdn_kernel, tc=tc),
        grid=(batch, nblk),
        in_specs=[pl.BlockSpec((tc, 3 * hw), lambda b, i: (b * nblk + i, 0)),
                  pl.BlockSpec((tc, LANES), lambda b, i: (b * nblk + i, 0)),
                  pl.BlockSpec((tc, hw), lambda b, i: (b * nblk + i, col_z // hw)),
                  pl.BlockSpec((1, HEAD_W), lambda b, i: (0, 0))],
        out_specs=pl.BlockSpec((tc, hw), lambda b, i: (b * nblk + i, 0)),
        out_shape=jax.ShapeDtypeStruct((batch * seq, hw), BF16),
        scratch_shapes=[pltpu.VMEM((GDN_HEADS, GDN_K_DIM, GDN_V_DIM), F32)],
        compiler_params=_cparams("parallel", "arbitrary"),
        name="gdn",
    )(qkv, gb, proj, onorm_w)


def _merge_kernel(x_ref, oa_ref, ob_ref, oc_ref, gate_ref, wa_ref, wb_ref, wc_ref, wout_ref, mod_ref, o_ref):
    d = x_ref.shape[1]
    merged = None
    for idx, (o_r, w_r) in enumerate(((oa_ref, wa_ref), (ob_ref, wb_ref), (oc_ref, wc_ref))):
        y = jnp.dot(o_r[...], w_r[...], preferred_element_type=F32)
        g = jax.nn.sigmoid(gate_ref[:, idx * d:(idx + 1) * d].astype(F32))
        merged = g * y if merged is None else merged + g * y
    upd = jnp.dot(merged.astype(BF16), wout_ref[...], preferred_element_type=F32)
    o_ref[...] = x_ref[...] + mod_ref[2:3, :] * upd


def _merge(x2, oa, ob, oc, proj, wa, wb, wc, wout, mod, col_gate, seq, tm):
    n, d = x2.shape
    w = DIFF_HEADS * HEAD_W
    tiles_per_batch = seq // tm
    row = lambda i: (i, 0)
    const = lambda i: (0, 0)
    return pl.pallas_call(
        _merge_kernel,
        grid=(n // tm,),
        in_specs=[pl.BlockSpec((tm, d), row), pl.BlockSpec((tm, w), row), pl.BlockSpec((tm, w), row),
                  pl.BlockSpec((tm, w), row),
                  pl.BlockSpec((tm, N_BRANCHES * d), lambda i: (i, col_gate // (N_BRANCHES * d))),
                  pl.BlockSpec((w, d), const), pl.BlockSpec((w, d), const), pl.BlockSpec((w, d), const),
                  pl.BlockSpec((d, d), const),
                  pl.BlockSpec((None, 6, d), lambda i: (i // tiles_per_batch, 0, 0))],
        out_specs=pl.BlockSpec((tm, d), row),
        out_shape=jax.ShapeDtypeStruct((n, d), F32),
        compiler_params=_cparams("parallel"),
        name="merge",
    )(x2, oa, ob, oc, proj, wa, wb, wc, wout, mod)


def _ffn_kernel(x_ref, nw_ref, mod_ref, wg_ref, wu_ref, wd_ref, o_ref, h_ref, acc_ref):
    j = pl.program_id(1)

    @pl.when(j == 0)
    def _():
        h_ref[...] = _norm_mod(x_ref[...], nw_ref[...], mod_ref[...], 3, 4).astype(BF16)
        acc_ref[...] = jnp.zeros(acc_ref.shape, F32)

    h = h_ref[...]
    gate = jnp.dot(h, wg_ref[...], preferred_element_type=F32)
    up = jnp.dot(h, wu_ref[...], preferred_element_type=F32)
    act = (gate * jax.nn.sigmoid(gate) * up).astype(BF16)
    acc_ref[...] += jnp.dot(act, wd_ref[...], preferred_element_type=F32)

    @pl.when(j == pl.num_programs(1) - 1)
    def _():
        o_ref[...] = x_ref[...] + mod_ref[5:6, :] * acc_ref[...]


def _ffn(x2, nw, mod, w_gu, w_down, seq, tm, th):
    n, d = x2.shape
    hidden = w_down.shape[0]
    nh = hidden // th
    tiles_per_batch = seq // tm
    return pl.pallas_call(
        _ffn_kernel,
        grid=(n // tm, nh),
        in_specs=[pl.BlockSpec((tm, d), lambda i, j: (i, 0)),
                  pl.BlockSpec((1, d), lambda i, j: (0, 0)),
                  pl.BlockSpec((None, 6, d), lambda i, j: (i // tiles_per_batch, 0, 0)),
                  pl.BlockSpec((d, th), lambda i, j: (0, j)),
                  pl.BlockSpec((d, th), lambda i, j: (0, nh + j)),
                  pl.BlockSpec((th, d), lambda i, j: (j, 0))],
        out_specs=pl.BlockSpec((tm, d), lambda i, j: (i, 0)),
        out_shape=jax.ShapeDtypeStruct((n, d), F32),
        scratch_shapes=[pltpu.VMEM((tm, d), BF16), pltpu.VMEM((tm, d), F32)],
        compiler_params=_cparams("parallel", "arbitrary"),
        name="ffn",
    )(x2, nw, mod, w_gu, w_gu, w_down)


def _pad_cols(w, width):
    return jnp.pad(w, ((0, 0), (0, width - w.shape[1])))


def _in_proj_layout(d_model):
    a = DIFF_HEADS * HEAD_W
    g = GDN_HEADS * HEAD_W
    cols = {}
    off = 0
    for name, width in (("gate", N_BRANCHES * d_model), ("c_qkv", 3 * g), ("a_q", a), ("a_k", a), ("a_v", a),
                        ("c_z", g), ("c_ba", LANES), ("b_q", MLA_Q_LORA), ("b_kv", MLA_Q_LORA)):
        if off % width:
            off += width - off % width
        cols[name] = off
        off += width
    return cols, off


def _build_w_in(w_in, d_model, total):
    a = DIFF_HEADS * 2 * DIFF_QK_DIM
    g = GDN_HEADS * GDN_K_DIM
    widths = (a, a, DIFF_HEADS * DIFF_V_DIM, MLA_Q_LORA, MLA_KV_LORA + MLA_ROPE, g, g, GDN_HEADS * GDN_V_DIM,
              GDN_HEADS * GDN_V_DIM, GDN_HEADS, GDN_HEADS, N_BRANCHES * d_model)
    offs = np.concatenate([[0], np.cumsum(widths)])
    part = [w_in[:, int(offs[i]):int(offs[i + 1])] for i in range(len(widths))]
    cols, _ = _in_proj_layout(d_model)
    out = jnp.zeros((w_in.shape[0], total), w_in.dtype)
    place = (("a_q", part[0]), ("a_k", part[1]), ("a_v", part[2]), ("b_q", part[3]), ("b_kv", part[4]),
             ("c_qkv", jnp.concatenate(part[5:8], axis=1)), ("c_z", part[8]), ("gate", part[11]),
             ("c_ba", jnp.concatenate(part[9:11], axis=1)))
    for name, p in place:
        out = lax.dynamic_update_slice(out, p, (0, cols[name]))
    return out.astype(BF16)


def _head_pad(w, heads, per_head, lane_off=0):
    r = w.shape[0]
    w = w.reshape(r, heads, per_head)
    w = jnp.pad(w, ((0, 0), (0, 0), (lane_off, HEAD_W - per_head - lane_off)))
    return w.reshape(r, heads * HEAD_W)


def kernel(x, c, positions, ada_w, ada_b, norm1_w, w_in, diff_qnorm_w, diff_knorm_w, diff_lambda, diff_subln_w, w_o_diff, mla_qa_norm_w, mla_q_up, mla_kva_norm_w, mla_kv_up, mla_qnorm_w, mla_knorm_w, w_o_mla, gdn_conv_w, gdn_a_log, gdn_dt_bias, gdn_onorm_w, w_o_gdn, w_out, norm2_w, ffn_w_gu, ffn_w_down):
    batch, seq, d = x.shape
    depth = ada_w.shape[0]
    n = batch * seq
    qk_dim = MLA_NOPE + MLA_ROPE

    tm = min(1024, seq)
    tq = min(512, seq)
    tc = min(512, seq)

    c_pad = jnp.pad(c, ((0, SUBLANES - batch), (0, 0)))
    mod_all = _ada_mod(c_pad, ada_w, ada_b)[:, :batch].reshape(depth, batch, 6, d)

    cols, total = _in_proj_layout(d)
    tn = 512
    total = -(-total // tn) * tn

    pos2 = positions.reshape(n, 1)
    half = MLA_ROPE // 2
    inv_freq = ROPE_THETA ** (-jnp.arange(half, dtype=F32) / half)
    invf = jnp.zeros((1, HEAD_W), F32).at[0, MLA_NOPE:qk_dim].set(jnp.concatenate([inv_freq, inv_freq]))

    x2 = x.reshape(n, d)
    for l in range(depth):
        lambda_init = 0.8 - 0.6 * math.exp(-0.3 * l)
        mod = mod_all[l]
        proj = _in_proj(x2, norm1_w[l][None], mod, _build_w_in(w_in[l], d, total), seq, tm, tn)

        qw2 = jnp.tile(diff_qnorm_w[l], 2)[None]
        kw2 = jnp.tile(diff_knorm_w[l], 2)[None]
        a_q, a_k = _diff_prep(proj, qw2, kw2, cols["a_q"], cols["a_k"], tm)
        tqa = max(tq // 2, CHUNK)
        o_a = _attention(a_q, a_k, proj, (diff_lambda[l], diff_subln_w[l][None]), heads=DIFF_HEADS, batch=batch,
                         seq=seq, tq=tqa, tk=tqa, n_maps=2, v_col=cols["a_v"], lambda_init=lambda_init)

        kv_up = mla_kv_up[l].reshape(MLA_KV_LORA, MLA_HEADS, MLA_NOPE + MLA_V_DIM)
        wk_nope = _head_pad(kv_up[:, :, :MLA_NOPE].reshape(MLA_KV_LORA, -1), MLA_HEADS, MLA_NOPE)
        place_rope = _head_pad(jnp.tile(jnp.eye(MLA_ROPE, dtype=F32), (1, MLA_HEADS)), MLA_HEADS, MLA_ROPE, MLA_NOPE)
        pad_rows = MLA_Q_LORA - MLA_KV_LORA - MLA_ROPE
        wk = jnp.concatenate([wk_nope, place_rope, jnp.zeros((pad_rows, MLA_HEADS * HEAD_W), F32)], axis=0)
        wv = jnp.pad(kv_up[:, :, MLA_NOPE:].reshape(MLA_KV_LORA, -1), ((0, MLA_Q_LORA - MLA_KV_LORA), (0, 0)))
        kva_w = jnp.pad(mla_kva_norm_w[l], (0, MLA_Q_LORA - MLA_KV_LORA))[None]
        b_q, b_k, b_v = _mla_prep(
            proj, pos2, mla_qa_norm_w[l][None], _head_pad(mla_q_up[l], MLA_HEADS, qk_dim).astype(BF16),
            kva_w, wk.astype(BF16), wv.astype(BF16),
            jnp.pad(mla_qnorm_w[l], (0, HEAD_W - qk_dim))[None], jnp.pad(mla_knorm_w[l], (0, HEAD_W - qk_dim))[None],
            invf, cols["b_q"], cols["b_kv"], tm)
        o_b = _attention(b_q, b_k, b_v, (), heads=MLA_HEADS, batch=batch, seq=seq, tq=tq, tk=tq, n_maps=1)

        exp_a = jnp.zeros((1, LANES), F32).at[0, GDN_HEADS:2 * GDN_HEADS].set(gdn_a_log[l])
        dt_b = jnp.zeros((1, LANES), F32).at[0, GDN_HEADS:2 * GDN_HEADS].set(gdn_dt_bias[l])
        c_qkv, c_gb = _gdn_prep(proj, gdn_conv_w[l], exp_a, dt_b, cols["c_qkv"], cols["c_ba"], seq, tm)
        o_c = _gdn(c_qkv, c_gb, proj, gdn_onorm_w[l][None], cols["c_z"], batch, seq, tc)

        x2 = _merge(x2, o_a, o_b, o_c, proj, w_o_diff[l].astype(BF16), w_o_mla[l].astype(BF16),
                    w_o_gdn[l].astype(BF16), w_out[l].astype(BF16), mod, cols["gate"], seq, min(512, seq))
        x2 = _ffn(x2, norm2_w[l][None], mod, ffn_w_gu[l].astype(BF16), ffn_w_down[l].astype(BF16), seq, tm, 256)
    return x2.reshape(batch, seq, d)
```

```python
import functools
import math

import jax
import jax.numpy as jnp
import numpy as np
from jax import lax
from jax.experimental import pallas as pl
from jax.experimental.pallas import tpu as pltpu

F32 = jnp.float32
BF16 = jnp.bfloat16

CHUNK = 64
NORM_EPS = 1e-6
ROPE_THETA = 10000.0
DIFF_HEADS = 4
DIFF_QK_DIM = 64
DIFF_V_DIM = 128
MLA_HEADS = 4
MLA_Q_LORA = 384
MLA_KV_LORA = 256
MLA_NOPE = 64
MLA_ROPE = 32
MLA_V_DIM = 128
GDN_HEADS = 4
GDN_K_DIM = 128
GDN_V_DIM = 128
GDN_CONV = 4
N_BRANCHES = 3

LANES = 128
SUBLANES = 8
HEAD_W = 128
LOG2E = 1.4426950408889634
NEG_BIG = -1e30
VMEM_LIMIT = 56 * 1024 * 1024


def _cparams(*sem):
    return pltpu.CompilerParams(dimension_semantics=sem, vmem_limit_bytes=VMEM_LIMIT)


def _mm(a, b):
    return jnp.dot(a.astype(BF16), b.astype(BF16), preferred_element_type=F32)


def _mm_nt(a, b):
    return lax.dot_general(a.astype(BF16), b.astype(BF16), (((1,), (1,)), ((), ())),
                           preferred_element_type=F32)


def _mod_kernel(c_ref, w_ref, b_ref, o_ref):
    o_ref[...] = jnp.dot(c_ref[...], w_ref[...], preferred_element_type=F32) + b_ref[...]


def _ada_mod(c_pad, ada_w, ada_b):
    depth, d, n6 = ada_w.shape
    tn = 1536
    return pl.pallas_call(
        _mod_kernel,
        grid=(depth, n6 // tn),
        in_specs=[pl.BlockSpec((SUBLANES, d), lambda l, j: (0, 0)),
                  pl.BlockSpec((None, d, tn), lambda l, j: (l, 0, j)),
                  pl.BlockSpec((None, 1, tn), lambda l, j: (l, 0, j))],
        out_specs=pl.BlockSpec((None, SUBLANES, tn), lambda l, j: (l, 0, j)),
        out_shape=jax.ShapeDtypeStruct((depth, SUBLANES, n6), F32),
        compiler_params=_cparams("parallel", "parallel"),
        name="ada_mod",
    )(c_pad, ada_w, ada_b.reshape(depth, 1, n6))


def _norm_mod(x, nw, mod, shift_row, scale_row):
    r = lax.rsqrt(jnp.mean(x * x, axis=-1, keepdims=True) + NORM_EPS)
    h = x * r * nw
    return h * (1.0 + mod[scale_row:scale_row + 1, :]) + mod[shift_row:shift_row + 1, :]


def _in_proj_kernel(x_ref, nw_ref, mod_ref, w_ref, o_ref, h_ref):
    @pl.when(pl.program_id(1) == 0)
    def _():
        h_ref[...] = _norm_mod(x_ref[...], nw_ref[...], mod_ref[...], 0, 1).astype(BF16)

    o_ref[...] = jnp.dot(h_ref[...], w_ref[...], preferred_element_type=F32).astype(o_ref.dtype)


def _in_proj(x2, nw, mod, w, seq, tm, tn):
    n, d = x2.shape
    cols = w.shape[1]
    tiles_per_batch = seq // tm
    return pl.pallas_call(
        _in_proj_kernel,
        grid=(n // tm, cols // tn),
        in_specs=[pl.BlockSpec((tm, d), lambda i, j: (i, 0)),
                  pl.BlockSpec((1, d), lambda i, j: (0, 0)),
                  pl.BlockSpec((None, 6, d), lambda i, j: (i // tiles_per_batch, 0, 0)),
                  pl.BlockSpec((d, tn), lambda i, j: (0, j))],
        out_specs=pl.BlockSpec((tm, tn), lambda i, j: (i, j)),
        out_shape=jax.ShapeDtypeStruct((n, cols), BF16),
        scratch_shapes=[pltpu.VMEM((tm, d), BF16)],
        compiler_params=_cparams("parallel", "arbitrary"),
        name="in_proj",
    )(x2, nw, mod, w)


def _half_rms(t, w):
    lane = lax.broadcasted_iota(jnp.int32, t.shape, 1)
    lo = lane < DIFF_QK_DIM
    sq = t * t
    s_lo = jnp.sum(jnp.where(lo, sq, 0.0), axis=-1, keepdims=True)
    s_hi = jnp.sum(jnp.where(lo, 0.0, sq), axis=-1, keepdims=True)
    r = lax.rsqrt(jnp.where(lo, s_lo, s_hi) * (1.0 / DIFF_QK_DIM) + NORM_EPS)
    return t * r * w


def _diff_prep_kernel(q_ref, k_ref, qw_ref, kw_ref, qo_ref, ko_ref):
    qscale = (DIFF_QK_DIM ** -0.5) * LOG2E
    for h in range(DIFF_HEADS):
        sl = slice(h * HEAD_W, (h + 1) * HEAD_W)
        qo_ref[:, sl] = (_half_rms(q_ref[:, sl].astype(F32), qw_ref[...]) * qscale).astype(BF16)
        ko_ref[:, sl] = _half_rms(k_ref[:, sl].astype(F32), kw_ref[...]).astype(BF16)


def _diff_prep(proj, qw2, kw2, col_q, col_k, tm):
    n = proj.shape[0]
    w = DIFF_HEADS * HEAD_W
    return pl.pallas_call(
        _diff_prep_kernel,
        grid=(n // tm,),
        in_specs=[pl.BlockSpec((tm, w), lambda i: (i, col_q // w)),
                  pl.BlockSpec((tm, w), lambda i: (i, col_k // w)),
                  pl.BlockSpec((1, HEAD_W), lambda i: (0, 0)),
                  pl.BlockSpec((1, HEAD_W), lambda i: (0, 0))],
        out_specs=[pl.BlockSpec((tm, w), lambda i: (i, 0)),
                   pl.BlockSpec((tm, w), lambda i: (i, 0))],
        out_shape=[jax.ShapeDtypeStruct((n, w), BF16), jax.ShapeDtypeStruct((n, w), BF16)],
        compiler_params=_cparams("parallel"),
        name="diff_prep",
    )(proj, proj, qw2, kw2)


def _rope_tile(t, cos_t, sin_lo, sin_hi):
    half = MLA_ROPE // 2
    return (t * cos_t + pltpu.roll(t, HEAD_W - half, 1) * sin_lo + pltpu.roll(t, half, 1) * sin_hi)


def _mla_prep_kernel(q_ref, kv_ref, pos_ref, qa_ref, qup_ref, kva_ref, wk_ref, wv_ref,
                     qn_ref, kn_ref, invf_ref, qo_ref, ko_ref, vo_ref):
    qk_dim = MLA_NOPE + MLA_ROPE
    qscale = (qk_dim ** -0.5) * LOG2E
    half = MLA_ROPE // 2

    xq = q_ref[...].astype(F32)
    rq = lax.rsqrt(jnp.mean(xq * xq, axis=-1, keepdims=True) + NORM_EPS)
    q = _mm(xq * rq * qa_ref[...], qup_ref[...])

    xkv = kv_ref[...].astype(F32)
    lane_kv = lax.broadcasted_iota(jnp.int32, xkv.shape, 1)
    is_lat = lane_kv < MLA_KV_LORA
    ssq = jnp.sum(jnp.where(is_lat, xkv * xkv, 0.0), axis=-1, keepdims=True)
    rkv = lax.rsqrt(ssq * (1.0 / MLA_KV_LORA) + NORM_EPS)
    lhs = jnp.where(is_lat, xkv * rkv * kva_ref[...], xkv).astype(BF16)
    k = jnp.dot(lhs, wk_ref[...], preferred_element_type=F32)
    vo_ref[...] = jnp.dot(lhs, wv_ref[...], preferred_element_type=F32).astype(BF16)

    ang = pos_ref[...].astype(F32) * invf_ref[...]
    lane = lax.broadcasted_iota(jnp.int32, ang.shape, 1)
    in_lo = (lane >= MLA_NOPE) & (lane < MLA_NOPE + half)
    in_hi = (lane >= MLA_NOPE + half) & (lane < qk_dim)
    cos_a, sin_a = jnp.cos(ang), jnp.sin(ang)
    cos_t = jnp.where(in_lo | in_hi, cos_a, 1.0)
    sin_lo = jnp.where(in_lo, -sin_a, 0.0)
    sin_hi = jnp.where(in_hi, sin_a, 0.0)

    for h in range(MLA_HEADS):
        sl = slice(h * HEAD_W, (h + 1) * HEAD_W)
        qh = q[:, sl]
        qh = qh * lax.rsqrt(jnp.sum(qh * qh, axis=-1, keepdims=True) * (1.0 / qk_dim) + NORM_EPS)
        qh = _rope_tile(qh * qn_ref[...], cos_t, sin_lo, sin_hi)
        qo_ref[:, sl] = (qh * qscale).astype(BF16)
        kh = k[:, sl]
        kh = kh * lax.rsqrt(jnp.sum(kh * kh, axis=-1, keepdims=True) * (1.0 / qk_dim) + NORM_EPS)
        kh = _rope_tile(kh * kn_ref[...], cos_t, sin_lo, sin_hi)
        ko_ref[:, sl] = kh.astype(BF16)


def _mla_prep(proj, pos2, qa_w, qup, kva_w, wk, wv, qn_w, kn_w, invf, col_q, col_kv, tm):
    n = proj.shape[0]
    w = MLA_HEADS * HEAD_W
    lat = MLA_Q_LORA
    const = lambda i: (0, 0)
    out = jax.ShapeDtypeStruct((n, w), BF16)
    return pl.pallas_call(
        _mla_prep_kernel,
        grid=(n // tm,),
        in_specs=[pl.BlockSpec((tm, lat), lambda i: (i, col_q // lat)),
                  pl.BlockSpec((tm, lat), lambda i: (i, col_kv // lat)),
                  pl.BlockSpec((tm, 1), lambda i: (i, 0)),
                  pl.BlockSpec((1, lat), const),
                  pl.BlockSpec((lat, w), const),
                  pl.BlockSpec((1, lat), const),
                  pl.BlockSpec((lat, w), const),
                  pl.BlockSpec((lat, w), const),
                  pl.BlockSpec((1, HEAD_W), const),
                  pl.BlockSpec((1, HEAD_W), const),
                  pl.BlockSpec((1, HEAD_W), const)],
        out_specs=[pl.BlockSpec((tm, w), lambda i: (i, 0))] * 3,
        out_shape=[out, out, out],
        compiler_params=_cparams("parallel"),
        name="mla_prep",
    )(proj, proj, pos2, qa_w, qup, kva_w, wk, wv, qn_w, kn_w, invf)


def _attn_kernel(*refs, tq, tk, rg, n_maps, lambda_init):
    if n_maps == 2:
        q_ref, k_ref, v_ref, lam_ref, sw_ref, o_ref, qq_ref, s_ref, m_ref, l_ref, acc_ref = refs
    else:
        q_ref, k_ref, v_ref, o_ref, s_ref, m_ref, l_ref, acc_ref = refs
        qq_ref = q_ref
    i = pl.program_id(2)
    rows = n_maps * tq
    if n_maps == 2:
        q = q_ref[...]
        lane = lax.broadcasted_iota(jnp.int32, q.shape, 1)
        zero = jnp.zeros_like(q)
        qq_ref[:tq, :] = jnp.where(lane < DIFF_QK_DIM, q, zero)
        qq_ref[tq:, :] = jnp.where(lane >= DIFF_QK_DIM, q, zero)

    m_ref[...] = jnp.full(m_ref.shape, NEG_BIG, F32)
    l_ref[...] = jnp.zeros(l_ref.shape, F32)
    acc_ref[...] = jnp.zeros(acc_ref.shape, F32)

    def scores(j, g):
        start = pl.multiple_of(j * tk, tk)
        return lax.dot_general(qq_ref[g * rg:(g + 1) * rg, :], k_ref[pl.ds(start, tk), :],
                               (((1,), (1,)), ((), ())), preferred_element_type=F32)

    def step(j, slot, masked, prefetch):
        start = pl.multiple_of(j * tk, tk)
        v = v_ref[pl.ds(start, tk), :]
        def softmax(g):
            rsl = slice(g * rg, (g + 1) * rg)
            s = s_ref[slot, rsl, :]
            if masked:
                r_idx = lax.broadcasted_iota(jnp.int32, (rg, tk), 0)
                c_idx = lax.broadcasted_iota(jnp.int32, (rg, tk), 1)
                q_chunk = (i * tq + (g * rg) % tq + r_idx) // CHUNK
                k_chunk = (j * tk + c_idx) // CHUNK
                s = jnp.where(k_chunk <= q_chunk, s, NEG_BIG)
            m_prev = m_ref[rsl, :]
            m_new = jnp.maximum(m_prev, jnp.max(s, axis=-1, keepdims=True))
            alpha = jnp.exp2(m_prev - m_new)
            p = jnp.exp2(s - pltpu.repeat(m_new, tk // LANES, 1))
            l_ref[rsl, :] = alpha * l_ref[rsl, :] + jnp.sum(p, axis=-1, keepdims=True)
            m_ref[rsl, :] = m_new
            return p.astype(BF16), alpha

        def weighted_values(g, p, alpha):
            rsl = slice(g * rg, (g + 1) * rg)
            acc_ref[rsl, :] = alpha * acc_ref[rsl, :] + jnp.dot(p, v, preferred_element_type=F32)

        for g in range(rows // rg):
            if prefetch:
                s_ref[1 - slot, g * rg:(g + 1) * rg, :] = scores(j + 1, g)
            weighted_values(g, *softmax(g))

    n_full = (i * tq) // tk
    n_diag = max(tq // tk, 1)

    for g in range(rows // rg):
        s_ref[0, g * rg:(g + 1) * rg, :] = scores(0, g)

    def body(jj, carry):
        step(2 * jj, 0, False, True)
        step(2 * jj + 1, 1, False, True)
        return carry

    lax.fori_loop(0, n_full // 2, body, 0)

    def tail(first_slot):
        for d in range(n_diag):
            step(n_full + d, (first_slot + d) % 2, True, d < n_diag - 1)

    @pl.when(n_full % 2 == 1)
    def _():
        step(n_full - 1, 0, False, True)
        tail(1)

    @pl.when(n_full % 2 == 0)
    def _():
        tail(0)

    o = acc_ref[...] / l_ref[...]
    if n_maps == 2:
        lp = lam_ref[...]
        lam = (jnp.exp(jnp.sum(lp[0:1] * lp[1:2], axis=-1, keepdims=True))
               - jnp.exp(jnp.sum(lp[2:3] * lp[3:4], axis=-1, keepdims=True)) + lambda_init)
        o = o[:tq] - lam * o[tq:]
        r = lax.rsqrt(jnp.mean(o * o, axis=-1, keepdims=True) + NORM_EPS)
        o = o * r * sw_ref[...] * (1.0 - lambda_init)
    o_ref[...] = o.astype(o_ref.dtype)


def _attention(q, k, v, extra, *, heads, batch, seq, tq, tk, rg, n_maps, v_col=0, lambda_init=0.0):
    q3, k3, v3 = (t.reshape(batch, seq, t.shape[1]) for t in (q, k, v))
    rows = n_maps * tq
    rg = min(rg, tq)
    vb = v_col // HEAD_W
    scratch = [pltpu.VMEM((2, rows, tk), F32), pltpu.VMEM((rows, LANES), F32), pltpu.VMEM((rows, LANES), F32),
               pltpu.VMEM((rows, HEAD_W), F32)]
    if n_maps == 2:
        scratch.insert(0, pltpu.VMEM((rows, HEAD_W), BF16))
    in_specs = [pl.BlockSpec((None, tq, HEAD_W), lambda b, h, i: (b, i, h)),
                pl.BlockSpec((None, seq, HEAD_W), lambda b, h, i: (b, 0, h)),
                pl.BlockSpec((None, seq, HEAD_W), lambda b, h, i: (b, 0, vb + h))]
    for e in extra:
        in_specs.append(pl.BlockSpec(e.shape, lambda b, h, i: (0, 0)))
    out = pl.pallas_call(
        functools.partial(_attn_kernel, tq=tq, tk=tk, rg=rg, n_maps=n_maps, lambda_init=lambda_init),
        grid=(batch, heads, seq // tq),
        in_specs=in_specs,
        out_specs=pl.BlockSpec((None, tq, HEAD_W), lambda b, h, i: (b, i, h)),
        out_shape=jax.ShapeDtypeStruct((batch, seq, heads * HEAD_W), BF16),
        scratch_shapes=scratch,
        compiler_params=_cparams("parallel", "parallel", "arbitrary"),
        name="diff_attn" if n_maps == 2 else "mla_attn",
    )(q3, k3, v3, *extra)
    return out.reshape(batch * seq, heads * HEAD_W)


def _gdn_prep_kernel(x_ref, halo_ref, ba_ref, cw_ref, ea_ref, dtb_ref, qkv_ref, gb_ref, *, tiles_per_batch):
    i = pl.program_id(0)
    x = x_ref[...].astype(F32)
    halo = halo_ref[...].astype(F32)
    halo = jnp.where(i % tiles_per_batch == 0, 0.0, halo)
    cw = cw_ref[...]
    row8 = lax.broadcasted_iota(jnp.int32, halo.shape, 0)
    y = x * cw[GDN_CONV - 1:GDN_CONV]
    for s in range(1, GDN_CONV):
        xs = pltpu.roll(x, s, 0)
        hs = pltpu.roll(halo, s, 0)
        first = jnp.where(row8 < s, hs, xs[:SUBLANES])
        xs = jnp.concatenate([first, xs[SUBLANES:]], axis=0)
        y = y + xs * cw[GDN_CONV - 1 - s:GDN_CONV - s]
    y = y * jax.nn.sigmoid(y)
    nqk = 2 * GDN_HEADS
    for h in range(3 * GDN_HEADS):
        sl = slice(h * HEAD_W, (h + 1) * HEAD_W)
        t = y[:, sl]
        if h < nqk:
            t = t * lax.rsqrt(jnp.sum(t * t, axis=-1, keepdims=True) + NORM_EPS)
            if h < GDN_HEADS:
                t = t * (GDN_K_DIM ** -0.5)
        qkv_ref[:, sl] = t
    ba = ba_ref[...].astype(F32)
    lane = lax.broadcasted_iota(jnp.int32, ba.shape, 1)
    sp_in = ba + dtb_ref[...]
    softplus = jnp.maximum(sp_in, 0.0) + jnp.log1p(jnp.exp(-jnp.abs(sp_in)))
    gb_ref[...] = jnp.where(lane < GDN_HEADS, jax.nn.sigmoid(ba), -jnp.exp(ea_ref[...]) * softplus)


def _gdn_prep(proj, conv_w, exp_a, dt_b, col_qkv, col_ba, seq, tm):
    n = proj.shape[0]
    w = 3 * GDN_HEADS * HEAD_W
    tiles_per_batch = seq // tm
    hb = tm // SUBLANES
    return pl.pallas_call(
        functools.partial(_gdn_prep_kernel, tiles_per_batch=tiles_per_batch),
        grid=(n // tm,),
        in_specs=[pl.BlockSpec((tm, w), lambda i: (i, col_qkv // w)),
                  pl.BlockSpec((SUBLANES, w), lambda i: (jnp.maximum(i * hb - 1, 0), col_qkv // w)),
                  pl.BlockSpec((tm, LANES), lambda i: (i, col_ba // LANES)),
                  pl.BlockSpec((GDN_CONV, w), lambda i: (0, 0)),
                  pl.BlockSpec((1, LANES), lambda i: (0, 0)),
                  pl.BlockSpec((1, LANES), lambda i: (0, 0))],
        out_specs=[pl.BlockSpec((tm, w), lambda i: (i, 0)),
                   pl.BlockSpec((tm, LANES), lambda i: (i, 0))],
        out_shape=[jax.ShapeDtypeStruct((n, w), F32), jax.ShapeDtypeStruct((n, LANES), F32)],
        compiler_params=_cparams("parallel"),
        name="gdn_prep",
    )(proj, proj, proj, conv_w, exp_a, dt_b)


def _unit_lower_inverse(lmat, row, col):
    eye = (row == col).astype(F32)
    same2 = (row // 2) == (col // 2)
    t = eye - jnp.where(same2, lmat, 0.0)
    b = 2
    while b < CHUNK:
        off = ((row // (2 * b)) == (col // (2 * b))) & ((row // b) % 2 == 1) & ((col // b) % 2 == 0)
        c = jnp.where(off, lmat, 0.0)
        t = t - _mm(t, _mm(c, t))
        b *= 2
    return t


def _gdn_kernel(qkv_ref, gb_ref, z_ref, ow_ref, o_ref, state_ref, *, tc):
    @pl.when(pl.program_id(1) == 0)
    def _():
        state_ref[...] = jnp.zeros(state_ref.shape, F32)

    hw = GDN_HEADS * HEAD_W
    gb = gb_ref[...]
    r_in_chunk = lax.broadcasted_iota(jnp.int32, gb.shape, 0) % CHUNK
    gc = gb
    sh = 1
    while sh < CHUNK:
        gc = gc + jnp.where(r_in_chunk >= sh, pltpu.roll(gc, sh, 0), 0.0)
        sh *= 2
    gc_t = gc.T

    row = lax.broadcasted_iota(jnp.int32, (CHUNK, CHUNK), 0)
    col = lax.broadcasted_iota(jnp.int32, (CHUNK, CHUNK), 1)
    incl = row >= col
    strict = row > col

    for c in range(tc // CHUNK):
        rs = slice(c * CHUNK, (c + 1) * CHUNK)
        for h in range(GDN_HEADS):
            q = qkv_ref[rs, h * HEAD_W:(h + 1) * HEAD_W]
            k = qkv_ref[rs, hw + h * HEAD_W:hw + (h + 1) * HEAD_W]
            v = qkv_ref[rs, 2 * hw + h * HEAD_W:2 * hw + (h + 1) * HEAD_W]
            beta = gb[rs, h:h + 1]
            g_col = gc[rs, GDN_HEADS + h:GDN_HEADS + h + 1]
            g_row = gc_t[GDN_HEADS + h:GDN_HEADS + h + 1, rs]
            g_last = g_col[CHUNK - 1:CHUNK, :]
            decay = jnp.exp(jnp.where(incl, g_col - g_row, NEG_BIG))
            k_beta = k * beta
            lmat = jnp.where(strict, _mm_nt(k_beta, k) * decay, 0.0)
            t_mat = _unit_lower_inverse(lmat, row, col)
            uw = _mm(t_mat, jnp.concatenate([v * beta, k_beta * jnp.exp(g_col)], axis=1))
            u, w = uw[:, :HEAD_W], uw[:, HEAD_W:]
            attn = jnp.where(incl, _mm_nt(q, k) * decay, 0.0)
            state = state_ref[h]
            v_new = u - _mm(w, state)
            o = _mm(q * jnp.exp(g_col), state) + _mm(attn, v_new)
            k_dec = (k * jnp.exp(g_last - g_col)).T
            state_ref[h] = state * jnp.exp(g_last) + _mm(k_dec, v_new)
            o = o * lax.rsqrt(jnp.mean(o * o, axis=-1, keepdims=True) + NORM_EPS) * ow_ref[...]
            z = z_ref[rs, h * HEAD_W:(h + 1) * HEAD_W].astype(F32)
            o_ref[rs, h * HEAD_W:(h + 1) * HEAD_W] = (o * (z * jax.nn.sigmoid(z))).astype(o_ref.dtype)


def _gdn(qkv, gb, proj, onorm_w, col_z, batch, seq, tc):
    hw = GDN_HEADS * HEAD_W
    nblk = seq // tc
    return pl.pallas_call(
        functools.partial(_gdn_kernel, tc=tc),
        grid=(batch, nblk),
        in_specs=[pl.BlockSpec((tc, 3 * hw), lambda b, i: (b * nblk + i, 0)),
                  pl.BlockSpec((tc, LANES), lambda b, i: (b * nblk + i, 0)),
                  pl.BlockSpec((tc, hw), lambda b, i: (b * nblk + i, col_z // hw)),
                  pl.BlockSpec((1, HEAD_W), lambda b, i: (0, 0))],
        out_specs=pl.BlockSpec((tc, hw), lambda b, i: (b * nblk + i, 0)),
        out_shape=jax.ShapeDtypeStruct((batch * seq, hw), BF16),
        scratch_shapes=[pltpu.VMEM((GDN_HEADS, GDN_K_DIM, GDN_V_DIM), F32)],
        compiler_params=_cparams("parallel", "arbitrary"),
        name="gdn",
    )(qkv, gb, proj, onorm_w)


def _merge_kernel(x_ref, oa_ref, ob_ref, oc_ref, gate_ref, wa_ref, wb_ref, wc_ref, wout_ref, mod_ref, o_ref):
    d = x_ref.shape[1]
    merged = None
    for idx, (o_r, w_r) in enumerate(((oa_ref, wa_ref), (ob_ref, wb_ref), (oc_ref, wc_ref))):
        y = jnp.dot(o_r[...], w_r[...], preferred_element_type=F32)
        g = jax.nn.sigmoid(gate_ref[:, idx * d:(idx + 1) * d].astype(F32))
        merged = g * y if merged is None else merged + g * y
    upd = jnp.dot(merged.astype(BF16), wout_ref[...], preferred_element_type=F32)
    o_ref[...] = x_ref[...] + mod_ref[2:3, :] * upd


def _merge(x2, oa, ob, oc, proj, wa, wb, wc, wout, mod, col_gate, seq, tm):
    n, d = x2.shape
    w = DIFF_HEADS * HEAD_W
    tiles_per_batch = seq // tm
    row = lambda i: (i, 0)
    const = lambda i: (0, 0)
    return pl.pallas_call(
        _merge_kernel,
        grid=(n // tm,),
        in_specs=[pl.BlockSpec((tm, d), row), pl.BlockSpec((tm, w), row), pl.BlockSpec((tm, w), row),
                  pl.BlockSpec((tm, w), row),
                  pl.BlockSpec((tm, N_BRANCHES * d), lambda i: (i, col_gate // (N_BRANCHES * d))),
                  pl.BlockSpec((w, d), const), pl.BlockSpec((w, d), const), pl.BlockSpec((w, d), const),
                  pl.BlockSpec((d, d), const),
                  pl.BlockSpec((None, 6, d), lambda i: (i // tiles_per_batch, 0, 0))],
        out_specs=pl.BlockSpec((tm, d), row),
        out_shape=jax.ShapeDtypeStruct((n, d), F32),
        compiler_params=_cparams("parallel"),
        name="merge",
    )(x2, oa, ob, oc, proj, wa, wb, wc, wout, mod)


def _ffn_kernel(x_ref, nw_ref, mod_ref, wg_ref, wu_ref, wd_ref, o_ref, h_ref, acc_ref):
    j = pl.program_id(1)

    @pl.when(j == 0)
    def _():
        h_ref[...] = _norm_mod(x_ref[...], nw_ref[...], mod_ref[...], 3, 4).astype(BF16)
        acc_ref[...] = jnp.zeros(acc_ref.shape, F32)

    h = h_ref[...]
    gate = jnp.dot(h, wg_ref[...], preferred_element_type=F32)
    up = jnp.dot(h, wu_ref[...], preferred_element_type=F32)
    act = (gate * jax.nn.sigmoid(gate) * up).astype(BF16)
    acc_ref[...] += jnp.dot(act, wd_ref[...], preferred_element_type=F32)

    @pl.when(j == pl.num_programs(1) - 1)
    def _():
        o_ref[...] = x_ref[...] + mod_ref[5:6, :] * acc_ref[...]


def _ffn(x2, nw, mod, w_gu, w_down, seq, tm, th):
    n, d = x2.shape
    hidden = w_down.shape[0]
    nh = hidden // th
    tiles_per_batch = seq // tm
    return pl.pallas_call(
        _ffn_kernel,
        grid=(n // tm, nh),
        in_specs=[pl.BlockSpec((tm, d), lambda i, j: (i, 0)),
                  pl.BlockSpec((1, d), lambda i, j: (0, 0)),
                  pl.BlockSpec((None, 6, d), lambda i, j: (i // tiles_per_batch, 0, 0)),
                  pl.BlockSpec((d, th), lambda i, j: (0, j)),
                  pl.BlockSpec((d, th), lambda i, j: (0, nh + j)),
                  pl.BlockSpec((th, d), lambda i, j: (j, 0))],
        out_specs=pl.BlockSpec((tm, d), lambda i, j: (i, 0)),
        out_shape=jax.ShapeDtypeStruct((n, d), F32),
        scratch_shapes=[pltpu.VMEM((tm, d), BF16), pltpu.VMEM((tm, d), F32)],
        compiler_params=_cparams("parallel", "arbitrary"),
        name="ffn",
    )(x2, nw, mod, w_gu, w_gu, w_down)


def _pad_cols(w, width):
    return jnp.pad(w, ((0, 0), (0, width - w.shape[1])))


def _in_proj_layout(d_model):
    a = DIFF_HEADS * HEAD_W
    g = GDN_HEADS * HEAD_W
    cols = {}
    off = 0
    for name, width in (("gate", N_BRANCHES * d_model), ("c_qkv", 3 * g), ("a_q", a), ("a_k", a), ("a_v", a),
                        ("c_z", g), ("c_ba", LANES), ("b_q", MLA_Q_LORA), ("b_kv", MLA_Q_LORA)):
        if off % width:
            off += width - off % width
        cols[name] = off
        off += width
    return cols, off


def _build_w_in(w_in, d_model, total):
    a = DIFF_HEADS * 2 * DIFF_QK_DIM
    g = GDN_HEADS * GDN_K_DIM
    widths = (a, a, DIFF_HEADS * DIFF_V_DIM, MLA_Q_LORA, MLA_KV_LORA + MLA_ROPE, g, g, GDN_HEADS * GDN_V_DIM,
              GDN_HEADS * GDN_V_DIM, GDN_HEADS, GDN_HEADS, N_BRANCHES * d_model)
    offs = np.concatenate([[0], np.cumsum(widths)])
    part = [w_in[:, int(offs[i]):int(offs[i + 1])] for i in range(len(widths))]
    cols, _ = _in_proj_layout(d_model)
    out = jnp.zeros((w_in.shape[0], total), w_in.dtype)
    place = (("a_q", part[0]), ("a_k", part[1]), ("a_v", part[2]), ("b_q", part[3]), ("b_kv", part[4]),
             ("c_qkv", jnp.concatenate(part[5:8], axis=1)), ("c_z", part[8]), ("gate", part[11]),
             ("c_ba", jnp.concatenate(part[9:11], axis=1)))
    for name, p in place:
        out = lax.dynamic_update_slice(out, p, (0, cols[name]))
    return out.astype(BF16)


def _head_pad(w, heads, per_head, lane_off=0):
    r = w.shape[0]
    w = w.reshape(r, heads, per_head)
    w = jnp.pad(w, ((0, 0), (0, 0), (lane_off, HEAD_W - per_head - lane_off)))
    return w.reshape(r, heads * HEAD_W)


def kernel(x, c, positions, ada_w, ada_b, norm1_w, w_in, diff_qnorm_w, diff_knorm_w, diff_lambda, diff_subln_w, w_o_diff, mla_qa_norm_w, mla_q_up, mla_kva_norm_w, mla_kv_up, mla_qnorm_w, mla_knorm_w, w_o_mla, gdn_conv_w, gdn_a_log, gdn_dt_bias, gdn_onorm_w, w_o_gdn, w_out, norm2_w, ffn_w_gu, ffn_w_down):
    batch, seq, d = x.shape
    depth = ada_w.shape[0]
    n = batch * seq
    qk_dim = MLA_NOPE + MLA_ROPE

    tm = min(1024, seq)
    tq = min(512, seq)
    tc = min(512, seq)

    c_pad = jnp.pad(c, ((0, SUBLANES - batch), (0, 0)))
    mod_all = _ada_mod(c_pad, ada_w, ada_b)[:, :batch].reshape(depth, batch, 6, d)

    cols, total = _in_proj_layout(d)
    tn = 512
    total = -(-total // tn) * tn

    pos2 = positions.reshape(n, 1)
    half = MLA_ROPE // 2
    inv_freq = ROPE_THETA ** (-jnp.arange(half, dtype=F32) / half)
    invf = jnp.zeros((1, HEAD_W), F32).at[0, MLA_NOPE:qk_dim].set(jnp.concatenate([inv_freq, inv_freq]))

    x2 = x.reshape(n, d)
    for l in range(depth):
        lambda_init = 0.8 - 0.6 * math.exp(-0.3 * l)
        mod = mod_all[l]
        proj = _in_proj(x2, norm1_w[l][None], mod, _build_w_in(w_in[l], d, total), seq, tm, tn)

        qw2 = jnp.tile(diff_qnorm_w[l], 2)[None]
        kw2 = jnp.tile(diff_knorm_w[l], 2)[None]
        a_q, a_k = _diff_prep(proj, qw2, kw2, cols["a_q"], cols["a_k"], tm)
        o_a = _attention(a_q, a_k, proj, (diff_lambda[l], diff_subln_w[l][None]), heads=DIFF_HEADS, batch=batch,
                         seq=seq, tq=tq // 2, tk=tq, rg=128, n_maps=2, v_col=cols["a_v"], lambda_init=lambda_init)

        kv_up = mla_kv_up[l].reshape(MLA_KV_LORA, MLA_HEADS, MLA_NOPE + MLA_V_DIM)
        wk_nope = _head_pad(kv_up[:, :, :MLA_NOPE].reshape(MLA_KV_LORA, -1), MLA_HEADS, MLA_NOPE)
        place_rope = _head_pad(jnp.tile(jnp.eye(MLA_ROPE, dtype=F32), (1, MLA_HEADS)), MLA_HEADS, MLA_ROPE, MLA_NOPE)
        pad_rows = MLA_Q_LORA - MLA_KV_LORA - MLA_ROPE
        wk = jnp.concatenate([wk_nope, place_rope, jnp.zeros((pad_rows, MLA_HEADS * HEAD_W), F32)], axis=0)
        wv = jnp.pad(kv_up[:, :, MLA_NOPE:].reshape(MLA_KV_LORA, -1), ((0, MLA_Q_LORA - MLA_KV_LORA), (0, 0)))
        kva_w = jnp.pad(mla_kva_norm_w[l], (0, MLA_Q_LORA - MLA_KV_LORA))[None]
        b_q, b_k, b_v = _mla_prep(
            proj, pos2, mla_qa_norm_w[l][None], _head_pad(mla_q_up[l], MLA_HEADS, qk_dim).astype(BF16),
            kva_w, wk.astype(BF16), wv.astype(BF16),
            jnp.pad(mla_qnorm_w[l], (0, HEAD_W - qk_dim))[None], jnp.pad(mla_knorm_w[l], (0, HEAD_W - qk_dim))[None],
            invf, cols["b_q"], cols["b_kv"], tm)
        o_b = _attention(b_q, b_k, b_v, (), heads=MLA_HEADS, batch=batch, seq=seq, tq=tq, tk=tq,
                         rg=128, n_maps=1)

        exp_a = jnp.zeros((1, LANES), F32).at[0, GDN_HEADS:2 * GDN_HEADS].set(gdn_a_log[l])
        dt_b = jnp.zeros((1, LANES), F32).at[0, GDN_HEADS:2 * GDN_HEADS].set(gdn_dt_bias[l])
        c_qkv, c_gb = _gdn_prep(proj, gdn_conv_w[l], exp_a, dt_b, cols["c_qkv"], cols["c_ba"], seq, tm)
        o_c = _gdn(c_qkv, c_gb, proj, gdn_onorm_w[l][None], cols["c_z"], batch, seq, tc)

        x2 = _merge(x2, o_a, o_b, o_c, proj, w_o_diff[l].astype(BF16), w_o_mla[l].astype(BF16),
                    w_o_gdn[l].astype(BF16), w_out[l].astype(BF16), mod, cols["gate"], seq, min(512, seq))
        x2 = _ffn(x2, norm2_w[l][None], mod, ffn_w_gu[l].astype(BF16), ffn_w_down[l].astype(BF16), seq, tm, 256)
    return x2.reshape(batch, seq, d)
```

```python
import functools
import math

import jax
import jax.numpy as jnp
import numpy as np
from jax import lax
from jax.experimental import pallas as pl
from jax.experimental.pallas import tpu as pltpu

F32 = jnp.float32
BF16 = jnp.bfloat16

CHUNK = 64
NORM_EPS = 1e-6
ROPE_THETA = 10000.0
DIFF_HEADS = 4
DIFF_QK_DIM = 64
DIFF_V_DIM = 128
MLA_HEADS = 4
MLA_Q_LORA = 384
MLA_KV_LORA = 256
MLA_NOPE = 64
MLA_ROPE = 32
MLA_V_DIM = 128
GDN_HEADS = 4
GDN_K_DIM = 128
GDN_V_DIM = 128
GDN_CONV = 4
N_BRANCHES = 3

LANES = 128
SUBLANES = 8
HEAD_W = 128
LOG2E = 1.4426950408889634
NEG_BIG = -1e30
VMEM_LIMIT = 56 * 1024 * 1024


def _cparams(*sem):
    return pltpu.CompilerParams(dimension_semantics=sem, vmem_limit_bytes=VMEM_LIMIT)


def _mm(a, b):
    return jnp.dot(a.astype(BF16), b.astype(BF16), preferred_element_type=F32)


def _mm_nt(a, b):
    return lax.dot_general(a.astype(BF16), b.astype(BF16), (((1,), (1,)), ((), ())),
                           preferred_element_type=F32)


def _mod_kernel(c_ref, w_ref, b_ref, o_ref):
    o_ref[...] = jnp.dot(c_ref[...], w_ref[...], preferred_element_type=F32) + b_ref[...]


def _ada_mod(c_pad, ada_w, ada_b):
    depth, d, n6 = ada_w.shape
    tn = 1536
    return pl.pallas_call(
        _mod_kernel,
        grid=(depth, n6 // tn),
        in_specs=[pl.BlockSpec((SUBLANES, d), lambda l, j: (0, 0)),
                  pl.BlockSpec((None, d, tn), lambda l, j: (l, 0, j)),
                  pl.BlockSpec((None, 1, tn), lambda l, j: (l, 0, j))],
        out_specs=pl.BlockSpec((None, SUBLANES, tn), lambda l, j: (l, 0, j)),
        out_shape=jax.ShapeDtypeStruct((depth, SUBLANES, n6), F32),
        compiler_params=_cparams("parallel", "parallel"),
        name="ada_mod",
    )(c_pad, ada_w, ada_b.reshape(depth, 1, n6))


def _norm_mod(x, nw, mod, shift_row, scale_row):
    r = lax.rsqrt(jnp.mean(x * x, axis=-1, keepdims=True) + NORM_EPS)
    h = x * r * nw
    return h * (1.0 + mod[scale_row:scale_row + 1, :]) + mod[shift_row:shift_row + 1, :]


def _in_proj_kernel(x_ref, nw_ref, mod_ref, w_ref, o_ref, h_ref):
    @pl.when(pl.program_id(1) == 0)
    def _():
        h_ref[...] = _norm_mod(x_ref[...], nw_ref[...], mod_ref[...], 0, 1).astype(BF16)

    o_ref[...] = jnp.dot(h_ref[...], w_ref[...], preferred_element_type=F32).astype(o_ref.dtype)


def _in_proj(x2, nw, mod, w, seq, tm, tn):
    n, d = x2.shape
    cols = w.shape[1]
    tiles_per_batch = seq // tm
    return pl.pallas_call(
        _in_proj_kernel,
        grid=(n // tm, cols // tn),
        in_specs=[pl.BlockSpec((tm, d), lambda i, j: (i, 0)),
                  pl.BlockSpec((1, d), lambda i, j: (0, 0)),
                  pl.BlockSpec((None, 6, d), lambda i, j: (i // tiles_per_batch, 0, 0)),
                  pl.BlockSpec((d, tn), lambda i, j: (0, j))],
        out_specs=pl.BlockSpec((tm, tn), lambda i, j: (i, j)),
        out_shape=jax.ShapeDtypeStruct((n, cols), BF16),
        scratch_shapes=[pltpu.VMEM((tm, d), BF16)],
        compiler_params=_cparams("parallel", "arbitrary"),
        name="in_proj",
    )(x2, nw, mod, w)


def _half_rms(t, w):
    lane = lax.broadcasted_iota(jnp.int32, t.shape, 1)
    lo = lane < DIFF_QK_DIM
    sq = t * t
    s_lo = jnp.sum(jnp.where(lo, sq, 0.0), axis=-1, keepdims=True)
    s_hi = jnp.sum(jnp.where(lo, 0.0, sq), axis=-1, keepdims=True)
    r = lax.rsqrt(jnp.where(lo, s_lo, s_hi) * (1.0 / DIFF_QK_DIM) + NORM_EPS)
    return t * r * w


def _diff_prep_kernel(q_ref, k_ref, qw_ref, kw_ref, qo_ref, ko_ref):
    qscale = (DIFF_QK_DIM ** -0.5) * LOG2E
    for h in range(DIFF_HEADS):
        sl = slice(h * HEAD_W, (h + 1) * HEAD_W)
        qo_ref[:, sl] = (_half_rms(q_ref[:, sl].astype(F32), qw_ref[...]) * qscale).astype(BF16)
        ko_ref[:, sl] = _half_rms(k_ref[:, sl].astype(F32), kw_ref[...]).astype(BF16)


def _diff_prep(proj, qw2, kw2, col_q, col_k, tm):
    n = proj.shape[0]
    w = DIFF_HEADS * HEAD_W
    return pl.pallas_call(
        _diff_prep_kernel,
        grid=(n // tm,),
        in_specs=[pl.BlockSpec((tm, w), lambda i: (i, col_q // w)),
                  pl.BlockSpec((tm, w), lambda i: (i, col_k // w)),
                  pl.BlockSpec((1, HEAD_W), lambda i: (0, 0)),
                  pl.BlockSpec((1, HEAD_W), lambda i: (0, 0))],
        out_specs=[pl.BlockSpec((tm, w), lambda i: (i, 0)),
                   pl.BlockSpec((tm, w), lambda i: (i, 0))],
        out_shape=[jax.ShapeDtypeStruct((n, w), BF16), jax.ShapeDtypeStruct((n, w), BF16)],
        compiler_params=_cparams("parallel"),
        name="diff_prep",
    )(proj, proj, qw2, kw2)


def _rope_tile(t, cos_t, sin_lo, sin_hi):
    half = MLA_ROPE // 2
    return (t * cos_t + pltpu.roll(t, HEAD_W - half, 1) * sin_lo + pltpu.roll(t, half, 1) * sin_hi)


def _mla_prep_kernel(q_ref, kv_ref, pos_ref, qa_ref, qup_ref, kva_ref, wk_ref, wv_ref,
                     qn_ref, kn_ref, invf_ref, qo_ref, ko_ref, vo_ref):
    qk_dim = MLA_NOPE + MLA_ROPE
    qscale = (qk_dim ** -0.5) * LOG2E
    half = MLA_ROPE // 2

    xq = q_ref[...].astype(F32)
    rq = lax.rsqrt(jnp.mean(xq * xq, axis=-1, keepdims=True) + NORM_EPS)
    q = _mm(xq * rq * qa_ref[...], qup_ref[...])

    xkv = kv_ref[...].astype(F32)
    lane_kv = lax.broadcasted_iota(jnp.int32, xkv.shape, 1)
    is_lat = lane_kv < MLA_KV_LORA
    ssq = jnp.sum(jnp.where(is_lat, xkv * xkv, 0.0), axis=-1, keepdims=True)
    rkv = lax.rsqrt(ssq * (1.0 / MLA_KV_LORA) + NORM_EPS)
    lhs = jnp.where(is_lat, xkv * rkv * kva_ref[...], xkv).astype(BF16)
    k = jnp.dot(lhs, wk_ref[...], preferred_element_type=F32)
    vo_ref[...] = jnp.dot(lhs, wv_ref[...], preferred_element_type=F32).astype(BF16)

    ang = pos_ref[...].astype(F32) * invf_ref[...]
    lane = lax.broadcasted_iota(jnp.int32, ang.shape, 1)
    in_lo = (lane >= MLA_NOPE) & (lane < MLA_NOPE + half)
    in_hi = (lane >= MLA_NOPE + half) & (lane < qk_dim)
    cos_a, sin_a = jnp.cos(ang), jnp.sin(ang)
    cos_t = jnp.where(in_lo | in_hi, cos_a, 1.0)
    sin_lo = jnp.where(in_lo, -sin_a, 0.0)
    sin_hi = jnp.where(in_hi, sin_a, 0.0)

    for h in range(MLA_HEADS):
        sl = slice(h * HEAD_W, (h + 1) * HEAD_W)
        qh = q[:, sl]
        qh = qh * lax.rsqrt(jnp.sum(qh * qh, axis=-1, keepdims=True) * (1.0 / qk_dim) + NORM_EPS)
        qh = _rope_tile(qh * qn_ref[...], cos_t, sin_lo, sin_hi)
        qo_ref[:, sl] = (qh * qscale).astype(BF16)
        kh = k[:, sl]
        kh = kh * lax.rsqrt(jnp.sum(kh * kh, axis=-1, keepdims=True) * (1.0 / qk_dim) + NORM_EPS)
        kh = _rope_tile(kh * kn_ref[...], cos_t, sin_lo, sin_hi)
        ko_ref[:, sl] = kh.astype(BF16)


def _mla_prep(proj, pos2, qa_w, qup, kva_w, wk, wv, qn_w, kn_w, invf, col_q, col_kv, tm):
    n = proj.shape[0]
    w = MLA_HEADS * HEAD_W
    lat = MLA_Q_LORA
    const = lambda i: (0, 0)
    out = jax.ShapeDtypeStruct((n, w), BF16)
    return pl.pallas_call(
        _mla_prep_kernel,
        grid=(n // tm,),
        in_specs=[pl.BlockSpec((tm, lat), lambda i: (i, col_q // lat)),
                  pl.BlockSpec((tm, lat), lambda i: (i, col_kv // lat)),
                  pl.BlockSpec((tm, 1), lambda i: (i, 0)),
                  pl.BlockSpec((1, lat), const),
                  pl.BlockSpec((lat, w), const),
                  pl.BlockSpec((1, lat), const),
                  pl.BlockSpec((lat, w), const),
                  pl.BlockSpec((lat, w), const),
                  pl.BlockSpec((1, HEAD_W), const),
                  pl.BlockSpec((1, HEAD_W), const),
                  pl.BlockSpec((1, HEAD_W), const)],
        out_specs=[pl.BlockSpec((tm, w), lambda i: (i, 0))] * 3,
        out_shape=[out, out, out],
        compiler_params=_cparams("parallel"),
        name="mla_prep",
    )(proj, proj, pos2, qa_w, qup, kva_w, wk, wv, qn_w, kn_w, invf)


def _attn_kernel(*refs, tq, tk, rg, n_maps, lambda_init):
    if n_maps == 2:
        q_ref, k_ref, v_ref, lam_ref, sw_ref, o_ref, qq_ref, s_ref, m_ref, l_ref, acc_ref = refs
    else:
        q_ref, k_ref, v_ref, o_ref, s_ref, m_ref, l_ref, acc_ref = refs
        qq_ref = q_ref
    i = pl.program_id(2)
    rows = n_maps * tq
    if n_maps == 2:
        q = q_ref[...]
        lane = lax.broadcasted_iota(jnp.int32, q.shape, 1)
        zero = jnp.zeros_like(q)
        qq_ref[:tq, :] = jnp.where(lane < DIFF_QK_DIM, q, zero)
        qq_ref[tq:, :] = jnp.where(lane >= DIFF_QK_DIM, q, zero)

    m_ref[...] = jnp.full(m_ref.shape, NEG_BIG, F32)
    l_ref[...] = jnp.zeros(l_ref.shape, F32)
    acc_ref[...] = jnp.zeros(acc_ref.shape, F32)

    def scores(j, g):
        start = pl.multiple_of(j * tk, tk)
        return lax.dot_general(qq_ref[g * rg:(g + 1) * rg, :], k_ref[pl.ds(start, tk), :],
                               (((1,), (1,)), ((), ())), preferred_element_type=F32)

    def step(j, slot, masked, prefetch):
        start = pl.multiple_of(j * tk, tk)
        v = v_ref[pl.ds(start, tk), :]
        def softmax(g):
            rsl = slice(g * rg, (g + 1) * rg)
            s = s_ref[slot, rsl, :]
            if masked:
                r_idx = lax.broadcasted_iota(jnp.int32, (rg, tk), 0)
                c_idx = lax.broadcasted_iota(jnp.int32, (rg, tk), 1)
                q_chunk = (i * tq + (g * rg) % tq + r_idx) // CHUNK
                k_chunk = (j * tk + c_idx) // CHUNK
                s = jnp.where(k_chunk <= q_chunk, s, NEG_BIG)
            m_prev = m_ref[rsl, :]
            m_new = jnp.maximum(m_prev, jnp.max(s, axis=-1, keepdims=True))
            alpha = jnp.exp2(m_prev - m_new)
            p = jnp.exp2(s - jnp.tile(m_new, (1, tk // LANES)))
            l_ref[rsl, :] = alpha * l_ref[rsl, :] + jnp.sum(p, axis=-1, keepdims=True)
            m_ref[rsl, :] = m_new
            return p.astype(BF16), alpha

        def weighted_values(g, p, alpha):
            rsl = slice(g * rg, (g + 1) * rg)
            acc_ref[rsl, :] = alpha * acc_ref[rsl, :] + jnp.dot(p, v, preferred_element_type=F32)

        for g in range(rows // rg):
            if prefetch:
                s_ref[1 - slot, g * rg:(g + 1) * rg, :] = scores(j + 1, g)
            weighted_values(g, *softmax(g))

    n_full = (i * tq) // tk
    n_diag = max(tq // tk, 1)

    for g in range(rows // rg):
        s_ref[0, g * rg:(g + 1) * rg, :] = scores(0, g)

    def body(jj, carry):
        step(2 * jj, 0, False, True)
        step(2 * jj + 1, 1, False, True)
        return carry

    lax.fori_loop(0, n_full // 2, body, 0)

    def tail(first_slot):
        for d in range(n_diag):
            step(n_full + d, (first_slot + d) % 2, True, d < n_diag - 1)

    @pl.when(n_full % 2 == 1)
    def _():
        step(n_full - 1, 0, False, True)
        tail(1)

    @pl.when(n_full % 2 == 0)
    def _():
        tail(0)

    o = acc_ref[...] / l_ref[...]
    if n_maps == 2:
        lp = lam_ref[...]
        lam = (jnp.exp(jnp.sum(lp[0:1] * lp[1:2], axis=-1, keepdims=True))
               - jnp.exp(jnp.sum(lp[2:3] * lp[3:4], axis=-1, keepdims=True)) + lambda_init)
        o = o[:tq] - lam * o[tq:]
        r = lax.rsqrt(jnp.mean(o * o, axis=-1, keepdims=True) + NORM_EPS)
        o = o * r * sw_ref[...] * (1.0 - lambda_init)
    o_ref[...] = o.astype(o_ref.dtype)


def _attention(q, k, v, extra, *, heads, batch, seq, tq, tk, rg, n_maps, v_col=0, lambda_init=0.0):
    q3, k3, v3 = (t.reshape(batch, seq, t.shape[1]) for t in (q, k, v))
    rows = n_maps * tq
    rg = min(rg, tq)
    vb = v_col // HEAD_W
    scratch = [pltpu.VMEM((2, rows, tk), F32), pltpu.VMEM((rows, LANES), F32), pltpu.VMEM((rows, LANES), F32),
               pltpu.VMEM((rows, HEAD_W), F32)]
    if n_maps == 2:
        scratch.insert(0, pltpu.VMEM((rows, HEAD_W), BF16))
    in_specs = [pl.BlockSpec((None, tq, HEAD_W), lambda b, h, i: (b, i, h)),
                pl.BlockSpec((None, seq, HEAD_W), lambda b, h, i: (b, 0, h)),
                pl.BlockSpec((None, seq, HEAD_W), lambda b, h, i: (b, 0, vb + h))]
    for e in extra:
        in_specs.append(pl.BlockSpec(e.shape, lambda b, h, i: (0, 0)))
    out = pl.pallas_call(
        functools.partial(_attn_kernel, tq=tq, tk=tk, rg=rg, n_maps=n_maps, lambda_init=lambda_init),
        grid=(batch, heads, seq // tq),
        in_specs=in_specs,
        out_specs=pl.BlockSpec((None, tq, HEAD_W), lambda b, h, i: (b, i, h)),
        out_shape=jax.ShapeDtypeStruct((batch, seq, heads * HEAD_W), BF16),
        scratch_shapes=scratch,
        compiler_params=_cparams("parallel", "parallel", "arbitrary"),
        name="diff_attn" if n_maps == 2 else "mla_attn",
    )(q3, k3, v3, *extra)
    return out.reshape(batch * seq, heads * HEAD_W)


def _gdn_prep_kernel(x_ref, halo_ref, ba_ref, cw_ref, ea_ref, dtb_ref, qkv_ref, gb_ref, *, tiles_per_batch):
    i = pl.program_id(0)
    x = x_ref[...].astype(F32)
    halo = halo_ref[...].astype(F32)
    halo = jnp.where(i % tiles_per_batch == 0, 0.0, halo)
    cw = cw_ref[...]
    row8 = lax.broadcasted_iota(jnp.int32, halo.shape, 0)
    y = x * cw[GDN_CONV - 1:GDN_CONV]
    for s in range(1, GDN_CONV):
        xs = pltpu.roll(x, s, 0)
        hs = pltpu.roll(halo, s, 0)
        first = jnp.where(row8 < s, hs, xs[:SUBLANES])
        xs = jnp.concatenate([first, xs[SUBLANES:]], axis=0)
        y = y + xs * cw[GDN_CONV - 1 - s:GDN_CONV - s]
    y = y * jax.nn.sigmoid(y)
    nqk = 2 * GDN_HEADS
    for h in range(3 * GDN_HEADS):
        sl = slice(h * HEAD_W, (h + 1) * HEAD_W)
        t = y[:, sl]
        if h < nqk:
            t = t * lax.rsqrt(jnp.sum(t * t, axis=-1, keepdims=True) + NORM_EPS)
            if h < GDN_HEADS:
                t = t * (GDN_K_DIM ** -0.5)
        qkv_ref[:, sl] = t
    ba = ba_ref[...].astype(F32)
    lane = lax.broadcasted_iota(jnp.int32, ba.shape, 1)
    sp_in = ba + dtb_ref[...]
    softplus = jnp.maximum(sp_in, 0.0) + jnp.log1p(jnp.exp(-jnp.abs(sp_in)))
    gb_ref[...] = jnp.where(lane < GDN_HEADS, jax.nn.sigmoid(ba), -jnp.exp(ea_ref[...]) * softplus)


def _gdn_prep(proj, conv_w, exp_a, dt_b, col_qkv, col_ba, seq, tm):
    n = proj.shape[0]
    w = 3 * GDN_HEADS * HEAD_W
    tiles_per_batch = seq // tm
    hb = tm // SUBLANES
    return pl.pallas_call(
        functools.partial(_gdn_prep_kernel, tiles_per_batch=tiles_per_batch),
        grid=(n // tm,),
        in_specs=[pl.BlockSpec((tm, w), lambda i: (i, col_qkv // w)),
                  pl.BlockSpec((SUBLANES, w), lambda i: (jnp.maximum(i * hb - 1, 0), col_qkv // w)),
                  pl.BlockSpec((tm, LANES), lambda i: (i, col_ba // LANES)),
                  pl.BlockSpec((GDN_CONV, w), lambda i: (0, 0)),
                  pl.BlockSpec((1, LANES), lambda i: (0, 0)),
                  pl.BlockSpec((1, LANES), lambda i: (0, 0))],
        out_specs=[pl.BlockSpec((tm, w), lambda i: (i, 0)),
                   pl.BlockSpec((tm, LANES), lambda i: (i, 0))],
        out_shape=[jax.ShapeDtypeStruct((n, w), F32), jax.ShapeDtypeStruct((n, LANES), F32)],
        compiler_params=_cparams("parallel"),
        name="gdn_prep",
    )(proj, proj, proj, conv_w, exp_a, dt_b)


def _gdn_intra_kernel(qkv_ref, gb_ref, u_ref, w_ref, qt_ref, ot_ref, kd_ref, eg_ref, *, tc):
    hw = GDN_HEADS * HEAD_W
    nc = tc // CHUNK
    gb = gb_ref[...]
    r_in_chunk = lax.broadcasted_iota(jnp.int32, gb.shape, 0) % CHUNK
    gc = gb
    sh = 1
    while sh < CHUNK:
        gc = gc + jnp.where(r_in_chunk >= sh, pltpu.roll(gc, sh, 0), 0.0)
        sh *= 2
    gc_t = gc.T

    row = lax.broadcasted_iota(jnp.int32, (CHUNK, CHUNK), 0)
    col = lax.broadcasted_iota(jnp.int32, (CHUNK, CHUNK), 1)
    incl = row >= col
    strict = row > col
    eye = (row == col).astype(F32)

    pairs = [(c, h) for c in range(nc) for h in range(GDN_HEADS)]
    st = {}
    for c, h in pairs:
        rs = slice(c * CHUNK, (c + 1) * CHUNK)
        hs = slice(h * HEAD_W, (h + 1) * HEAD_W)
        q = qkv_ref[rs, h * HEAD_W:(h + 1) * HEAD_W]
        k = qkv_ref[rs, hw + h * HEAD_W:hw + (h + 1) * HEAD_W]
        v = qkv_ref[rs, 2 * hw + h * HEAD_W:2 * hw + (h + 1) * HEAD_W]
        beta = gb[rs, h:h + 1]
        g_col = gc[rs, GDN_HEADS + h:GDN_HEADS + h + 1]
        g_row = gc_t[GDN_HEADS + h:GDN_HEADS + h + 1, rs]
        g_last = g_col[CHUNK - 1:CHUNK, :]
        k_beta = k * beta
        e_g = jnp.exp(g_col)
        kd_ref[rs, hs] = k * jnp.exp(g_last - g_col)
        eg_ref[c:c + 1, hs] = jnp.broadcast_to(jnp.exp(g_last), (1, HEAD_W))
        st[c, h] = dict(
            kb16=k_beta.astype(BF16), k16=k.astype(BF16), q16=q.astype(BF16),
            decay=jnp.exp(jnp.where(incl, g_col - g_row, NEG_BIG)),
            rhs=jnp.concatenate([v * beta, k_beta * e_g], axis=1).astype(BF16),
            qg=q * e_g)
    for p in pairs:
        s = st[p]
        s["lmat"] = jnp.where(strict, _mm_nt(s["kb16"], s["k16"]) * s["decay"], 0.0)
        s["attn"] = jnp.where(incl, _mm_nt(s["q16"], s["k16"]) * s["decay"], 0.0).astype(BF16)
        s["t"] = eye - jnp.where((row // 2) == (col // 2), s["lmat"], 0.0)
    b = 2
    while b < CHUNK:
        off = ((row // (2 * b)) == (col // (2 * b))) & ((row // b) % 2 == 1) & ((col // b) % 2 == 0)
        for p in pairs:
            s = st[p]
            s["ct"] = _mm(jnp.where(off, s["lmat"], 0.0), s["t"])
        for p in pairs:
            s = st[p]
            s["t"] = s["t"] - _mm(s["t"], s["ct"])
        b *= 2
    for p in pairs:
        s = st[p]
        s["uw"] = _mm(s["t"], s["rhs"])
    for c, h in pairs:
        s = st[c, h]
        rs = slice(c * CHUNK, (c + 1) * CHUNK)
        hs = slice(h * HEAD_W, (h + 1) * HEAD_W)
        a_uw = _mm(s["attn"], s["uw"])
        u_ref[rs, hs] = s["uw"][:, :HEAD_W]
        w_ref[rs, hs] = s["uw"][:, HEAD_W:].astype(BF16)
        ot_ref[rs, hs] = a_uw[:, :HEAD_W]
        qt_ref[rs, hs] = (s["qg"] - a_uw[:, HEAD_W:]).astype(BF16)


def _gdn_intra(qkv, gb, tc):
    n = qkv.shape[0]
    hw = GDN_HEADS * HEAD_W
    nc = tc // CHUNK
    blk = pl.BlockSpec((tc, hw), lambda i: (i, 0))
    return pl.pallas_call(
        functools.partial(_gdn_intra_kernel, tc=tc),
        grid=(n // tc,),
        in_specs=[pl.BlockSpec((tc, 3 * hw), lambda i: (i, 0)),
                  pl.BlockSpec((tc, LANES), lambda i: (i, 0))],
        out_specs=[blk, blk, blk, blk, blk, pl.BlockSpec((nc, hw), lambda i: (i, 0))],
        out_shape=[jax.ShapeDtypeStruct((n, hw), F32), jax.ShapeDtypeStruct((n, hw), BF16),
                   jax.ShapeDtypeStruct((n, hw), BF16), jax.ShapeDtypeStruct((n, hw), F32),
                   jax.ShapeDtypeStruct((n, hw), F32), jax.ShapeDtypeStruct((n // CHUNK, hw), F32)],
        compiler_params=_cparams("parallel"),
        name="gdn_intra",
    )(qkv, gb)


def _gdn_scan_kernel(u_ref, w_ref, qt_ref, ot_ref, kd_ref, eg_ref, z_ref, ow_ref, o_ref, state_ref, *, tc, batch):
    @pl.when(pl.program_id(0) == 0)
    def _():
        state_ref[...] = jnp.zeros(state_ref.shape, F32)

    chains = [(b, h) for b in range(batch) for h in range(GDN_HEADS)]
    for c in range(tc // CHUNK):
        rs = slice(c * CHUNK, (c + 1) * CHUNK)
        prod, v_new = {}, {}
        for b, h in chains:
            hs = slice(h * HEAD_W, (h + 1) * HEAD_W)
            lhs = jnp.concatenate([w_ref[b, rs, hs], qt_ref[b, rs, hs]], axis=0)
            prod[b, h] = jnp.dot(lhs, state_ref[b * GDN_HEADS + h].astype(BF16), preferred_element_type=F32)
        for b, h in chains:
            hs = slice(h * HEAD_W, (h + 1) * HEAD_W)
            v_new[b, h] = (u_ref[b, rs, hs] - prod[b, h][:CHUNK]).astype(BF16)
        for b, h in chains:
            hs = slice(h * HEAD_W, (h + 1) * HEAD_W)
            idx = b * GDN_HEADS + h
            k_t = kd_ref[b, rs, hs].T.astype(BF16)
            state_ref[idx] = (state_ref[idx] * eg_ref[b, c:c + 1, hs]
                              + jnp.dot(k_t, v_new[b, h], preferred_element_type=F32))
        for b, h in chains:
            hs = slice(h * HEAD_W, (h + 1) * HEAD_W)
            o = prod[b, h][CHUNK:] + ot_ref[b, rs, hs]
            o = o * lax.rsqrt(jnp.mean(o * o, axis=-1, keepdims=True) + NORM_EPS) * ow_ref[...]
            z = z_ref[b, rs, hs].astype(F32)
            o_ref[b, rs, hs] = (o * (z * jax.nn.sigmoid(z))).astype(o_ref.dtype)


def _gdn_scan(u, w, qt, ot, kd, eg, proj, onorm_w, col_z, batch, seq, tc):
    hw = GDN_HEADS * HEAD_W
    nc = tc // CHUNK
    r3 = lambda t: t.reshape(batch, t.shape[0] // batch, t.shape[1])
    blk = pl.BlockSpec((batch, tc, hw), lambda i: (0, i, 0))
    out = pl.pallas_call(
        functools.partial(_gdn_scan_kernel, tc=tc, batch=batch),
        grid=(seq // tc,),
        in_specs=[blk, blk, blk, blk, blk,
                  pl.BlockSpec((batch, nc, hw), lambda i: (0, i, 0)),
                  pl.BlockSpec((batch, tc, hw), lambda i: (0, i, col_z // hw)),
                  pl.BlockSpec((1, HEAD_W), lambda i: (0, 0))],
        out_specs=blk,
        out_shape=jax.ShapeDtypeStruct((batch, seq, hw), BF16),
        scratch_shapes=[pltpu.VMEM((batch * GDN_HEADS, GDN_K_DIM, GDN_V_DIM), F32)],
        compiler_params=_cparams("arbitrary"),
        name="gdn_scan",
    )(r3(u), r3(w), r3(qt), r3(ot), r3(kd), r3(eg), r3(proj), onorm_w)
    return out.reshape(batch * seq, hw)


def _merge_kernel(x_ref, oa_ref, ob_ref, oc_ref, gate_ref, wa_ref, wb_ref, wc_ref, wout_ref, mod_ref, o_ref):
    d = x_ref.shape[1]
    merged = None
    for idx, (o_r, w_r) in enumerate(((oa_ref, wa_ref), (ob_ref, wb_ref), (oc_ref, wc_ref))):
        y = jnp.dot(o_r[...], w_r[...], preferred_element_type=F32)
        g = jax.nn.sigmoid(gate_ref[:, idx * d:(idx + 1) * d].astype(F32))
        merged = g * y if merged is None else merged + g * y
    upd = jnp.dot(merged.astype(BF16), wout_ref[...], preferred_element_type=F32)
    o_ref[...] = x_ref[...] + mod_ref[2:3, :] * upd


def _merge(x2, oa, ob, oc, proj, wa, wb, wc, wout, mod, col_gate, seq, tm):
    n, d = x2.shape
    w = DIFF_HEADS * HEAD_W
    tiles_per_batch = seq // tm
    row = lambda i: (i, 0)
    const = lambda i: (0, 0)
    return pl.pallas_call(
        _merge_kernel,
        grid=(n // tm,),
        in_specs=[pl.BlockSpec((tm, d), row), pl.BlockSpec((tm, w), row), pl.BlockSpec((tm, w), row),
                  pl.BlockSpec((tm, w), row),
                  pl.BlockSpec((tm, N_BRANCHES * d), lambda i: (i, col_gate // (N_BRANCHES * d))),
                  pl.BlockSpec((w, d), const), pl.BlockSpec((w, d), const), pl.BlockSpec((w, d), const),
                  pl.BlockSpec((d, d), const),
                  pl.BlockSpec((None, 6, d), lambda i: (i // tiles_per_batch, 0, 0))],
        out_specs=pl.BlockSpec((tm, d), row),
        out_shape=jax.ShapeDtypeStruct((n, d), F32),
        compiler_params=_cparams("parallel"),
        name="merge",
    )(x2, oa, ob, oc, proj, wa, wb, wc, wout, mod)


def _ffn_kernel(x_ref, nw_ref, mod_ref, wg_ref, wu_ref, wd_ref, o_ref, h_ref, acc_ref):
    j = pl.program_id(1)

    @pl.when(j == 0)
    def _():
        h_ref[...] = _norm_mod(x_ref[...], nw_ref[...], mod_ref[...], 3, 4).astype(BF16)
        acc_ref[...] = jnp.zeros(acc_ref.shape, F32)

    h = h_ref[...]
    gate = jnp.dot(h, wg_ref[...], preferred_element_type=F32)
    up = jnp.dot(h, wu_ref[...], preferred_element_type=F32)
    act = (gate * jax.nn.sigmoid(gate) * up).astype(BF16)
    acc_ref[...] += jnp.dot(act, wd_ref[...], preferred_element_type=F32)

    @pl.when(j == pl.num_programs(1) - 1)
    def _():
        o_ref[...] = x_ref[...] + mod_ref[5:6, :] * acc_ref[...]


def _ffn(x2, nw, mod, w_gu, w_down, seq, tm, th):
    n, d = x2.shape
    hidden = w_down.shape[0]
    nh = hidden // th
    tiles_per_batch = seq // tm
    return pl.pallas_call(
        _ffn_kernel,
        grid=(n // tm, nh),
        in_specs=[pl.BlockSpec((tm, d), lambda i, j: (i, 0)),
                  pl.BlockSpec((1, d), lambda i, j: (0, 0)),
                  pl.BlockSpec((None, 6, d), lambda i, j: (i // tiles_per_batch, 0, 0)),
                  pl.BlockSpec((d, th), lambda i, j: (0, j)),
                  pl.BlockSpec((d, th), lambda i, j: (0, nh + j)),
                  pl.BlockSpec((th, d), lambda i, j: (j, 0))],
        out_specs=pl.BlockSpec((tm, d), lambda i, j: (i, 0)),
        out_shape=jax.ShapeDtypeStruct((n, d), F32),
        scratch_shapes=[pltpu.VMEM((tm, d), BF16), pltpu.VMEM((tm, d), F32)],
        compiler_params=_cparams("parallel", "arbitrary"),
        name="ffn",
    )(x2, nw, mod, w_gu, w_gu, w_down)


def _pad_cols(w, width):
    return jnp.pad(w, ((0, 0), (0, width - w.shape[1])))


def _in_proj_layout(d_model):
    a = DIFF_HEADS * HEAD_W
    g = GDN_HEADS * HEAD_W
    cols = {}
    off = 0
    for name, width in (("gate", N_BRANCHES * d_model), ("c_qkv", 3 * g), ("a_q", a), ("a_k", a), ("a_v", a),
                        ("c_z", g), ("c_ba", LANES), ("b_q", MLA_Q_LORA), ("b_kv", MLA_Q_LORA)):
        if off % width:
            off += width - off % width
        cols[name] = off
        off += width
    return cols, off


def _build_w_in(w_in, d_model, total):
    a = DIFF_HEADS * 2 * DIFF_QK_DIM
    g = GDN_HEADS * GDN_K_DIM
    widths = (a, a, DIFF_HEADS * DIFF_V_DIM, MLA_Q_LORA, MLA_KV_LORA + MLA_ROPE, g, g, GDN_HEADS * GDN_V_DIM,
              GDN_HEADS * GDN_V_DIM, GDN_HEADS, GDN_HEADS, N_BRANCHES * d_model)
    offs = np.concatenate([[0], np.cumsum(widths)])
    part = [w_in[:, int(offs[i]):int(offs[i + 1])] for i in range(len(widths))]
    cols, _ = _in_proj_layout(d_model)
    out = jnp.zeros((w_in.shape[0], total), w_in.dtype)
    place = (("a_q", part[0]), ("a_k", part[1]), ("a_v", part[2]), ("b_q", part[3]), ("b_kv", part[4]),
             ("c_qkv", jnp.concatenate(part[5:8], axis=1)), ("c_z", part[8]), ("gate", part[11]),
             ("c_ba", jnp.concatenate(part[9:11], axis=1)))
    for name, p in place:
        out = lax.dynamic_update_slice(out, p, (0, cols[name]))
    return out.astype(BF16)


def _head_pad(w, heads, per_head, lane_off=0):
    r = w.shape[0]
    w = w.reshape(r, heads, per_head)
    w = jnp.pad(w, ((0, 0), (0, 0), (lane_off, HEAD_W - per_head - lane_off)))
    return w.reshape(r, heads * HEAD_W)


def kernel(x, c, positions, ada_w, ada_b, norm1_w, w_in, diff_qnorm_w, diff_knorm_w, diff_lambda, diff_subln_w, w_o_diff, mla_qa_norm_w, mla_q_up, mla_kva_norm_w, mla_kv_up, mla_qnorm_w, mla_knorm_w, w_o_mla, gdn_conv_w, gdn_a_log, gdn_dt_bias, gdn_onorm_w, w_o_gdn, w_out, norm2_w, ffn_w_gu, ffn_w_down):
    batch, seq, d = x.shape
    depth = ada_w.shape[0]
    n = batch * seq
    qk_dim = MLA_NOPE + MLA_ROPE

    tm = min(1024, seq)
    tq = min(512, seq)
    tc = min(512, seq)

    c_pad = jnp.pad(c, ((0, SUBLANES - batch), (0, 0)))
    mod_all = _ada_mod(c_pad, ada_w, ada_b)[:, :batch].reshape(depth, batch, 6, d)

    cols, total = _in_proj_layout(d)
    tn = 512
    total = -(-total // tn) * tn

    pos2 = positions.reshape(n, 1)
    half = MLA_ROPE // 2
    inv_freq = ROPE_THETA ** (-jnp.arange(half, dtype=F32) / half)
    invf = jnp.zeros((1, HEAD_W), F32).at[0, MLA_NOPE:qk_dim].set(jnp.concatenate([inv_freq, inv_freq]))

    x2 = x.reshape(n, d)
    for l in range(depth):
        lambda_init = 0.8 - 0.6 * math.exp(-0.3 * l)
        mod = mod_all[l]
        proj = _in_proj(x2, norm1_w[l][None], mod, _build_w_in(w_in[l], d, total), seq, tm, tn)

        qw2 = jnp.tile(diff_qnorm_w[l], 2)[None]
        kw2 = jnp.tile(diff_knorm_w[l], 2)[None]
        a_q, a_k = _diff_prep(proj, qw2, kw2, cols["a_q"], cols["a_k"], tm)
        o_a = _attention(a_q, a_k, proj, (diff_lambda[l], diff_subln_w[l][None]), heads=DIFF_HEADS, batch=batch,
                         seq=seq, tq=tq // 2, tk=tq, rg=128, n_maps=2, v_col=cols["a_v"], lambda_init=lambda_init)

        kv_up = mla_kv_up[l].reshape(MLA_KV_LORA, MLA_HEADS, MLA_NOPE + MLA_V_DIM)
        wk_nope = _head_pad(kv_up[:, :, :MLA_NOPE].reshape(MLA_KV_LORA, -1), MLA_HEADS, MLA_NOPE)
        place_rope = _head_pad(jnp.tile(jnp.eye(MLA_ROPE, dtype=F32), (1, MLA_HEADS)), MLA_HEADS, MLA_ROPE, MLA_NOPE)
        pad_rows = MLA_Q_LORA - MLA_KV_LORA - MLA_ROPE
        wk = jnp.concatenate([wk_nope, place_rope, jnp.zeros((pad_rows, MLA_HEADS * HEAD_W), F32)], axis=0)
        wv = jnp.pad(kv_up[:, :, MLA_NOPE:].reshape(MLA_KV_LORA, -1), ((0, MLA_Q_LORA - MLA_KV_LORA), (0, 0)))
        kva_w = jnp.pad(mla_kva_norm_w[l], (0, MLA_Q_LORA - MLA_KV_LORA))[None]
        b_q, b_k, b_v = _mla_prep(
            proj, pos2, mla_qa_norm_w[l][None], _head_pad(mla_q_up[l], MLA_HEADS, qk_dim).astype(BF16),
            kva_w, wk.astype(BF16), wv.astype(BF16),
            jnp.pad(mla_qnorm_w[l], (0, HEAD_W - qk_dim))[None], jnp.pad(mla_knorm_w[l], (0, HEAD_W - qk_dim))[None],
            invf, cols["b_q"], cols["b_kv"], tm)
        o_b = _attention(b_q, b_k, b_v, (), heads=MLA_HEADS, batch=batch, seq=seq, tq=tq, tk=tq,
                         rg=128, n_maps=1)

        exp_a = jnp.zeros((1, LANES), F32).at[0, GDN_HEADS:2 * GDN_HEADS].set(gdn_a_log[l])
        dt_b = jnp.zeros((1, LANES), F32).at[0, GDN_HEADS:2 * GDN_HEADS].set(gdn_dt_bias[l])
        c_qkv, c_gb = _gdn_prep(proj, gdn_conv_w[l], exp_a, dt_b, cols["c_qkv"], cols["c_ba"], seq, tm)
        g_u, g_w, g_qt, g_ot, g_kd, g_eg = _gdn_intra(c_qkv, c_gb, tc)
        o_c = _gdn_scan(g_u, g_w, g_qt, g_ot, g_kd, g_eg, proj, gdn_onorm_w[l][None], cols["c_z"], batch, seq, tc)

        x2 = _merge(x2, o_a, o_b, o_c, proj, w_o_diff[l].astype(BF16), w_o_mla[l].astype(BF16),
                    w_o_gdn[l].astype(BF16), w_out[l].astype(BF16), mod, cols["gate"], seq, min(512, seq))
        x2 = _ffn(x2, norm2_w[l][None], mod, ffn_w_gu[l].astype(BF16), ffn_w_down[l].astype(BF16), seq, tm, 256)
    return x2.reshape(batch, seq, d)
```

```python
import functools
import math

import jax
import jax.numpy as jnp
import numpy as np
from jax import lax
from jax.experimental import pallas as pl
from jax.experimental.pallas import tpu as pltpu

F32 = jnp.float32
BF16 = jnp.bfloat16

CHUNK = 64
NORM_EPS = 1e-6
ROPE_THETA = 10000.0
DIFF_HEADS = 4
DIFF_QK_DIM = 64
DIFF_V_DIM = 128
MLA_HEADS = 4
MLA_Q_LORA = 384
MLA_KV_LORA = 256
MLA_NOPE = 64
MLA_ROPE = 32
MLA_V_DIM = 128
GDN_HEADS = 4
GDN_K_DIM = 128
GDN_V_DIM = 128
GDN_CONV = 4
N_BRANCHES = 3

LANES = 128
SUBLANES = 8
HEAD_W = 128
LOG2E = 1.4426950408889634
NEG_BIG = -1e30
VMEM_LIMIT = 56 * 1024 * 1024


def _cparams(*sem):
    return pltpu.CompilerParams(dimension_semantics=sem, vmem_limit_bytes=VMEM_LIMIT)


def _mm(a, b):
    return jnp.dot(a.astype(BF16), b.astype(BF16), preferred_element_type=F32)


def _mm_nt(a, b):
    return lax.dot_general(a.astype(BF16), b.astype(BF16), (((1,), (1,)), ((), ())),
                           preferred_element_type=F32)


def _mod_kernel(c_ref, w_ref, b_ref, o_ref):
    o_ref[...] = jnp.dot(c_ref[...], w_ref[...], preferred_element_type=F32) + b_ref[...]


def _ada_mod(c_pad, ada_w, ada_b):
    depth, d, n6 = ada_w.shape
    tn = 1536
    return pl.pallas_call(
        _mod_kernel,
        grid=(depth, n6 // tn),
        in_specs=[pl.BlockSpec((SUBLANES, d), lambda l, j: (0, 0)),
                  pl.BlockSpec((None, d, tn), lambda l, j: (l, 0, j)),
                  pl.BlockSpec((None, 1, tn), lambda l, j: (l, 0, j))],
        out_specs=pl.BlockSpec((None, SUBLANES, tn), lambda l, j: (l, 0, j)),
        out_shape=jax.ShapeDtypeStruct((depth, SUBLANES, n6), F32),
        compiler_params=_cparams("parallel", "parallel"),
        name="ada_mod",
    )(c_pad, ada_w, ada_b.reshape(depth, 1, n6))


def _norm_mod(x, nw, mod, shift_row, scale_row):
    r = lax.rsqrt(jnp.mean(x * x, axis=-1, keepdims=True) + NORM_EPS)
    h = x * r * nw
    return h * (1.0 + mod[scale_row:scale_row + 1, :]) + mod[shift_row:shift_row + 1, :]


def _in_proj_kernel(x_ref, nw_ref, mod_ref, w_ref, o_ref, h_ref):
    @pl.when(pl.program_id(1) == 0)
    def _():
        h_ref[...] = _norm_mod(x_ref[...], nw_ref[...], mod_ref[...], 0, 1).astype(BF16)

    o_ref[...] = jnp.dot(h_ref[...], w_ref[...], preferred_element_type=F32).astype(o_ref.dtype)


def _in_proj(x2, nw, mod, w, seq, tm, tn):
    n, d = x2.shape
    cols = w.shape[1]
    tiles_per_batch = seq // tm
    return pl.pallas_call(
        _in_proj_kernel,
        grid=(n // tm, cols // tn),
        in_specs=[pl.BlockSpec((tm, d), lambda i, j: (i, 0)),
                  pl.BlockSpec((1, d), lambda i, j: (0, 0)),
                  pl.BlockSpec((None, 6, d), lambda i, j: (i // tiles_per_batch, 0, 0)),
                  pl.BlockSpec((d, tn), lambda i, j: (0, j))],
        out_specs=pl.BlockSpec((tm, tn), lambda i, j: (i, j)),
        out_shape=jax.ShapeDtypeStruct((n, cols), BF16),
        scratch_shapes=[pltpu.VMEM((tm, d), BF16)],
        compiler_params=_cparams("parallel", "arbitrary"),
        name="in_proj",
    )(x2, nw, mod, w)


def _half_rms(t, w):
    lane = lax.broadcasted_iota(jnp.int32, t.shape, 1)
    lo = lane < DIFF_QK_DIM
    sq = t * t
    s_lo = jnp.sum(jnp.where(lo, sq, 0.0), axis=-1, keepdims=True)
    s_hi = jnp.sum(jnp.where(lo, 0.0, sq), axis=-1, keepdims=True)
    r = lax.rsqrt(jnp.where(lo, s_lo, s_hi) * (1.0 / DIFF_QK_DIM) + NORM_EPS)
    return t * r * w


def _diff_prep_kernel(q_ref, k_ref, qw_ref, kw_ref, qo_ref, ko_ref):
    qscale = (DIFF_QK_DIM ** -0.5) * LOG2E
    for h in range(DIFF_HEADS):
        sl = slice(h * HEAD_W, (h + 1) * HEAD_W)
        qo_ref[:, sl] = (_half_rms(q_ref[:, sl].astype(F32), qw_ref[...]) * qscale).astype(BF16)
        ko_ref[:, sl] = _half_rms(k_ref[:, sl].astype(F32), kw_ref[...]).astype(BF16)


def _diff_prep(proj, qw2, kw2, col_q, col_k, tm):
    n = proj.shape[0]
    w = DIFF_HEADS * HEAD_W
    return pl.pallas_call(
        _diff_prep_kernel,
        grid=(n // tm,),
        in_specs=[pl.BlockSpec((tm, w), lambda i: (i, col_q // w)),
                  pl.BlockSpec((tm, w), lambda i: (i, col_k // w)),
                  pl.BlockSpec((1, HEAD_W), lambda i: (0, 0)),
                  pl.BlockSpec((1, HEAD_W), lambda i: (0, 0))],
        out_specs=[pl.BlockSpec((tm, w), lambda i: (i, 0)),
                   pl.BlockSpec((tm, w), lambda i: (i, 0))],
        out_shape=[jax.ShapeDtypeStruct((n, w), BF16), jax.ShapeDtypeStruct((n, w), BF16)],
        compiler_params=_cparams("parallel"),
        name="diff_prep",
    )(proj, proj, qw2, kw2)


def _rope_tile(t, cos_t, sin_lo, sin_hi):
    half = MLA_ROPE // 2
    return (t * cos_t + pltpu.roll(t, HEAD_W - half, 1) * sin_lo + pltpu.roll(t, half, 1) * sin_hi)


def _mla_prep_kernel(q_ref, kv_ref, pos_ref, qa_ref, qup_ref, kva_ref, wk_ref, wv_ref,
                     qn_ref, kn_ref, invf_ref, qo_ref, ko_ref, vo_ref):
    qk_dim = MLA_NOPE + MLA_ROPE
    qscale = (qk_dim ** -0.5) * LOG2E
    half = MLA_ROPE // 2

    xq = q_ref[...].astype(F32)
    rq = lax.rsqrt(jnp.mean(xq * xq, axis=-1, keepdims=True) + NORM_EPS)
    q = _mm(xq * rq * qa_ref[...], qup_ref[...])

    xkv = kv_ref[...].astype(F32)
    lane_kv = lax.broadcasted_iota(jnp.int32, xkv.shape, 1)
    is_lat = lane_kv < MLA_KV_LORA
    ssq = jnp.sum(jnp.where(is_lat, xkv * xkv, 0.0), axis=-1, keepdims=True)
    rkv = lax.rsqrt(ssq * (1.0 / MLA_KV_LORA) + NORM_EPS)
    lhs = jnp.where(is_lat, xkv * rkv * kva_ref[...], xkv).astype(BF16)
    k = jnp.dot(lhs, wk_ref[...], preferred_element_type=F32)
    vo_ref[...] = jnp.dot(lhs, wv_ref[...], preferred_element_type=F32).astype(BF16)

    ang = pos_ref[...].astype(F32) * invf_ref[...]
    lane = lax.broadcasted_iota(jnp.int32, ang.shape, 1)
    in_lo = (lane >= MLA_NOPE) & (lane < MLA_NOPE + half)
    in_hi = (lane >= MLA_NOPE + half) & (lane < qk_dim)
    cos_a, sin_a = jnp.cos(ang), jnp.sin(ang)
    cos_t = jnp.where(in_lo | in_hi, cos_a, 1.0)
    sin_lo = jnp.where(in_lo, -sin_a, 0.0)
    sin_hi = jnp.where(in_hi, sin_a, 0.0)

    for h in range(MLA_HEADS):
        sl = slice(h * HEAD_W, (h + 1) * HEAD_W)
        qh = q[:, sl]
        qh = qh * lax.rsqrt(jnp.sum(qh * qh, axis=-1, keepdims=True) * (1.0 / qk_dim) + NORM_EPS)
        qh = _rope_tile(qh * qn_ref[...], cos_t, sin_lo, sin_hi)
        qo_ref[:, sl] = (qh * qscale).astype(BF16)
        kh = k[:, sl]
        kh = kh * lax.rsqrt(jnp.sum(kh * kh, axis=-1, keepdims=True) * (1.0 / qk_dim) + NORM_EPS)
        kh = _rope_tile(kh * kn_ref[...], cos_t, sin_lo, sin_hi)
        ko_ref[:, sl] = kh.astype(BF16)


def _mla_prep(proj, pos2, qa_w, qup, kva_w, wk, wv, qn_w, kn_w, invf, col_q, col_kv, tm):
    n = proj.shape[0]
    w = MLA_HEADS * HEAD_W
    lat = MLA_Q_LORA
    const = lambda i: (0, 0)
    out = jax.ShapeDtypeStruct((n, w), BF16)
    return pl.pallas_call(
        _mla_prep_kernel,
        grid=(n // tm,),
        in_specs=[pl.BlockSpec((tm, lat), lambda i: (i, col_q // lat)),
                  pl.BlockSpec((tm, lat), lambda i: (i, col_kv // lat)),
                  pl.BlockSpec((tm, 1), lambda i: (i, 0)),
                  pl.BlockSpec((1, lat), const),
                  pl.BlockSpec((lat, w), const),
                  pl.BlockSpec((1, lat), const),
                  pl.BlockSpec((lat, w), const),
                  pl.BlockSpec((lat, w), const),
                  pl.BlockSpec((1, HEAD_W), const),
                  pl.BlockSpec((1, HEAD_W), const),
                  pl.BlockSpec((1, HEAD_W), const)],
        out_specs=[pl.BlockSpec((tm, w), lambda i: (i, 0))] * 3,
        out_shape=[out, out, out],
        compiler_params=_cparams("parallel"),
        name="mla_prep",
    )(proj, proj, pos2, qa_w, qup, kva_w, wk, wv, qn_w, kn_w, invf)


def _attn_kernel(*refs, tq, tk, rg, n_maps, lambda_init):
    if n_maps == 2:
        q_ref, k_ref, v_ref, lam_ref, sw_ref, o_ref, qq_ref, s_ref, mx_ref, m_ref, l_ref, acc_ref = refs
    else:
        q_ref, k_ref, v_ref, o_ref, s_ref, mx_ref, m_ref, l_ref, acc_ref = refs
        qq_ref = q_ref
    i = pl.program_id(2)
    rows = n_maps * tq
    if n_maps == 2:
        q = q_ref[...]
        lane = lax.broadcasted_iota(jnp.int32, q.shape, 1)
        zero = jnp.zeros_like(q)
        qq_ref[:tq, :] = jnp.where(lane < DIFF_QK_DIM, q, zero)
        qq_ref[tq:, :] = jnp.where(lane >= DIFF_QK_DIM, q, zero)

    m_ref[...] = jnp.full(m_ref.shape, NEG_BIG, F32)
    l_ref[...] = jnp.zeros(l_ref.shape, F32)
    acc_ref[...] = jnp.zeros(acc_ref.shape, F32)

    n_groups = rows // rg
    n_full = (i * tq) // tk
    n_diag = max(tq // tk, 1)

    def produce(j, slot, g, masked):
        rsl = slice(g * rg, (g + 1) * rg)
        start = pl.multiple_of(j * tk, tk)
        s = lax.dot_general(qq_ref[rsl, :], k_ref[pl.ds(start, tk), :],
                            (((1,), (1,)), ((), ())), preferred_element_type=F32)
        if masked:
            r_idx = lax.broadcasted_iota(jnp.int32, (rg, tk), 0)
            c_idx = lax.broadcasted_iota(jnp.int32, (rg, tk), 1)
            q_chunk = (i * tq + (g * rg) % tq + r_idx) // CHUNK
            k_chunk = (j * tk + c_idx) // CHUNK
            s = jnp.where(k_chunk <= q_chunk, s, NEG_BIG)
        s_ref[slot, rsl, :] = s
        mx_ref[slot, rsl, :] = jnp.broadcast_to(jnp.max(s, axis=-1, keepdims=True), (rg, LANES))

    def consume(j, slot, g, v):
        rsl = slice(g * rg, (g + 1) * rg)
        m_prev = m_ref[rsl, :]
        m_new = jnp.maximum(m_prev, mx_ref[slot, rsl, :])
        alpha = jnp.exp2(m_prev - m_new)
        p = jnp.exp2(s_ref[slot, rsl, :] - jnp.tile(m_new, (1, tk // LANES)))
        p_sum = p[:, :LANES]
        for t in range(1, tk // LANES):
            p_sum = p_sum + p[:, t * LANES:(t + 1) * LANES]
        l_ref[rsl, :] = alpha * l_ref[rsl, :] + p_sum
        m_ref[rsl, :] = m_new
        acc_ref[rsl, :] = alpha * acc_ref[rsl, :] + jnp.dot(p.astype(BF16), v, preferred_element_type=F32)

    def step(j, slot, next_masked):
        start = pl.multiple_of(j * tk, tk)
        v = v_ref[pl.ds(start, tk), :]
        for g in range(n_groups):
            if next_masked is not None:
                produce(j + 1, 1 - slot, g, next_masked)
            consume(j, slot, g, v)

    for g in range(n_groups):
        produce(0, 0, g, True)

    def body(jj, carry):
        step(2 * jj, 0, False)
        step(2 * jj + 1, 1, False)
        return carry

    n_pairs = jnp.maximum((n_full - 1) // 2, 0)
    lax.fori_loop(0, n_pairs, body, 0)
    left = n_full - 2 * n_pairs

    def tail(first_slot):
        for d in range(n_diag):
            step(n_full + d, (first_slot + d) % 2, True if d < n_diag - 1 else None)

    @pl.when(left == 0)
    def _():
        tail(0)

    @pl.when(left == 1)
    def _():
        step(n_full - 1, 0, True)
        tail(1)

    @pl.when(left == 2)
    def _():
        step(n_full - 2, 0, False)
        step(n_full - 1, 1, True)
        tail(0)

    o = acc_ref[...] / jnp.sum(l_ref[...], axis=-1, keepdims=True)
    if n_maps == 2:
        lp = lam_ref[...]
        lam = (jnp.exp(jnp.sum(lp[0:1] * lp[1:2], axis=-1, keepdims=True))
               - jnp.exp(jnp.sum(lp[2:3] * lp[3:4], axis=-1, keepdims=True)) + lambda_init)
        o = o[:tq] - lam * o[tq:]
        r = lax.rsqrt(jnp.mean(o * o, axis=-1, keepdims=True) + NORM_EPS)
        o = o * r * sw_ref[...] * (1.0 - lambda_init)
    o_ref[...] = o.astype(o_ref.dtype)


def _attention(q, k, v, extra, *, heads, batch, seq, tq, tk, rg, n_maps, v_col=0, lambda_init=0.0):
    q3, k3, v3 = (t.reshape(batch, seq, t.shape[1]) for t in (q, k, v))
    rows = n_maps * tq
    rg = min(rg, tq)
    vb = v_col // HEAD_W
    scratch = [pltpu.VMEM((2, rows, tk), F32), pltpu.VMEM((2, rows, LANES), F32), pltpu.VMEM((rows, LANES), F32),
               pltpu.VMEM((rows, LANES), F32), pltpu.VMEM((rows, HEAD_W), F32)]
    if n_maps == 2:
        scratch.insert(0, pltpu.VMEM((rows, HEAD_W), BF16))
    in_specs = [pl.BlockSpec((None, tq, HEAD_W), lambda b, h, i: (b, i, h)),
                pl.BlockSpec((None, seq, HEAD_W), lambda b, h, i: (b, 0, h)),
                pl.BlockSpec((None, seq, HEAD_W), lambda b, h, i: (b, 0, vb + h))]
    for e in extra:
        in_specs.append(pl.BlockSpec(e.shape, lambda b, h, i: (0, 0)))
    out = pl.pallas_call(
        functools.partial(_attn_kernel, tq=tq, tk=tk, rg=rg, n_maps=n_maps, lambda_init=lambda_init),
        grid=(batch, heads, seq // tq),
        in_specs=in_specs,
        out_specs=pl.BlockSpec((None, tq, HEAD_W), lambda b, h, i: (b, i, h)),
        out_shape=jax.ShapeDtypeStruct((batch, seq, heads * HEAD_W), BF16),
        scratch_shapes=scratch,
        compiler_params=_cparams("parallel", "parallel", "arbitrary"),
        name="diff_attn" if n_maps == 2 else "mla_attn",
    )(q3, k3, v3, *extra)
    return out.reshape(batch * seq, heads * HEAD_W)


def _gdn_prep_kernel(x_ref, halo_ref, ba_ref, cw_ref, ea_ref, dtb_ref, qkv_ref, gb_ref, *, tiles_per_batch):
    i = pl.program_id(0)
    x = x_ref[...].astype(F32)
    halo = halo_ref[...].astype(F32)
    halo = jnp.where(i % tiles_per_batch == 0, 0.0, halo)
    cw = cw_ref[...]
    row8 = lax.broadcasted_iota(jnp.int32, halo.shape, 0)
    y = x * cw[GDN_CONV - 1:GDN_CONV]
    for s in range(1, GDN_CONV):
        xs = pltpu.roll(x, s, 0)
        hs = pltpu.roll(halo, s, 0)
        first = jnp.where(row8 < s, hs, xs[:SUBLANES])
        xs = jnp.concatenate([first, xs[SUBLANES:]], axis=0)
        y = y + xs * cw[GDN_CONV - 1 - s:GDN_CONV - s]
    y = y * jax.nn.sigmoid(y)
    nqk = 2 * GDN_HEADS
    for h in range(3 * GDN_HEADS):
        sl = slice(h * HEAD_W, (h + 1) * HEAD_W)
        t = y[:, sl]
        if h < nqk:
            t = t * lax.rsqrt(jnp.sum(t * t, axis=-1, keepdims=True) + NORM_EPS)
            if h < GDN_HEADS:
                t = t * (GDN_K_DIM ** -0.5)
        qkv_ref[:, sl] = t
    ba = ba_ref[...].astype(F32)
    lane = lax.broadcasted_iota(jnp.int32, ba.shape, 1)
    sp_in = ba + dtb_ref[...]
    softplus = jnp.maximum(sp_in, 0.0) + jnp.log1p(jnp.exp(-jnp.abs(sp_in)))
    gb_ref[...] = jnp.where(lane < GDN_HEADS, jax.nn.sigmoid(ba), -jnp.exp(ea_ref[...]) * softplus)


def _gdn_prep(proj, conv_w, exp_a, dt_b, col_qkv, col_ba, seq, tm):
    n = proj.shape[0]
    w = 3 * GDN_HEADS * HEAD_W
    tiles_per_batch = seq // tm
    hb = tm // SUBLANES
    return pl.pallas_call(
        functools.partial(_gdn_prep_kernel, tiles_per_batch=tiles_per_batch),
        grid=(n // tm,),
        in_specs=[pl.BlockSpec((tm, w), lambda i: (i, col_qkv // w)),
                  pl.BlockSpec((SUBLANES, w), lambda i: (jnp.maximum(i * hb - 1, 0), col_qkv // w)),
                  pl.BlockSpec((tm, LANES), lambda i: (i, col_ba // LANES)),
                  pl.BlockSpec((GDN_CONV, w), lambda i: (0, 0)),
                  pl.BlockSpec((1, LANES), lambda i: (0, 0)),
                  pl.BlockSpec((1, LANES), lambda i: (0, 0))],
        out_specs=[pl.BlockSpec((tm, w), lambda i: (i, 0)),
                   pl.BlockSpec((tm, LANES), lambda i: (i, 0))],
        out_shape=[jax.ShapeDtypeStruct((n, w), F32), jax.ShapeDtypeStruct((n, LANES), F32)],
        compiler_params=_cparams("parallel"),
        name="gdn_prep",
    )(proj, proj, proj, conv_w, exp_a, dt_b)


def _gdn_intra_kernel(qkv_ref, gb_ref, u_ref, w_ref, qt_ref, ot_ref, kd_ref, eg_ref, *, tc):
    hw = GDN_HEADS * HEAD_W
    nc = tc // CHUNK
    gb = gb_ref[...]
    r_in_chunk = lax.broadcasted_iota(jnp.int32, gb.shape, 0) % CHUNK
    gc = gb
    sh = 1
    while sh < CHUNK:
        gc = gc + jnp.where(r_in_chunk >= sh, pltpu.roll(gc, sh, 0), 0.0)
        sh *= 2
    gc_t = gc.T

    row = lax.broadcasted_iota(jnp.int32, (CHUNK, CHUNK), 0)
    col = lax.broadcasted_iota(jnp.int32, (CHUNK, CHUNK), 1)
    incl = row >= col
    strict = row > col
    eye = (row == col).astype(F32)

    pairs = [(c, h) for c in range(nc) for h in range(GDN_HEADS)]
    st = {}
    for c, h in pairs:
        rs = slice(c * CHUNK, (c + 1) * CHUNK)
        hs = slice(h * HEAD_W, (h + 1) * HEAD_W)
        q = qkv_ref[rs, h * HEAD_W:(h + 1) * HEAD_W]
        k = qkv_ref[rs, hw + h * HEAD_W:hw + (h + 1) * HEAD_W]
        v = qkv_ref[rs, 2 * hw + h * HEAD_W:2 * hw + (h + 1) * HEAD_W]
        beta = gb[rs, h:h + 1]
        g_col = gc[rs, GDN_HEADS + h:GDN_HEADS + h + 1]
        g_row = gc_t[GDN_HEADS + h:GDN_HEADS + h + 1, rs]
        g_last = g_col[CHUNK - 1:CHUNK, :]
        k_beta = k * beta
        e_g = jnp.exp(g_col)
        kd_ref[rs, hs] = k * jnp.exp(g_last - g_col)
        eg_ref[c:c + 1, hs] = jnp.broadcast_to(jnp.exp(g_last), (1, HEAD_W))
        st[c, h] = dict(
            kb16=k_beta.astype(BF16), k16=k.astype(BF16), q16=q.astype(BF16),
            decay=jnp.exp(jnp.where(incl, g_col - g_row, NEG_BIG)),
            rhs=jnp.concatenate([v * beta, k_beta * e_g], axis=1).astype(BF16),
            qg=q * e_g)
    for p in pairs:
        s = st[p]
        s["lmat"] = jnp.where(strict, _mm_nt(s["kb16"], s["k16"]) * s["decay"], 0.0)
        s["attn"] = jnp.where(incl, _mm_nt(s["q16"], s["k16"]) * s["decay"], 0.0).astype(BF16)
        s["t"] = eye - jnp.where((row // 2) == (col // 2), s["lmat"], 0.0)
    b = 2
    while b < CHUNK:
        off = ((row // (2 * b)) == (col // (2 * b))) & ((row // b) % 2 == 1) & ((col // b) % 2 == 0)
        for p in pairs:
            s = st[p]
            s["ct"] = _mm(jnp.where(off, s["lmat"], 0.0), s["t"])
        for p in pairs:
            s = st[p]
            s["t"] = s["t"] - _mm(s["t"], s["ct"])
        b *= 2
    for p in pairs:
        s = st[p]
        s["uw"] = _mm(s["t"], s["rhs"])
    for c, h in pairs:
        s = st[c, h]
        rs = slice(c * CHUNK, (c + 1) * CHUNK)
        hs = slice(h * HEAD_W, (h + 1) * HEAD_W)
        a_uw = _mm(s["attn"], s["uw"])
        u_ref[rs, hs] = s["uw"][:, :HEAD_W]
        w_ref[rs, hs] = s["uw"][:, HEAD_W:].astype(BF16)
        ot_ref[rs, hs] = a_uw[:, :HEAD_W]
        qt_ref[rs, hs] = (s["qg"] - a_uw[:, HEAD_W:]).astype(BF16)


def _gdn_intra(qkv, gb, tc):
    n = qkv.shape[0]
    hw = GDN_HEADS * HEAD_W
    nc = tc // CHUNK
    blk = pl.BlockSpec((tc, hw), lambda i: (i, 0))
    return pl.pallas_call(
        functools.partial(_gdn_intra_kernel, tc=tc),
        grid=(n // tc,),
        in_specs=[pl.BlockSpec((tc, 3 * hw), lambda i: (i, 0)),
                  pl.BlockSpec((tc, LANES), lambda i: (i, 0))],
        out_specs=[blk, blk, blk, blk, blk, pl.BlockSpec((nc, hw), lambda i: (i, 0))],
        out_shape=[jax.ShapeDtypeStruct((n, hw), F32), jax.ShapeDtypeStruct((n, hw), BF16),
                   jax.ShapeDtypeStruct((n, hw), BF16), jax.ShapeDtypeStruct((n, hw), F32),
                   jax.ShapeDtypeStruct((n, hw), F32), jax.ShapeDtypeStruct((n // CHUNK, hw), F32)],
        compiler_params=_cparams("parallel"),
        name="gdn_intra",
    )(qkv, gb)


def _gdn_scan_kernel(u_ref, w_ref, qt_ref, ot_ref, kd_ref, eg_ref, z_ref, ow_ref, o_ref, state_ref, *, tc, batch):
    @pl.when(pl.program_id(0) == 0)
    def _():
        state_ref[...] = jnp.zeros(state_ref.shape, F32)

    chains = [(b, h) for b in range(batch) for h in range(GDN_HEADS)]
    for c in range(tc // CHUNK):
        rs = slice(c * CHUNK, (c + 1) * CHUNK)
        prod, v_new = {}, {}
        for b, h in chains:
            hs = slice(h * HEAD_W, (h + 1) * HEAD_W)
            lhs = jnp.concatenate([w_ref[b, rs, hs], qt_ref[b, rs, hs]], axis=0)
            prod[b, h] = jnp.dot(lhs, state_ref[b * GDN_HEADS + h].astype(BF16), preferred_element_type=F32)
        for b, h in chains:
            hs = slice(h * HEAD_W, (h + 1) * HEAD_W)
            v_new[b, h] = (u_ref[b, rs, hs] - prod[b, h][:CHUNK]).astype(BF16)
        for b, h in chains:
            hs = slice(h * HEAD_W, (h + 1) * HEAD_W)
            idx = b * GDN_HEADS + h
            k_t = kd_ref[b, rs, hs].T.astype(BF16)
            state_ref[idx] = (state_ref[idx] * eg_ref[b, c:c + 1, hs]
                              + jnp.dot(k_t, v_new[b, h], preferred_element_type=F32))
        for b, h in chains:
            hs = slice(h * HEAD_W, (h + 1) * HEAD_W)
            o = prod[b, h][CHUNK:] + ot_ref[b, rs, hs]
            o = o * lax.rsqrt(jnp.mean(o * o, axis=-1, keepdims=True) + NORM_EPS) * ow_ref[...]
            z = z_ref[b, rs, hs].astype(F32)
            o_ref[b, rs, hs] = (o * (z * jax.nn.sigmoid(z))).astype(o_ref.dtype)


def _gdn_scan(u, w, qt, ot, kd, eg, proj, onorm_w, col_z, batch, seq, tc):
    hw = GDN_HEADS * HEAD_W
    nc = tc // CHUNK
    r3 = lambda t: t.reshape(batch, t.shape[0] // batch, t.shape[1])
    blk = pl.BlockSpec((batch, tc, hw), lambda i: (0, i, 0))
    out = pl.pallas_call(
        functools.partial(_gdn_scan_kernel, tc=tc, batch=batch),
        grid=(seq // tc,),
        in_specs=[blk, blk, blk, blk, blk,
                  pl.BlockSpec((batch, nc, hw), lambda i: (0, i, 0)),
                  pl.BlockSpec((batch, tc, hw), lambda i: (0, i, col_z // hw)),
                  pl.BlockSpec((1, HEAD_W), lambda i: (0, 0))],
        out_specs=blk,
        out_shape=jax.ShapeDtypeStruct((batch, seq, hw), BF16),
        scratch_shapes=[pltpu.VMEM((batch * GDN_HEADS, GDN_K_DIM, GDN_V_DIM), F32)],
        compiler_params=_cparams("arbitrary"),
        name="gdn_scan",
    )(r3(u), r3(w), r3(qt), r3(ot), r3(kd), r3(eg), r3(proj), onorm_w)
    return out.reshape(batch * seq, hw)


def _merge_kernel(x_ref, oa_ref, ob_ref, oc_ref, gate_ref, wa_ref, wb_ref, wc_ref, wout_ref, mod_ref, o_ref):
    d = x_ref.shape[1]
    merged = None
    for idx, (o_r, w_r) in enumerate(((oa_ref, wa_ref), (ob_ref, wb_ref), (oc_ref, wc_ref))):
        y = jnp.dot(o_r[...], w_r[...], preferred_element_type=F32)
        g = jax.nn.sigmoid(gate_ref[:, idx * d:(idx + 1) * d].astype(F32))
        merged = g * y if merged is None else merged + g * y
    upd = jnp.dot(merged.astype(BF16), wout_ref[...], preferred_element_type=F32)
    o_ref[...] = x_ref[...] + mod_ref[2:3, :] * upd


def _merge(x2, oa, ob, oc, proj, wa, wb, wc, wout, mod, col_gate, seq, tm):
    n, d = x2.shape
    w = DIFF_HEADS * HEAD_W
    tiles_per_batch = seq // tm
    row = lambda i: (i, 0)
    const = lambda i: (0, 0)
    return pl.pallas_call(
        _merge_kernel,
        grid=(n // tm,),
        in_specs=[pl.BlockSpec((tm, d), row), pl.BlockSpec((tm, w), row), pl.BlockSpec((tm, w), row),
                  pl.BlockSpec((tm, w), row),
                  pl.BlockSpec((tm, N_BRANCHES * d), lambda i: (i, col_gate // (N_BRANCHES * d))),
                  pl.BlockSpec((w, d), const), pl.BlockSpec((w, d), const), pl.BlockSpec((w, d), const),
                  pl.BlockSpec((d, d), const),
                  pl.BlockSpec((None, 6, d), lambda i: (i // tiles_per_batch, 0, 0))],
        out_specs=pl.BlockSpec((tm, d), row),
        out_shape=jax.ShapeDtypeStruct((n, d), F32),
        compiler_params=_cparams("parallel"),
        name="merge",
    )(x2, oa, ob, oc, proj, wa, wb, wc, wout, mod)


def _ffn_kernel(x_ref, nw_ref, mod_ref, wg_ref, wu_ref, wd_ref, o_ref, h_ref, acc_ref):
    j = pl.program_id(1)

    @pl.when(j == 0)
    def _():
        h_ref[...] = _norm_mod(x_ref[...], nw_ref[...], mod_ref[...], 3, 4).astype(BF16)
        acc_ref[...] = jnp.zeros(acc_ref.shape, F32)

    h = h_ref[...]
    gate = jnp.dot(h, wg_ref[...], preferred_element_type=F32)
    up = jnp.dot(h, wu_ref[...], preferred_element_type=F32)
    act = (gate * jax.nn.sigmoid(gate) * up).astype(BF16)
    acc_ref[...] += jnp.dot(act, wd_ref[...], preferred_element_type=F32)

    @pl.when(j == pl.num_programs(1) - 1)
    def _():
        o_ref[...] = x_ref[...] + mod_ref[5:6, :] * acc_ref[...]


def _ffn(x2, nw, mod, w_gu, w_down, seq, tm, th):
    n, d = x2.shape
    hidden = w_down.shape[0]
    nh = hidden // th
    tiles_per_batch = seq // tm
    return pl.pallas_call(
        _ffn_kernel,
        grid=(n // tm, nh),
        in_specs=[pl.BlockSpec((tm, d), lambda i, j: (i, 0)),
                  pl.BlockSpec((1, d), lambda i, j: (0, 0)),
                  pl.BlockSpec((None, 6, d), lambda i, j: (i // tiles_per_batch, 0, 0)),
                  pl.BlockSpec((d, th), lambda i, j: (0, j)),
                  pl.BlockSpec((d, th), lambda i, j: (0, nh + j)),
                  pl.BlockSpec((th, d), lambda i, j: (j, 0))],
        out_specs=pl.BlockSpec((tm, d), lambda i, j: (i, 0)),
        out_shape=jax.ShapeDtypeStruct((n, d), F32),
        scratch_shapes=[pltpu.VMEM((tm, d), BF16), pltpu.VMEM((tm, d), F32)],
        compiler_params=_cparams("parallel", "arbitrary"),
        name="ffn",
    )(x2, nw, mod, w_gu, w_gu, w_down)


def _pad_cols(w, width):
    return jnp.pad(w, ((0, 0), (0, width - w.shape[1])))


def _in_proj_layout(d_model):
    a = DIFF_HEADS * HEAD_W
    g = GDN_HEADS * HEAD_W
    cols = {}
    off = 0
    for name, width in (("gate", N_BRANCHES * d_model), ("c_qkv", 3 * g), ("a_q", a), ("a_k", a), ("a_v", a),
                        ("c_z", g), ("c_ba", LANES), ("b_q", MLA_Q_LORA), ("b_kv", MLA_Q_LORA)):
        if off % width:
            off += width - off % width
        cols[name] = off
        off += width
    return cols, off


def _build_w_in(w_in, d_model, total):
    a = DIFF_HEADS * 2 * DIFF_QK_DIM
    g = GDN_HEADS * GDN_K_DIM
    widths = (a, a, DIFF_HEADS * DIFF_V_DIM, MLA_Q_LORA, MLA_KV_LORA + MLA_ROPE, g, g, GDN_HEADS * GDN_V_DIM,
              GDN_HEADS * GDN_V_DIM, GDN_HEADS, GDN_HEADS, N_BRANCHES * d_model)
    offs = np.concatenate([[0], np.cumsum(widths)])
    part = [w_in[:, int(offs[i]):int(offs[i + 1])] for i in range(len(widths))]
    cols, _ = _in_proj_layout(d_model)
    out = jnp.zeros((w_in.shape[0], total), w_in.dtype)
    place = (("a_q", part[0]), ("a_k", part[1]), ("a_v", part[2]), ("b_q", part[3]), ("b_kv", part[4]),
             ("c_qkv", jnp.concatenate(part[5:8], axis=1)), ("c_z", part[8]), ("gate", part[11]),
             ("c_ba", jnp.concatenate(part[9:11], axis=1)))
    for name, p in place:
        out = lax.dynamic_update_slice(out, p, (0, cols[name]))
    return out.astype(BF16)


def _head_pad(w, heads, per_head, lane_off=0):
    r = w.shape[0]
    w = w.reshape(r, heads, per_head)
    w = jnp.pad(w, ((0, 0), (0, 0), (lane_off, HEAD_W - per_head - lane_off)))
    return w.reshape(r, heads * HEAD_W)


def kernel(x, c, positions, ada_w, ada_b, norm1_w, w_in, diff_qnorm_w, diff_knorm_w, diff_lambda, diff_subln_w, w_o_diff, mla_qa_norm_w, mla_q_up, mla_kva_norm_w, mla_kv_up, mla_qnorm_w, mla_knorm_w, w_o_mla, gdn_conv_w, gdn_a_log, gdn_dt_bias, gdn_onorm_w, w_o_gdn, w_out, norm2_w, ffn_w_gu, ffn_w_down):
    batch, seq, d = x.shape
    depth = ada_w.shape[0]
    n = batch * seq
    qk_dim = MLA_NOPE + MLA_ROPE

    tm = min(1024, seq)
    tq = min(512, seq)
    tc = min(512, seq)

    c_pad = jnp.pad(c, ((0, SUBLANES - batch), (0, 0)))
    mod_all = _ada_mod(c_pad, ada_w, ada_b)[:, :batch].reshape(depth, batch, 6, d)

    cols, total = _in_proj_layout(d)
    tn = 512
    total = -(-total // tn) * tn

    pos2 = positions.reshape(n, 1)
    half = MLA_ROPE // 2
    inv_freq = ROPE_THETA ** (-jnp.arange(half, dtype=F32) / half)
    invf = jnp.zeros((1, HEAD_W), F32).at[0, MLA_NOPE:qk_dim].set(jnp.concatenate([inv_freq, inv_freq]))

    x2 = x.reshape(n, d)
    for l in range(depth):
        lambda_init = 0.8 - 0.6 * math.exp(-0.3 * l)
        mod = mod_all[l]
        proj = _in_proj(x2, norm1_w[l][None], mod, _build_w_in(w_in[l], d, total), seq, tm, tn)

        qw2 = jnp.tile(diff_qnorm_w[l], 2)[None]
        kw2 = jnp.tile(diff_knorm_w[l], 2)[None]
        a_q, a_k = _diff_prep(proj, qw2, kw2, cols["a_q"], cols["a_k"], tm)
        o_a = _attention(a_q, a_k, proj, (diff_lambda[l], diff_subln_w[l][None]), heads=DIFF_HEADS, batch=batch,
                         seq=seq, tq=tq, tk=tq, rg=128, n_maps=2, v_col=cols["a_v"], lambda_init=lambda_init)

        kv_up = mla_kv_up[l].reshape(MLA_KV_LORA, MLA_HEADS, MLA_NOPE + MLA_V_DIM)
        wk_nope = _head_pad(kv_up[:, :, :MLA_NOPE].reshape(MLA_KV_LORA, -1), MLA_HEADS, MLA_NOPE)
        place_rope = _head_pad(jnp.tile(jnp.eye(MLA_ROPE, dtype=F32), (1, MLA_HEADS)), MLA_HEADS, MLA_ROPE, MLA_NOPE)
        pad_rows = MLA_Q_LORA - MLA_KV_LORA - MLA_ROPE
        wk = jnp.concatenate([wk_nope, place_rope, jnp.zeros((pad_rows, MLA_HEADS * HEAD_W), F32)], axis=0)
        wv = jnp.pad(kv_up[:, :, MLA_NOPE:].reshape(MLA_KV_LORA, -1), ((0, MLA_Q_LORA - MLA_KV_LORA), (0, 0)))
        kva_w = jnp.pad(mla_kva_norm_w[l], (0, MLA_Q_LORA - MLA_KV_LORA))[None]
        b_q, b_k, b_v = _mla_prep(
            proj, pos2, mla_qa_norm_w[l][None], _head_pad(mla_q_up[l], MLA_HEADS, qk_dim).astype(BF16),
            kva_w, wk.astype(BF16), wv.astype(BF16),
            jnp.pad(mla_qnorm_w[l], (0, HEAD_W - qk_dim))[None], jnp.pad(mla_knorm_w[l], (0, HEAD_W - qk_dim))[None],
            invf, cols["b_q"], cols["b_kv"], tm)
        o_b = _attention(b_q, b_k, b_v, (), heads=MLA_HEADS, batch=batch, seq=seq, tq=min(2 * tq, seq), tk=tq,
                         rg=128, n_maps=1)

        exp_a = jnp.zeros((1, LANES), F32).at[0, GDN_HEADS:2 * GDN_HEADS].set(gdn_a_log[l])
        dt_b = jnp.zeros((1, LANES), F32).at[0, GDN_HEADS:2 * GDN_HEADS].set(gdn_dt_bias[l])
        c_qkv, c_gb = _gdn_prep(proj, gdn_conv_w[l], exp_a, dt_b, cols["c_qkv"], cols["c_ba"], seq, tm)
        g_u, g_w, g_qt, g_ot, g_kd, g_eg = _gdn_intra(c_qkv, c_gb, tc)
        o_c = _gdn_scan(g_u, g_w, g_qt, g_ot, g_kd, g_eg, proj, gdn_onorm_w[l][None], cols["c_z"], batch, seq, tc)

        x2 = _merge(x2, o_a, o_b, o_c, proj, w_o_diff[l].astype(BF16), w_o_mla[l].astype(BF16),
                    w_o_gdn[l].astype(BF16), w_out[l].astype(BF16), mod, cols["gate"], seq, min(512, seq))
        x2 = _ffn(x2, norm2_w[l][None], mod, ffn_w_gu[l].astype(BF16), ffn_w_down[l].astype(BF16), seq, tm, 256)
    return x2.reshape(batch, seq, d)
```

```python
import functools
import math

import jax
import jax.numpy as jnp
import numpy as np
from jax import lax
from jax.experimental import pallas as pl
from jax.experimental.pallas import tpu as pltpu

F32 = jnp.float32
BF16 = jnp.bfloat16

CHUNK = 64
NORM_EPS = 1e-6
ROPE_THETA = 10000.0
DIFF_HEADS = 4
DIFF_QK_DIM = 64
DIFF_V_DIM = 128
MLA_HEADS = 4
MLA_Q_LORA = 384
MLA_KV_LORA = 256
MLA_NOPE = 64
MLA_ROPE = 32
MLA_V_DIM = 128
GDN_HEADS = 4
GDN_K_DIM = 128
GDN_V_DIM = 128
GDN_CONV = 4
N_BRANCHES = 3

LANES = 128
SUBLANES = 8
HEAD_W = 128
LOG2E = 1.4426950408889634
NEG_BIG = -1e30
VMEM_LIMIT = 56 * 1024 * 1024


def _cparams(*sem):
    return pltpu.CompilerParams(dimension_semantics=sem, vmem_limit_bytes=VMEM_LIMIT)


def _mm(a, b):
    return jnp.dot(a.astype(BF16), b.astype(BF16), preferred_element_type=F32)


def _mm_nt(a, b):
    return lax.dot_general(a.astype(BF16), b.astype(BF16), (((1,), (1,)), ((), ())),
                           preferred_element_type=F32)


def _mod_kernel(c_ref, w_ref, b_ref, o_ref):
    o_ref[...] = jnp.dot(c_ref[...], w_ref[...], preferred_element_type=F32) + b_ref[...]


def _ada_mod(c_pad, ada_w, ada_b):
    depth, d, n6 = ada_w.shape
    tn = 1536
    return pl.pallas_call(
        _mod_kernel,
        grid=(depth, n6 // tn),
        in_specs=[pl.BlockSpec((SUBLANES, d), lambda l, j: (0, 0)),
                  pl.BlockSpec((None, d, tn), lambda l, j: (l, 0, j)),
                  pl.BlockSpec((None, 1, tn), lambda l, j: (l, 0, j))],
        out_specs=pl.BlockSpec((None, SUBLANES, tn), lambda l, j: (l, 0, j)),
        out_shape=jax.ShapeDtypeStruct((depth, SUBLANES, n6), F32),
        compiler_params=_cparams("parallel", "parallel"),
        name="ada_mod",
    )(c_pad, ada_w, ada_b.reshape(depth, 1, n6))


def _norm_mod(x, nw, mod, shift_row, scale_row):
    r = lax.rsqrt(jnp.mean(x * x, axis=-1, keepdims=True) + NORM_EPS)
    h = x * r * nw
    return h * (1.0 + mod[scale_row:scale_row + 1, :]) + mod[shift_row:shift_row + 1, :]


def _in_proj_kernel(x_ref, nw_ref, mod_ref, w_ref, o_ref, h_ref):
    @pl.when(pl.program_id(1) == 0)
    def _():
        h_ref[...] = _norm_mod(x_ref[...], nw_ref[...], mod_ref[...], 0, 1).astype(BF16)

    o_ref[...] = jnp.dot(h_ref[...], w_ref[...], preferred_element_type=F32).astype(o_ref.dtype)


def _in_proj(x2, nw, mod, w, seq, tm, tn):
    n, d = x2.shape
    cols = w.shape[1]
    tiles_per_batch = seq // tm
    return pl.pallas_call(
        _in_proj_kernel,
        grid=(n // tm, cols // tn),
        in_specs=[pl.BlockSpec((tm, d), lambda i, j: (i, 0)),
                  pl.BlockSpec((1, d), lambda i, j: (0, 0)),
                  pl.BlockSpec((None, 6, d), lambda i, j: (i // tiles_per_batch, 0, 0)),
                  pl.BlockSpec((d, tn), lambda i, j: (0, j))],
        out_specs=pl.BlockSpec((tm, tn), lambda i, j: (i, j)),
        out_shape=jax.ShapeDtypeStruct((n, cols), BF16),
        scratch_shapes=[pltpu.VMEM((tm, d), BF16)],
        compiler_params=_cparams("parallel", "arbitrary"),
        name="in_proj",
    )(x2, nw, mod, w)


def _half_rms(t, w):
    lane = lax.broadcasted_iota(jnp.int32, t.shape, 1)
    lo = lane < DIFF_QK_DIM
    sq = t * t
    s_lo = jnp.sum(jnp.where(lo, sq, 0.0), axis=-1, keepdims=True)
    s_hi = jnp.sum(jnp.where(lo, 0.0, sq), axis=-1, keepdims=True)
    r = lax.rsqrt(jnp.where(lo, s_lo, s_hi) * (1.0 / DIFF_QK_DIM) + NORM_EPS)
    return t * r * w


def _diff_prep_kernel(q_ref, k_ref, qw_ref, kw_ref, qo_ref, ko_ref):
    qscale = (DIFF_QK_DIM ** -0.5) * LOG2E
    for h in range(DIFF_HEADS):
        sl = slice(h * HEAD_W, (h + 1) * HEAD_W)
        qo_ref[:, sl] = (_half_rms(q_ref[:, sl].astype(F32), qw_ref[...]) * qscale).astype(BF16)
        ko_ref[:, sl] = _half_rms(k_ref[:, sl].astype(F32), kw_ref[...]).astype(BF16)


def _diff_prep(proj, qw2, kw2, col_q, col_k, tm):
    n = proj.shape[0]
    w = DIFF_HEADS * HEAD_W
    return pl.pallas_call(
        _diff_prep_kernel,
        grid=(n // tm,),
        in_specs=[pl.BlockSpec((tm, w), lambda i: (i, col_q // w)),
                  pl.BlockSpec((tm, w), lambda i: (i, col_k // w)),
                  pl.BlockSpec((1, HEAD_W), lambda i: (0, 0)),
                  pl.BlockSpec((1, HEAD_W), lambda i: (0, 0))],
        out_specs=[pl.BlockSpec((tm, w), lambda i: (i, 0)),
                   pl.BlockSpec((tm, w), lambda i: (i, 0))],
        out_shape=[jax.ShapeDtypeStruct((n, w), BF16), jax.ShapeDtypeStruct((n, w), BF16)],
        compiler_params=_cparams("parallel"),
        name="diff_prep",
    )(proj, proj, qw2, kw2)


def _rope_tile(t, cos_t, sin_lo, sin_hi):
    half = MLA_ROPE // 2
    return (t * cos_t + pltpu.roll(t, HEAD_W - half, 1) * sin_lo + pltpu.roll(t, half, 1) * sin_hi)


def _mla_prep_kernel(q_ref, kv_ref, pos_ref, qa_ref, qup_ref, kva_ref, wk_ref, wv_ref,
                     qn_ref, kn_ref, invf_ref, qo_ref, ko_ref, vo_ref):
    qk_dim = MLA_NOPE + MLA_ROPE
    qscale = (qk_dim ** -0.5) * LOG2E
    half = MLA_ROPE // 2

    xq = q_ref[...].astype(F32)
    rq = lax.rsqrt(jnp.mean(xq * xq, axis=-1, keepdims=True) + NORM_EPS)
    q = _mm(xq * rq * qa_ref[...], qup_ref[...])

    xkv = kv_ref[...].astype(F32)
    lane_kv = lax.broadcasted_iota(jnp.int32, xkv.shape, 1)
    is_lat = lane_kv < MLA_KV_LORA
    ssq = jnp.sum(jnp.where(is_lat, xkv * xkv, 0.0), axis=-1, keepdims=True)
    rkv = lax.rsqrt(ssq * (1.0 / MLA_KV_LORA) + NORM_EPS)
    lhs = jnp.where(is_lat, xkv * rkv * kva_ref[...], xkv).astype(BF16)
    k = jnp.dot(lhs, wk_ref[...], preferred_element_type=F32)
    vo_ref[...] = jnp.dot(lhs, wv_ref[...], preferred_element_type=F32).astype(BF16)

    ang = pos_ref[...].astype(F32) * invf_ref[...]
    lane = lax.broadcasted_iota(jnp.int32, ang.shape, 1)
    in_lo = (lane >= MLA_NOPE) & (lane < MLA_NOPE + half)
    in_hi = (lane >= MLA_NOPE + half) & (lane < qk_dim)
    cos_a, sin_a = jnp.cos(ang), jnp.sin(ang)
    cos_t = jnp.where(in_lo | in_hi, cos_a, 1.0)
    sin_lo = jnp.where(in_lo, -sin_a, 0.0)
    sin_hi = jnp.where(in_hi, sin_a, 0.0)

    for h in range(MLA_HEADS):
        sl = slice(h * HEAD_W, (h + 1) * HEAD_W)
        qh = q[:, sl]
        qh = qh * lax.rsqrt(jnp.sum(qh * qh, axis=-1, keepdims=True) * (1.0 / qk_dim) + NORM_EPS)
        qh = _rope_tile(qh * qn_ref[...], cos_t, sin_lo, sin_hi)
        qo_ref[:, sl] = (qh * qscale).astype(BF16)
        kh = k[:, sl]
        kh = kh * lax.rsqrt(jnp.sum(kh * kh, axis=-1, keepdims=True) * (1.0 / qk_dim) + NORM_EPS)
        kh = _rope_tile(kh * kn_ref[...], cos_t, sin_lo, sin_hi)
        ko_ref[:, sl] = kh.astype(BF16)


def _mla_prep(proj, pos2, qa_w, qup, kva_w, wk, wv, qn_w, kn_w, invf, col_q, col_kv, tm):
    n = proj.shape[0]
    w = MLA_HEADS * HEAD_W
    lat = MLA_Q_LORA
    const = lambda i: (0, 0)
    out = jax.ShapeDtypeStruct((n, w), BF16)
    return pl.pallas_call(
        _mla_prep_kernel,
        grid=(n // tm,),
        in_specs=[pl.BlockSpec((tm, lat), lambda i: (i, col_q // lat)),
                  pl.BlockSpec((tm, lat), lambda i: (i, col_kv // lat)),
                  pl.BlockSpec((tm, 1), lambda i: (i, 0)),
                  pl.BlockSpec((1, lat), const),
                  pl.BlockSpec((lat, w), const),
                  pl.BlockSpec((1, lat), const),
                  pl.BlockSpec((lat, w), const),
                  pl.BlockSpec((lat, w), const),
                  pl.BlockSpec((1, HEAD_W), const),
                  pl.BlockSpec((1, HEAD_W), const),
                  pl.BlockSpec((1, HEAD_W), const)],
        out_specs=[pl.BlockSpec((tm, w), lambda i: (i, 0))] * 3,
        out_shape=[out, out, out],
        compiler_params=_cparams("parallel"),
        name="mla_prep",
    )(proj, proj, pos2, qa_w, qup, kva_w, wk, wv, qn_w, kn_w, invf)


def _attn_kernel(*refs, tq, tk, rg, n_maps, lambda_init):
    if n_maps == 2:
        q_ref, k_ref, v_ref, lam_ref, sw_ref, o_ref, qq_ref, s_ref, mx_ref, m_ref, l_ref, acc_ref = refs
    else:
        q_ref, k_ref, v_ref, o_ref, s_ref, mx_ref, m_ref, l_ref, acc_ref = refs
        qq_ref = q_ref
    i = pl.program_id(2)
    rows = n_maps * tq
    if n_maps == 2:
        q = q_ref[...]
        lane = lax.broadcasted_iota(jnp.int32, q.shape, 1)
        zero = jnp.zeros_like(q)
        qq_ref[:tq, :] = jnp.where(lane < DIFF_QK_DIM, q, zero)
        qq_ref[tq:, :] = jnp.where(lane >= DIFF_QK_DIM, q, zero)

    m_ref[...] = jnp.full(m_ref.shape, NEG_BIG, F32)
    l_ref[...] = jnp.zeros(l_ref.shape, F32)
    acc_ref[...] = jnp.zeros(acc_ref.shape, F32)

    n_groups = rows // rg
    n_full = (i * tq) // tk
    n_diag = max(tq // tk, 1)

    def produce(j, slot, g, masked):
        rsl = slice(g * rg, (g + 1) * rg)
        start = pl.multiple_of(j * tk, tk)
        s = lax.dot_general(qq_ref[rsl, :], k_ref[pl.ds(start, tk), :],
                            (((1,), (1,)), ((), ())), preferred_element_type=F32)
        if masked:
            r_idx = lax.broadcasted_iota(jnp.int32, (rg, tk), 0)
            c_idx = lax.broadcasted_iota(jnp.int32, (rg, tk), 1)
            q_chunk = (i * tq + (g * rg) % tq + r_idx) // CHUNK
            k_chunk = (j * tk + c_idx) // CHUNK
            s = jnp.where(k_chunk <= q_chunk, s, NEG_BIG)
        s_ref[slot, rsl, :] = s
        mx_ref[slot, rsl, :] = jnp.broadcast_to(jnp.max(s, axis=-1, keepdims=True), (rg, LANES))

    def consume(j, slot, g, v):
        rsl = slice(g * rg, (g + 1) * rg)
        m_prev = m_ref[rsl, :]
        m_new = jnp.maximum(m_prev, mx_ref[slot, rsl, :])
        alpha = jnp.exp2(m_prev - m_new)
        p = jnp.exp2(s_ref[slot, rsl, :] - jnp.tile(m_new, (1, tk // LANES)))
        p_sum = p[:, :LANES]
        for t in range(1, tk // LANES):
            p_sum = p_sum + p[:, t * LANES:(t + 1) * LANES]
        l_ref[rsl, :] = alpha * l_ref[rsl, :] + p_sum
        m_ref[rsl, :] = m_new
        acc_ref[rsl, :] = alpha * acc_ref[rsl, :] + jnp.dot(p.astype(BF16), v, preferred_element_type=F32)

    def step(j, slot, next_masked, cur_diag=None, next_diag=None):
        start = pl.multiple_of(j * tk, tk)
        v = v_ref[pl.ds(start, tk), :]
        for g in range(n_groups):
            if next_masked is not None and live(g, next_diag):
                produce(j + 1, 1 - slot, g, next_masked)
            if live(g, cur_diag):
                consume(j, slot, g, v)

    def live(g, diag):
        return diag is None or (g * rg) % tq + rg > diag * tk

    for g in range(n_groups):
        produce(0, 0, g, True)

    def body(jj, carry):
        step(2 * jj, 0, False)
        step(2 * jj + 1, 1, False)
        return carry

    n_pairs = jnp.maximum((n_full - 1) // 2, 0)
    lax.fori_loop(0, n_pairs, body, 0)
    left = n_full - 2 * n_pairs

    def tail(first_slot):
        for d in range(n_diag):
            more = d < n_diag - 1
            step(n_full + d, (first_slot + d) % 2, True if more else None, cur_diag=d,
                 next_diag=d + 1 if more else None)

    @pl.when(left == 0)
    def _():
        tail(0)

    if tq % (2 * tk):
        @pl.when(left == 1)
        def _():
            step(n_full - 1, 0, True)
            tail(1)

    @pl.when(left == 2)
    def _():
        step(n_full - 2, 0, False)
        step(n_full - 1, 1, True)
        tail(0)

    o = acc_ref[...] / jnp.sum(l_ref[...], axis=-1, keepdims=True)
    if n_maps == 2:
        lp = lam_ref[...]
        lam = (jnp.exp(jnp.sum(lp[0:1] * lp[1:2], axis=-1, keepdims=True))
               - jnp.exp(jnp.sum(lp[2:3] * lp[3:4], axis=-1, keepdims=True)) + lambda_init)
        o = o[:tq] - lam * o[tq:]
        r = lax.rsqrt(jnp.mean(o * o, axis=-1, keepdims=True) + NORM_EPS)
        o = o * r * sw_ref[...] * (1.0 - lambda_init)
    o_ref[...] = o.astype(o_ref.dtype)


def _attention(q, k, v, extra, *, heads, batch, seq, tq, tk, rg, n_maps, v_col=0, lambda_init=0.0):
    q3, k3, v3 = (t.reshape(batch, seq, t.shape[1]) for t in (q, k, v))
    rows = n_maps * tq
    rg = min(rg, tq)
    vb = v_col // HEAD_W
    scratch = [pltpu.VMEM((2, rows, tk), F32), pltpu.VMEM((2, rows, LANES), F32), pltpu.VMEM((rows, LANES), F32),
               pltpu.VMEM((rows, LANES), F32), pltpu.VMEM((rows, HEAD_W), F32)]
    if n_maps == 2:
        scratch.insert(0, pltpu.VMEM((rows, HEAD_W), BF16))
    in_specs = [pl.BlockSpec((None, tq, HEAD_W), lambda b, h, i: (b, i, h)),
                pl.BlockSpec((None, seq, HEAD_W), lambda b, h, i: (b, 0, h)),
                pl.BlockSpec((None, seq, HEAD_W), lambda b, h, i: (b, 0, vb + h))]
    for e in extra:
        in_specs.append(pl.BlockSpec(e.shape, lambda b, h, i: (0, 0)))
    out = pl.pallas_call(
        functools.partial(_attn_kernel, tq=tq, tk=tk, rg=rg, n_maps=n_maps, lambda_init=lambda_init),
        grid=(batch, heads, seq // tq),
        in_specs=in_specs,
        out_specs=pl.BlockSpec((None, tq, HEAD_W), lambda b, h, i: (b, i, h)),
        out_shape=jax.ShapeDtypeStruct((batch, seq, heads * HEAD_W), BF16),
        scratch_shapes=scratch,
        compiler_params=_cparams("parallel", "parallel", "arbitrary"),
        name="diff_attn" if n_maps == 2 else "mla_attn",
    )(q3, k3, v3, *extra)
    return out.reshape(batch * seq, heads * HEAD_W)


def _gdn_prep_kernel(x_ref, halo_ref, ba_ref, cw_ref, ea_ref, dtb_ref, qkv_ref, gb_ref, *, tiles_per_batch):
    i = pl.program_id(0)
    x = x_ref[...].astype(F32)
    halo = halo_ref[...].astype(F32)
    halo = jnp.where(i % tiles_per_batch == 0, 0.0, halo)
    cw = cw_ref[...]
    row8 = lax.broadcasted_iota(jnp.int32, halo.shape, 0)
    y = x * cw[GDN_CONV - 1:GDN_CONV]
    for s in range(1, GDN_CONV):
        xs = pltpu.roll(x, s, 0)
        hs = pltpu.roll(halo, s, 0)
        first = jnp.where(row8 < s, hs, xs[:SUBLANES])
        xs = jnp.concatenate([first, xs[SUBLANES:]], axis=0)
        y = y + xs * cw[GDN_CONV - 1 - s:GDN_CONV - s]
    y = y * jax.nn.sigmoid(y)
    nqk = 2 * GDN_HEADS
    for h in range(3 * GDN_HEADS):
        sl = slice(h * HEAD_W, (h + 1) * HEAD_W)
        t = y[:, sl]
        if h < nqk:
            t = t * lax.rsqrt(jnp.sum(t * t, axis=-1, keepdims=True) + NORM_EPS)
            if h < GDN_HEADS:
                t = t * (GDN_K_DIM ** -0.5)
        qkv_ref[:, sl] = t
    ba = ba_ref[...].astype(F32)
    lane = lax.broadcasted_iota(jnp.int32, ba.shape, 1)
    sp_in = ba + dtb_ref[...]
    softplus = jnp.maximum(sp_in, 0.0) + jnp.log1p(jnp.exp(-jnp.abs(sp_in)))
    gb_ref[...] = jnp.where(lane < GDN_HEADS, jax.nn.sigmoid(ba), -jnp.exp(ea_ref[...]) * softplus)


def _gdn_prep(proj, conv_w, exp_a, dt_b, col_qkv, col_ba, seq, tm):
    n = proj.shape[0]
    w = 3 * GDN_HEADS * HEAD_W
    tiles_per_batch = seq // tm
    hb = tm // SUBLANES
    return pl.pallas_call(
        functools.partial(_gdn_prep_kernel, tiles_per_batch=tiles_per_batch),
        grid=(n // tm,),
        in_specs=[pl.BlockSpec((tm, w), lambda i: (i, col_qkv // w)),
                  pl.BlockSpec((SUBLANES, w), lambda i: (jnp.maximum(i * hb - 1, 0), col_qkv // w)),
                  pl.BlockSpec((tm, LANES), lambda i: (i, col_ba // LANES)),
                  pl.BlockSpec((GDN_CONV, w), lambda i: (0, 0)),
                  pl.BlockSpec((1, LANES), lambda i: (0, 0)),
                  pl.BlockSpec((1, LANES), lambda i: (0, 0))],
        out_specs=[pl.BlockSpec((tm, w), lambda i: (i, 0)),
                   pl.BlockSpec((tm, LANES), lambda i: (i, 0))],
        out_shape=[jax.ShapeDtypeStruct((n, w), F32), jax.ShapeDtypeStruct((n, LANES), F32)],
        compiler_params=_cparams("parallel"),
        name="gdn_prep",
    )(proj, proj, proj, conv_w, exp_a, dt_b)


def _gdn_intra_kernel(qkv_ref, gb_ref, u_ref, w_ref, qt_ref, ot_ref, kd_ref, eg_ref, *, tc):
    hw = GDN_HEADS * HEAD_W
    nc = tc // CHUNK
    gb = gb_ref[...]
    r_in_chunk = lax.broadcasted_iota(jnp.int32, gb.shape, 0) % CHUNK
    gc = gb
    sh = 1
    while sh < CHUNK:
        gc = gc + jnp.where(r_in_chunk >= sh, pltpu.roll(gc, sh, 0), 0.0)
        sh *= 2
    gc_t = gc.T

    row = lax.broadcasted_iota(jnp.int32, (CHUNK, CHUNK), 0)
    col = lax.broadcasted_iota(jnp.int32, (CHUNK, CHUNK), 1)
    incl = row >= col
    strict = row > col
    eye = (row == col).astype(F32)

    pairs = [(c, h) for c in range(nc) for h in range(GDN_HEADS)]
    st = {}
    for c, h in pairs:
        rs = slice(c * CHUNK, (c + 1) * CHUNK)
        hs = slice(h * HEAD_W, (h + 1) * HEAD_W)
        q = qkv_ref[rs, h * HEAD_W:(h + 1) * HEAD_W]
        k = qkv_ref[rs, hw + h * HEAD_W:hw + (h + 1) * HEAD_W]
        v = qkv_ref[rs, 2 * hw + h * HEAD_W:2 * hw + (h + 1) * HEAD_W]
        beta = gb[rs, h:h + 1]
        g_col = gc[rs, GDN_HEADS + h:GDN_HEADS + h + 1]
        g_row = gc_t[GDN_HEADS + h:GDN_HEADS + h + 1, rs]
        g_last = g_col[CHUNK - 1:CHUNK, :]
        k_beta = k * beta
        e_g = jnp.exp(g_col)
        kd_ref[rs, hs] = k * jnp.exp(g_last - g_col)
        eg_ref[c:c + 1, hs] = jnp.broadcast_to(jnp.exp(g_last), (1, HEAD_W))
        st[c, h] = dict(
            kb16=k_beta.astype(BF16), k16=k.astype(BF16), q16=q.astype(BF16),
            decay=jnp.exp(jnp.where(incl, g_col - g_row, NEG_BIG)),
            rhs=jnp.concatenate([v * beta, k_beta * e_g], axis=1).astype(BF16),
            qg=q * e_g)
    for p in pairs:
        s = st[p]
        s["lmat"] = jnp.where(strict, _mm_nt(s["kb16"], s["k16"]) * s["decay"], 0.0)
        s["attn"] = jnp.where(incl, _mm_nt(s["q16"], s["k16"]) * s["decay"], 0.0).astype(BF16)
        s["t"] = eye - jnp.where((row // 2) == (col // 2), s["lmat"], 0.0)
    b = 2
    while b < CHUNK:
        off = ((row // (2 * b)) == (col // (2 * b))) & ((row // b) % 2 == 1) & ((col // b) % 2 == 0)
        for p in pairs:
            s = st[p]
            s["ct"] = _mm(jnp.where(off, s["lmat"], 0.0), s["t"])
        for p in pairs:
            s = st[p]
            s["t"] = s["t"] - _mm(s["t"], s["ct"])
        b *= 2
    for p in pairs:
        s = st[p]
        s["uw"] = _mm(s["t"], s["rhs"])
    for c, h in pairs:
        s = st[c, h]
        rs = slice(c * CHUNK, (c + 1) * CHUNK)
        hs = slice(h * HEAD_W, (h + 1) * HEAD_W)
        a_uw = _mm(s["attn"], s["uw"])
        u_ref[rs, hs] = s["uw"][:, :HEAD_W]
        w_ref[rs, hs] = s["uw"][:, HEAD_W:].astype(BF16)
        ot_ref[rs, hs] = a_uw[:, :HEAD_W]
        qt_ref[rs, hs] = (s["qg"] - a_uw[:, HEAD_W:]).astype(BF16)


def _gdn_intra(qkv, gb, tc):
    n = qkv.shape[0]
    hw = GDN_HEADS * HEAD_W
    nc = tc // CHUNK
    blk = pl.BlockSpec((tc, hw), lambda i: (i, 0))
    return pl.pallas_call(
        functools.partial(_gdn_intra_kernel, tc=tc),
        grid=(n // tc,),
        in_specs=[pl.BlockSpec((tc, 3 * hw), lambda i: (i, 0)),
                  pl.BlockSpec((tc, LANES), lambda i: (i, 0))],
        out_specs=[blk, blk, blk, blk, blk, pl.BlockSpec((nc, hw), lambda i: (i, 0))],
        out_shape=[jax.ShapeDtypeStruct((n, hw), F32), jax.ShapeDtypeStruct((n, hw), BF16),
                   jax.ShapeDtypeStruct((n, hw), BF16), jax.ShapeDtypeStruct((n, hw), F32),
                   jax.ShapeDtypeStruct((n, hw), F32), jax.ShapeDtypeStruct((n // CHUNK, hw), F32)],
        compiler_params=_cparams("parallel"),
        name="gdn_intra",
    )(qkv, gb)


def _gdn_scan_kernel(u_ref, w_ref, qt_ref, ot_ref, kd_ref, eg_ref, z_ref, ow_ref, o_ref, state_ref, *, tc, batch):
    @pl.when(pl.program_id(0) == 0)
    def _():
        state_ref[...] = jnp.zeros(state_ref.shape, F32)

    chains = [(b, h) for b in range(batch) for h in range(GDN_HEADS)]
    for c in range(tc // CHUNK):
        rs = slice(c * CHUNK, (c + 1) * CHUNK)
        prod, v_new = {}, {}
        for b, h in chains:
            hs = slice(h * HEAD_W, (h + 1) * HEAD_W)
            lhs = jnp.concatenate([w_ref[b, rs, hs], qt_ref[b, rs, hs]], axis=0)
            prod[b, h] = jnp.dot(lhs, state_ref[b * GDN_HEADS + h].astype(BF16), preferred_element_type=F32)
        for b, h in chains:
            hs = slice(h * HEAD_W, (h + 1) * HEAD_W)
            v_new[b, h] = (u_ref[b, rs, hs] - prod[b, h][:CHUNK]).astype(BF16)
        for b, h in chains:
            hs = slice(h * HEAD_W, (h + 1) * HEAD_W)
            idx = b * GDN_HEADS + h
            k_t = kd_ref[b, rs, hs].T.astype(BF16)
            state_ref[idx] = (state_ref[idx] * eg_ref[b, c:c + 1, hs]
                              + jnp.dot(k_t, v_new[b, h], preferred_element_type=F32))
        for b, h in chains:
            hs = slice(h * HEAD_W, (h + 1) * HEAD_W)
            o = prod[b, h][CHUNK:] + ot_ref[b, rs, hs]
            o = o * lax.rsqrt(jnp.mean(o * o, axis=-1, keepdims=True) + NORM_EPS) * ow_ref[...]
            z = z_ref[b, rs, hs].astype(F32)
            o_ref[b, rs, hs] = (o * (z * jax.nn.sigmoid(z))).astype(o_ref.dtype)


def _gdn_scan(u, w, qt, ot, kd, eg, proj, onorm_w, col_z, batch, seq, tc):
    hw = GDN_HEADS * HEAD_W
    nc = tc // CHUNK
    r3 = lambda t: t.reshape(batch, t.shape[0] // batch, t.shape[1])
    blk = pl.BlockSpec((batch, tc, hw), lambda i: (0, i, 0))
    out = pl.pallas_call(
        functools.partial(_gdn_scan_kernel, tc=tc, batch=batch),
        grid=(seq // tc,),
        in_specs=[blk, blk, blk, blk, blk,
                  pl.BlockSpec((batch, nc, hw), lambda i: (0, i, 0)),
                  pl.BlockSpec((batch, tc, hw), lambda i: (0, i, col_z // hw)),
                  pl.BlockSpec((1, HEAD_W), lambda i: (0, 0))],
        out_specs=blk,
        out_shape=jax.ShapeDtypeStruct((batch, seq, hw), BF16),
        scratch_shapes=[pltpu.VMEM((batch * GDN_HEADS, GDN_K_DIM, GDN_V_DIM), F32)],
        compiler_params=_cparams("arbitrary"),
        name="gdn_scan",
    )(r3(u), r3(w), r3(qt), r3(ot), r3(kd), r3(eg), r3(proj), onorm_w)
    return out.reshape(batch * seq, hw)


def _merge_kernel(x_ref, oa_ref, ob_ref, oc_ref, gate_ref, wa_ref, wb_ref, wc_ref, wout_ref, mod_ref, o_ref):
    d = x_ref.shape[1]
    merged = None
    for idx, (o_r, w_r) in enumerate(((oa_ref, wa_ref), (ob_ref, wb_ref), (oc_ref, wc_ref))):
        y = jnp.dot(o_r[...], w_r[...], preferred_element_type=F32)
        g = jax.nn.sigmoid(gate_ref[:, idx * d:(idx + 1) * d].astype(F32))
        merged = g * y if merged is None else merged + g * y
    upd = jnp.dot(merged.astype(BF16), wout_ref[...], preferred_element_type=F32)
    o_ref[...] = x_ref[...] + mod_ref[2:3, :] * upd


def _merge(x2, oa, ob, oc, proj, wa, wb, wc, wout, mod, col_gate, seq, tm):
    n, d = x2.shape
    w = DIFF_HEADS * HEAD_W
    tiles_per_batch = seq // tm
    row = lambda i: (i, 0)
    const = lambda i: (0, 0)
    return pl.pallas_call(
        _merge_kernel,
        grid=(n // tm,),
        in_specs=[pl.BlockSpec((tm, d), row), pl.BlockSpec((tm, w), row), pl.BlockSpec((tm, w), row),
                  pl.BlockSpec((tm, w), row),
                  pl.BlockSpec((tm, N_BRANCHES * d), lambda i: (i, col_gate // (N_BRANCHES * d))),
                  pl.BlockSpec((w, d), const), pl.BlockSpec((w, d), const), pl.BlockSpec((w, d), const),
                  pl.BlockSpec((d, d), const),
                  pl.BlockSpec((None, 6, d), lambda i: (i // tiles_per_batch, 0, 0))],
        out_specs=pl.BlockSpec((tm, d), row),
        out_shape=jax.ShapeDtypeStruct((n, d), F32),
        compiler_params=_cparams("parallel"),
        name="merge",
    )(x2, oa, ob, oc, proj, wa, wb, wc, wout, mod)


def _ffn_kernel(x_ref, nw_ref, mod_ref, wg_ref, wu_ref, wd_ref, o_ref, h_ref, acc_ref):
    j = pl.program_id(1)

    @pl.when(j == 0)
    def _():
        h_ref[...] = _norm_mod(x_ref[...], nw_ref[...], mod_ref[...], 3, 4).astype(BF16)
        acc_ref[...] = jnp.zeros(acc_ref.shape, F32)

    h = h_ref[...]
    gate = jnp.dot(h, wg_ref[...], preferred_element_type=F32)
    up = jnp.dot(h, wu_ref[...], preferred_element_type=F32)
    act = (gate * jax.nn.sigmoid(gate) * up).astype(BF16)
    acc_ref[...] += jnp.dot(act, wd_ref[...], preferred_element_type=F32)

    @pl.when(j == pl.num_programs(1) - 1)
    def _():
        o_ref[...] = x_ref[...] + mod_ref[5:6, :] * acc_ref[...]


def _ffn(x2, nw, mod, w_gu, w_down, seq, tm, th):
    n, d = x2.shape
    hidden = w_down.shape[0]
    nh = hidden // th
    tiles_per_batch = seq // tm
    return pl.pallas_call(
        _ffn_kernel,
        grid=(n // tm, nh),
        in_specs=[pl.BlockSpec((tm, d), lambda i, j: (i, 0)),
                  pl.BlockSpec((1, d), lambda i, j: (0, 0)),
                  pl.BlockSpec((None, 6, d), lambda i, j: (i // tiles_per_batch, 0, 0)),
                  pl.BlockSpec((d, th), lambda i, j: (0, j)),
                  pl.BlockSpec((d, th), lambda i, j: (0, nh + j)),
                  pl.BlockSpec((th, d), lambda i, j: (j, 0))],
        out_specs=pl.BlockSpec((tm, d), lambda i, j: (i, 0)),
        out_shape=jax.ShapeDtypeStruct((n, d), F32),
        scratch_shapes=[pltpu.VMEM((tm, d), BF16), pltpu.VMEM((tm, d), F32)],
        compiler_params=_cparams("parallel", "arbitrary"),
        name="ffn",
    )(x2, nw, mod, w_gu, w_gu, w_down)


def _pad_cols(w, width):
    return jnp.pad(w, ((0, 0), (0, width - w.shape[1])))


def _in_proj_layout(d_model):
    a = DIFF_HEADS * HEAD_W
    g = GDN_HEADS * HEAD_W
    cols = {}
    off = 0
    for name, width in (("gate", N_BRANCHES * d_model), ("c_qkv", 3 * g), ("a_q", a), ("a_k", a), ("a_v", a),
                        ("c_z", g), ("c_ba", LANES), ("b_q", MLA_Q_LORA), ("b_kv", MLA_Q_LORA)):
        if off % width:
            off += width - off % width
        cols[name] = off
        off += width
    return cols, off


def _build_w_in(w_in, d_model, total):
    a = DIFF_HEADS * 2 * DIFF_QK_DIM
    g = GDN_HEADS * GDN_K_DIM
    widths = (a, a, DIFF_HEADS * DIFF_V_DIM, MLA_Q_LORA, MLA_KV_LORA + MLA_ROPE, g, g, GDN_HEADS * GDN_V_DIM,
              GDN_HEADS * GDN_V_DIM, GDN_HEADS, GDN_HEADS, N_BRANCHES * d_model)
    offs = np.concatenate([[0], np.cumsum(widths)])
    part = [w_in[:, int(offs[i]):int(offs[i + 1])] for i in range(len(widths))]
    cols, _ = _in_proj_layout(d_model)
    out = jnp.zeros((w_in.shape[0], total), w_in.dtype)
    place = (("a_q", part[0]), ("a_k", part[1]), ("a_v", part[2]), ("b_q", part[3]), ("b_kv", part[4]),
             ("c_qkv", jnp.concatenate(part[5:8], axis=1)), ("c_z", part[8]), ("gate", part[11]),
             ("c_ba", jnp.concatenate(part[9:11], axis=1)))
    for name, p in place:
        out = lax.dynamic_update_slice(out, p, (0, cols[name]))
    return out.astype(BF16)


def _head_pad(w, heads, per_head, lane_off=0):
    r = w.shape[0]
    w = w.reshape(r, heads, per_head)
    w = jnp.pad(w, ((0, 0), (0, 0), (lane_off, HEAD_W - per_head - lane_off)))
    return w.reshape(r, heads * HEAD_W)


def kernel(x, c, positions, ada_w, ada_b, norm1_w, w_in, diff_qnorm_w, diff_knorm_w, diff_lambda, diff_subln_w, w_o_diff, mla_qa_norm_w, mla_q_up, mla_kva_norm_w, mla_kv_up, mla_qnorm_w, mla_knorm_w, w_o_mla, gdn_conv_w, gdn_a_log, gdn_dt_bias, gdn_onorm_w, w_o_gdn, w_out, norm2_w, ffn_w_gu, ffn_w_down):
    batch, seq, d = x.shape
    depth = ada_w.shape[0]
    n = batch * seq
    qk_dim = MLA_NOPE + MLA_ROPE

    tm = min(1024, seq)
    tq = min(512, seq)
    tc = min(512, seq)

    c_pad = jnp.pad(c, ((0, SUBLANES - batch), (0, 0)))
    mod_all = _ada_mod(c_pad, ada_w, ada_b)[:, :batch].reshape(depth, batch, 6, d)

    cols, total = _in_proj_layout(d)
    tn = 512
    total = -(-total // tn) * tn

    pos2 = positions.reshape(n, 1)
    half = MLA_ROPE // 2
    inv_freq = ROPE_THETA ** (-jnp.arange(half, dtype=F32) / half)
    invf = jnp.zeros((1, HEAD_W), F32).at[0, MLA_NOPE:qk_dim].set(jnp.concatenate([inv_freq, inv_freq]))

    x2 = x.reshape(n, d)
    for l in range(depth):
        lambda_init = 0.8 - 0.6 * math.exp(-0.3 * l)
        mod = mod_all[l]
        proj = _in_proj(x2, norm1_w[l][None], mod, _build_w_in(w_in[l], d, total), seq, tm, tn)

        qw2 = jnp.tile(diff_qnorm_w[l], 2)[None]
        kw2 = jnp.tile(diff_knorm_w[l], 2)[None]
        a_q, a_k = _diff_prep(proj, qw2, kw2, cols["a_q"], cols["a_k"], tm)
        o_a = _attention(a_q, a_k, proj, (diff_lambda[l], diff_subln_w[l][None]), heads=DIFF_HEADS, batch=batch,
                         seq=seq, tq=min(2 * tq, seq), tk=tq, rg=128, n_maps=2, v_col=cols["a_v"],
                         lambda_init=lambda_init)

        kv_up = mla_kv_up[l].reshape(MLA_KV_LORA, MLA_HEADS, MLA_NOPE + MLA_V_DIM)
        wk_nope = _head_pad(kv_up[:, :, :MLA_NOPE].reshape(MLA_KV_LORA, -1), MLA_HEADS, MLA_NOPE)
        place_rope = _head_pad(jnp.tile(jnp.eye(MLA_ROPE, dtype=F32), (1, MLA_HEADS)), MLA_HEADS, MLA_ROPE, MLA_NOPE)
        pad_rows = MLA_Q_LORA - MLA_KV_LORA - MLA_ROPE
        wk = jnp.concatenate([wk_nope, place_rope, jnp.zeros((pad_rows, MLA_HEADS * HEAD_W), F32)], axis=0)
        wv = jnp.pad(kv_up[:, :, MLA_NOPE:].reshape(MLA_KV_LORA, -1), ((0, MLA_Q_LORA - MLA_KV_LORA), (0, 0)))
        kva_w = jnp.pad(mla_kva_norm_w[l], (0, MLA_Q_LORA - MLA_KV_LORA))[None]
        b_q, b_k, b_v = _mla_prep(
            proj, pos2, mla_qa_norm_w[l][None], _head_pad(mla_q_up[l], MLA_HEADS, qk_dim).astype(BF16),
            kva_w, wk.astype(BF16), wv.astype(BF16),
            jnp.pad(mla_qnorm_w[l], (0, HEAD_W - qk_dim))[None], jnp.pad(mla_knorm_w[l], (0, HEAD_W - qk_dim))[None],
            invf, cols["b_q"], cols["b_kv"], tm)
        o_b = _attention(b_q, b_k, b_v, (), heads=MLA_HEADS, batch=batch, seq=seq, tq=min(4 * tq, seq), tk=tq,
                         rg=128, n_maps=1)

        exp_a = jnp.zeros((1, LANES), F32).at[0, GDN_HEADS:2 * GDN_HEADS].set(gdn_a_log[l])
        dt_b = jnp.zeros((1, LANES), F32).at[0, GDN_HEADS:2 * GDN_HEADS].set(gdn_dt_bias[l])
        c_qkv, c_gb = _gdn_prep(proj, gdn_conv_w[l], exp_a, dt_b, cols["c_qkv"], cols["c_ba"], seq, tm)
        g_u, g_w, g_qt, g_ot, g_kd, g_eg = _gdn_intra(c_qkv, c_gb, tc)
        o_c = _gdn_scan(g_u, g_w, g_qt, g_ot, g_kd, g_eg, proj, gdn_onorm_w[l][None], cols["c_z"], batch, seq, tc)

        x2 = _merge(x2, o_a, o_b, o_c, proj, w_o_diff[l].astype(BF16), w_o_mla[l].astype(BF16),
                    w_o_gdn[l].astype(BF16), w_out[l].astype(BF16), mod, cols["gate"], seq, min(512, seq))
        x2 = _ffn(x2, norm2_w[l][None], mod, ffn_w_gu[l].astype(BF16), ffn_w_down[l].astype(BF16), seq, tm, 256)
    return x2.reshape(batch, seq, d)
```

```python
import functools
import math

import jax
import jax.numpy as jnp
import numpy as np
from jax import lax
from jax.experimental import pallas as pl
from jax.experimental.pallas import tpu as pltpu

F32 = jnp.float32
BF16 = jnp.bfloat16

CHUNK = 64
NORM_EPS = 1e-6
ROPE_THETA = 10000.0
DIFF_HEADS = 4
DIFF_QK_DIM = 64
DIFF_V_DIM = 128
MLA_HEADS = 4
MLA_Q_LORA = 384
MLA_KV_LORA = 256
MLA_NOPE = 64
MLA_ROPE = 32
MLA_V_DIM = 128
GDN_HEADS = 4
GDN_K_DIM = 128
GDN_V_DIM = 128
GDN_CONV = 4
N_BRANCHES = 3

LANES = 128
SUBLANES = 8
HEAD_W = 128
LOG2E = 1.4426950408889634
NEG_BIG = -1e30
VMEM_LIMIT = 56 * 1024 * 1024


def _cparams(*sem):
    return pltpu.CompilerParams(dimension_semantics=sem, vmem_limit_bytes=VMEM_LIMIT)


def _mm(a, b):
    return jnp.dot(a.astype(BF16), b.astype(BF16), preferred_element_type=F32)


def _mm_nt(a, b):
    return lax.dot_general(a.astype(BF16), b.astype(BF16), (((1,), (1,)), ((), ())),
                           preferred_element_type=F32)


def _mod_kernel(c_ref, w_ref, b_ref, o_ref):
    o_ref[...] = jnp.dot(c_ref[...], w_ref[...], preferred_element_type=F32) + b_ref[...]


def _ada_mod(c_pad, ada_w, ada_b):
    depth, d, n6 = ada_w.shape
    tn = 1536
    return pl.pallas_call(
        _mod_kernel,
        grid=(depth, n6 // tn),
        in_specs=[pl.BlockSpec((SUBLANES, d), lambda l, j: (0, 0)),
                  pl.BlockSpec((None, d, tn), lambda l, j: (l, 0, j)),
                  pl.BlockSpec((None, 1, tn), lambda l, j: (l, 0, j))],
        out_specs=pl.BlockSpec((None, SUBLANES, tn), lambda l, j: (l, 0, j)),
        out_shape=jax.ShapeDtypeStruct((depth, SUBLANES, n6), F32),
        compiler_params=_cparams("parallel", "parallel"),
        name="ada_mod",
    )(c_pad, ada_w, ada_b.reshape(depth, 1, n6))


def _norm_mod(x, nw, mod, shift_row, scale_row):
    r = lax.rsqrt(jnp.mean(x * x, axis=-1, keepdims=True) + NORM_EPS)
    h = x * r * nw
    return h * (1.0 + mod[scale_row:scale_row + 1, :]) + mod[shift_row:shift_row + 1, :]


def _in_proj_kernel(x_ref, nw_ref, mod_ref, w_ref, o_ref, h_ref):
    @pl.when(pl.program_id(1) == 0)
    def _():
        h_ref[...] = _norm_mod(x_ref[...], nw_ref[...], mod_ref[...], 0, 1).astype(BF16)

    o_ref[...] = jnp.dot(h_ref[...], w_ref[...], preferred_element_type=F32).astype(o_ref.dtype)


def _in_proj(x2, nw, mod, w, seq, tm, tn):
    n, d = x2.shape
    cols = w.shape[1]
    tiles_per_batch = seq // tm
    return pl.pallas_call(
        _in_proj_kernel,
        grid=(n // tm, cols // tn),
        in_specs=[pl.BlockSpec((tm, d), lambda i, j: (i, 0)),
                  pl.BlockSpec((1, d), lambda i, j: (0, 0)),
                  pl.BlockSpec((None, 6, d), lambda i, j: (i // tiles_per_batch, 0, 0)),
                  pl.BlockSpec((d, tn), lambda i, j: (0, j))],
        out_specs=pl.BlockSpec((tm, tn), lambda i, j: (i, j)),
        out_shape=jax.ShapeDtypeStruct((n, cols), BF16),
        scratch_shapes=[pltpu.VMEM((tm, d), BF16)],
        compiler_params=_cparams("parallel", "arbitrary"),
        name="in_proj",
    )(x2, nw, mod, w)


def _half_rms(t, w):
    lane = lax.broadcasted_iota(jnp.int32, t.shape, 1)
    lo = lane < DIFF_QK_DIM
    sq = t * t
    s_lo = jnp.sum(jnp.where(lo, sq, 0.0), axis=-1, keepdims=True)
    s_hi = jnp.sum(jnp.where(lo, 0.0, sq), axis=-1, keepdims=True)
    r = lax.rsqrt(jnp.where(lo, s_lo, s_hi) * (1.0 / DIFF_QK_DIM) + NORM_EPS)
    return t * r * w


def _diff_prep_kernel(q_ref, k_ref, qw_ref, kw_ref, qo_ref, ko_ref):
    qscale = (DIFF_QK_DIM ** -0.5) * LOG2E
    for h in range(DIFF_HEADS):
        sl = slice(h * HEAD_W, (h + 1) * HEAD_W)
        qo_ref[:, sl] = (_half_rms(q_ref[:, sl].astype(F32), qw_ref[...]) * qscale).astype(BF16)
        ko_ref[:, sl] = _half_rms(k_ref[:, sl].astype(F32), kw_ref[...]).astype(BF16)


def _diff_prep(proj, qw2, kw2, col_q, col_k, tm):
    n = proj.shape[0]
    w = DIFF_HEADS * HEAD_W
    return pl.pallas_call(
        _diff_prep_kernel,
        grid=(n // tm,),
        in_specs=[pl.BlockSpec((tm, w), lambda i: (i, col_q // w)),
                  pl.BlockSpec((tm, w), lambda i: (i, col_k // w)),
                  pl.BlockSpec((1, HEAD_W), lambda i: (0, 0)),
                  pl.BlockSpec((1, HEAD_W), lambda i: (0, 0))],
        out_specs=[pl.BlockSpec((tm, w), lambda i: (i, 0)),
                   pl.BlockSpec((tm, w), lambda i: (i, 0))],
        out_shape=[jax.ShapeDtypeStruct((n, w), BF16), jax.ShapeDtypeStruct((n, w), BF16)],
        compiler_params=_cparams("parallel"),
        name="diff_prep",
    )(proj, proj, qw2, kw2)


def _rope_tile(t, cos_t, sin_lo, sin_hi):
    half = MLA_ROPE // 2
    return (t * cos_t + pltpu.roll(t, HEAD_W - half, 1) * sin_lo + pltpu.roll(t, half, 1) * sin_hi)


def _mla_prep_kernel(q_ref, kv_ref, pos_ref, qa_ref, qup_ref, kva_ref, wk_ref, wv_ref,
                     qn_ref, kn_ref, invf_ref, qo_ref, ko_ref, vo_ref):
    qk_dim = MLA_NOPE + MLA_ROPE
    qscale = (qk_dim ** -0.5) * LOG2E
    half = MLA_ROPE // 2

    xq = q_ref[...].astype(F32)
    rq = lax.rsqrt(jnp.mean(xq * xq, axis=-1, keepdims=True) + NORM_EPS)
    q = _mm(xq * rq * qa_ref[...], qup_ref[...])

    xkv = kv_ref[...].astype(F32)
    lane_kv = lax.broadcasted_iota(jnp.int32, xkv.shape, 1)
    is_lat = lane_kv < MLA_KV_LORA
    ssq = jnp.sum(jnp.where(is_lat, xkv * xkv, 0.0), axis=-1, keepdims=True)
    rkv = lax.rsqrt(ssq * (1.0 / MLA_KV_LORA) + NORM_EPS)
    lhs = jnp.where(is_lat, xkv * rkv * kva_ref[...], xkv).astype(BF16)
    k = jnp.dot(lhs, wk_ref[...], preferred_element_type=F32)
    vo_ref[...] = jnp.dot(lhs, wv_ref[...], preferred_element_type=F32).astype(BF16)

    ang = pos_ref[...].astype(F32) * invf_ref[...]
    lane = lax.broadcasted_iota(jnp.int32, ang.shape, 1)
    in_lo = (lane >= MLA_NOPE) & (lane < MLA_NOPE + half)
    in_hi = (lane >= MLA_NOPE + half) & (lane < qk_dim)
    cos_a, sin_a = jnp.cos(ang), jnp.sin(ang)
    cos_t = jnp.where(in_lo | in_hi, cos_a, 1.0)
    sin_lo = jnp.where(in_lo, -sin_a, 0.0)
    sin_hi = jnp.where(in_hi, sin_a, 0.0)

    for h in range(MLA_HEADS):
        sl = slice(h * HEAD_W, (h + 1) * HEAD_W)
        qh = q[:, sl]
        qh = qh * lax.rsqrt(jnp.sum(qh * qh, axis=-1, keepdims=True) * (1.0 / qk_dim) + NORM_EPS)
        qh = _rope_tile(qh * qn_ref[...], cos_t, sin_lo, sin_hi)
        qo_ref[:, sl] = (qh * qscale).astype(BF16)
        kh = k[:, sl]
        kh = kh * lax.rsqrt(jnp.sum(kh * kh, axis=-1, keepdims=True) * (1.0 / qk_dim) + NORM_EPS)
        kh = _rope_tile(kh * kn_ref[...], cos_t, sin_lo, sin_hi)
        ko_ref[:, sl] = kh.astype(BF16)


def _mla_prep(proj, pos2, qa_w, qup, kva_w, wk, wv, qn_w, kn_w, invf, col_q, col_kv, tm):
    n = proj.shape[0]
    w = MLA_HEADS * HEAD_W
    lat = MLA_Q_LORA
    const = lambda i: (0, 0)
    out = jax.ShapeDtypeStruct((n, w), BF16)
    return pl.pallas_call(
        _mla_prep_kernel,
        grid=(n // tm,),
        in_specs=[pl.BlockSpec((tm, lat), lambda i: (i, col_q // lat)),
                  pl.BlockSpec((tm, lat), lambda i: (i, col_kv // lat)),
                  pl.BlockSpec((tm, 1), lambda i: (i, 0)),
                  pl.BlockSpec((1, lat), const),
                  pl.BlockSpec((lat, w), const),
                  pl.BlockSpec((1, lat), const),
                  pl.BlockSpec((lat, w), const),
                  pl.BlockSpec((lat, w), const),
                  pl.BlockSpec((1, HEAD_W), const),
                  pl.BlockSpec((1, HEAD_W), const),
                  pl.BlockSpec((1, HEAD_W), const)],
        out_specs=[pl.BlockSpec((tm, w), lambda i: (i, 0))] * 3,
        out_shape=[out, out, out],
        compiler_params=_cparams("parallel"),
        name="mla_prep",
    )(proj, proj, pos2, qa_w, qup, kva_w, wk, wv, qn_w, kn_w, invf)


def _attn_kernel(*refs, tq, tk, rg, n_maps, lambda_init):
    if n_maps == 2:
        q_ref, k_ref, v_ref, lam_ref, sw_ref, o_ref, qq_ref, s_ref, p_ref, mx_ref, m_ref, l_ref, acc_ref = refs
    else:
        q_ref, k_ref, v_ref, o_ref, s_ref, p_ref, mx_ref, m_ref, l_ref, acc_ref = refs
        qq_ref = q_ref
    i = pl.program_id(2)
    rows = n_maps * tq
    if n_maps == 2:
        q = q_ref[...]
        lane = lax.broadcasted_iota(jnp.int32, q.shape, 1)
        zero = jnp.zeros_like(q)
        qq_ref[:tq, :] = jnp.where(lane < DIFF_QK_DIM, q, zero)
        qq_ref[tq:, :] = jnp.where(lane >= DIFF_QK_DIM, q, zero)

    m_ref[...] = jnp.full(m_ref.shape, NEG_BIG, F32)
    l_ref[...] = jnp.zeros(l_ref.shape, F32)
    acc_ref[...] = jnp.zeros(acc_ref.shape, F32)

    n_groups = rows // rg
    n_full = (i * tq) // tk
    n_diag = max(tq // tk, 1)

    def produce(j, slot, g, masked):
        rsl = slice(g * rg, (g + 1) * rg)
        start = pl.multiple_of(j * tk, tk)
        s = lax.dot_general(qq_ref[rsl, :], k_ref[pl.ds(start, tk), :],
                            (((1,), (1,)), ((), ())), preferred_element_type=F32)
        if masked:
            r_idx = lax.broadcasted_iota(jnp.int32, (rg, tk), 0)
            c_idx = lax.broadcasted_iota(jnp.int32, (rg, tk), 1)
            q_chunk = (i * tq + (g * rg) % tq + r_idx) // CHUNK
            k_chunk = (j * tk + c_idx) // CHUNK
            s = jnp.where(k_chunk <= q_chunk, s, NEG_BIG)
        s_ref[slot, rsl, :] = s
        mx_ref[slot, rsl, :] = jnp.broadcast_to(jnp.max(s, axis=-1, keepdims=True), (rg, LANES))

    def consume(j, slot, g, v):
        rsl = slice(g * rg, (g + 1) * rg)
        m_prev = m_ref[rsl, :]
        m_new = jnp.maximum(m_prev, mx_ref[slot, rsl, :])
        alpha = jnp.exp2(m_prev - m_new)
        p_sum = None
        for t in range(tk // LANES):
            csl = slice(t * LANES, (t + 1) * LANES)
            p = jnp.exp2(s_ref[slot, rsl, csl] - m_new)
            p_ref[rsl, csl] = p.astype(BF16)
            p_sum = p if p_sum is None else p_sum + p
        l_ref[rsl, :] = alpha * l_ref[rsl, :] + p_sum
        m_ref[rsl, :] = m_new
        acc_ref[rsl, :] = alpha * acc_ref[rsl, :] + jnp.dot(p_ref[rsl, :], v, preferred_element_type=F32)

    def step(j, slot, next_masked, cur_diag=None, next_diag=None):
        start = pl.multiple_of(j * tk, tk)
        v = v_ref[pl.ds(start, tk), :]
        for g in range(n_groups):
            if next_masked is not None and live(g, next_diag):
                produce(j + 1, 1 - slot, g, next_masked)
            if live(g, cur_diag):
                consume(j, slot, g, v)

    def live(g, diag):
        return diag is None or (g * rg) % tq + rg > diag * tk

    for g in range(n_groups):
        produce(0, 0, g, True)

    def body(jj, carry):
        step(2 * jj, 0, False)
        step(2 * jj + 1, 1, False)
        return carry

    n_pairs = jnp.maximum((n_full - 1) // 2, 0)
    lax.fori_loop(0, n_pairs, body, 0)
    left = n_full - 2 * n_pairs

    def tail(first_slot):
        for d in range(n_diag):
            more = d < n_diag - 1
            step(n_full + d, (first_slot + d) % 2, True if more else None, cur_diag=d,
                 next_diag=d + 1 if more else None)

    @pl.when(left == 0)
    def _():
        tail(0)

    if tq % (2 * tk):
        @pl.when(left == 1)
        def _():
            step(n_full - 1, 0, True)
            tail(1)

    @pl.when(left == 2)
    def _():
        step(n_full - 2, 0, False)
        step(n_full - 1, 1, True)
        tail(0)

    o = acc_ref[...] / jnp.sum(l_ref[...], axis=-1, keepdims=True)
    if n_maps == 2:
        lp = lam_ref[...]
        lam = (jnp.exp(jnp.sum(lp[0:1] * lp[1:2], axis=-1, keepdims=True))
               - jnp.exp(jnp.sum(lp[2:3] * lp[3:4], axis=-1, keepdims=True)) + lambda_init)
        o = o[:tq] - lam * o[tq:]
        r = lax.rsqrt(jnp.mean(o * o, axis=-1, keepdims=True) + NORM_EPS)
        o = o * r * sw_ref[...] * (1.0 - lambda_init)
    o_ref[...] = o.astype(o_ref.dtype)


def _attention(q, k, v, extra, *, heads, batch, seq, tq, tk, rg, n_maps, v_col=0, lambda_init=0.0):
    q3, k3, v3 = (t.reshape(batch, seq, t.shape[1]) for t in (q, k, v))
    rows = n_maps * tq
    rg = min(rg, tq)
    vb = v_col // HEAD_W
    scratch = [pltpu.VMEM((2, rows, tk), F32), pltpu.VMEM((rows, tk), BF16), pltpu.VMEM((2, rows, LANES), F32), pltpu.VMEM((rows, LANES), F32),
               pltpu.VMEM((rows, LANES), F32), pltpu.VMEM((rows, HEAD_W), F32)]
    if n_maps == 2:
        scratch.insert(0, pltpu.VMEM((rows, HEAD_W), BF16))
    in_specs = [pl.BlockSpec((None, tq, HEAD_W), lambda b, h, i: (b, i, h)),
                pl.BlockSpec((None, seq, HEAD_W), lambda b, h, i: (b, 0, h)),
                pl.BlockSpec((None, seq, HEAD_W), lambda b, h, i: (b, 0, vb + h))]
    for e in extra:
        in_specs.append(pl.BlockSpec(e.shape, lambda b, h, i: (0, 0)))
    out = pl.pallas_call(
        functools.partial(_attn_kernel, tq=tq, tk=tk, rg=rg, n_maps=n_maps, lambda_init=lambda_init),
        grid=(batch, heads, seq // tq),
        in_specs=in_specs,
        out_specs=pl.BlockSpec((None, tq, HEAD_W), lambda b, h, i: (b, i, h)),
        out_shape=jax.ShapeDtypeStruct((batch, seq, heads * HEAD_W), BF16),
        scratch_shapes=scratch,
        compiler_params=_cparams("parallel", "parallel", "arbitrary"),
        name="diff_attn" if n_maps == 2 else "mla_attn",
    )(q3, k3, v3, *extra)
    return out.reshape(batch * seq, heads * HEAD_W)


def _gdn_prep_kernel(x_ref, halo_ref, ba_ref, cw_ref, ea_ref, dtb_ref, qkv_ref, gb_ref, *, tiles_per_batch):
    i = pl.program_id(0)
    x = x_ref[...].astype(F32)
    halo = halo_ref[...].astype(F32)
    halo = jnp.where(i % tiles_per_batch == 0, 0.0, halo)
    cw = cw_ref[...]
    row8 = lax.broadcasted_iota(jnp.int32, halo.shape, 0)
    y = x * cw[GDN_CONV - 1:GDN_CONV]
    for s in range(1, GDN_CONV):
        xs = pltpu.roll(x, s, 0)
        hs = pltpu.roll(halo, s, 0)
        first = jnp.where(row8 < s, hs, xs[:SUBLANES])
        xs = jnp.concatenate([first, xs[SUBLANES:]], axis=0)
        y = y + xs * cw[GDN_CONV - 1 - s:GDN_CONV - s]
    y = y * jax.nn.sigmoid(y)
    nqk = 2 * GDN_HEADS
    for h in range(3 * GDN_HEADS):
        sl = slice(h * HEAD_W, (h + 1) * HEAD_W)
        t = y[:, sl]
        if h < nqk:
            t = t * lax.rsqrt(jnp.sum(t * t, axis=-1, keepdims=True) + NORM_EPS)
            if h < GDN_HEADS:
                t = t * (GDN_K_DIM ** -0.5)
        qkv_ref[:, sl] = t
    ba = ba_ref[...].astype(F32)
    lane = lax.broadcasted_iota(jnp.int32, ba.shape, 1)
    sp_in = ba + dtb_ref[...]
    softplus = jnp.maximum(sp_in, 0.0) + jnp.log1p(jnp.exp(-jnp.abs(sp_in)))
    gb_ref[...] = jnp.where(lane < GDN_HEADS, jax.nn.sigmoid(ba), -jnp.exp(ea_ref[...]) * softplus)


def _gdn_prep(proj, conv_w, exp_a, dt_b, col_qkv, col_ba, seq, tm):
    n = proj.shape[0]
    w = 3 * GDN_HEADS * HEAD_W
    tiles_per_batch = seq // tm
    hb = tm // SUBLANES
    return pl.pallas_call(
        functools.partial(_gdn_prep_kernel, tiles_per_batch=tiles_per_batch),
        grid=(n // tm,),
        in_specs=[pl.BlockSpec((tm, w), lambda i: (i, col_qkv // w)),
                  pl.BlockSpec((SUBLANES, w), lambda i: (jnp.maximum(i * hb - 1, 0), col_qkv // w)),
                  pl.BlockSpec((tm, LANES), lambda i: (i, col_ba // LANES)),
                  pl.BlockSpec((GDN_CONV, w), lambda i: (0, 0)),
                  pl.BlockSpec((1, LANES), lambda i: (0, 0)),
                  pl.BlockSpec((1, LANES), lambda i: (0, 0))],
        out_specs=[pl.BlockSpec((tm, w), lambda i: (i, 0)),
                   pl.BlockSpec((tm, LANES), lambda i: (i, 0))],
        out_shape=[jax.ShapeDtypeStruct((n, w), F32), jax.ShapeDtypeStruct((n, LANES), F32)],
        compiler_params=_cparams("parallel"),
        name="gdn_prep",
    )(proj, proj, proj, conv_w, exp_a, dt_b)


def _gdn_intra_kernel(qkv_ref, gb_ref, u_ref, w_ref, qt_ref, ot_ref, kd_ref, eg_ref, *, tc):
    hw = GDN_HEADS * HEAD_W
    nc = tc // CHUNK
    gb = gb_ref[...]
    r_in_chunk = lax.broadcasted_iota(jnp.int32, gb.shape, 0) % CHUNK
    gc = gb
    sh = 1
    while sh < CHUNK:
        gc = gc + jnp.where(r_in_chunk >= sh, pltpu.roll(gc, sh, 0), 0.0)
        sh *= 2
    gc_t = gc.T

    row = lax.broadcasted_iota(jnp.int32, (CHUNK, CHUNK), 0)
    col = lax.broadcasted_iota(jnp.int32, (CHUNK, CHUNK), 1)
    incl = row >= col
    strict = row > col
    eye = (row == col).astype(F32)

    pairs = [(c, h) for c in range(nc) for h in range(GDN_HEADS)]
    st = {}
    for c, h in pairs:
        rs = slice(c * CHUNK, (c + 1) * CHUNK)
        hs = slice(h * HEAD_W, (h + 1) * HEAD_W)
        q = qkv_ref[rs, h * HEAD_W:(h + 1) * HEAD_W]
        k = qkv_ref[rs, hw + h * HEAD_W:hw + (h + 1) * HEAD_W]
        v = qkv_ref[rs, 2 * hw + h * HEAD_W:2 * hw + (h + 1) * HEAD_W]
        beta = gb[rs, h:h + 1]
        g_col = gc[rs, GDN_HEADS + h:GDN_HEADS + h + 1]
        g_row = gc_t[GDN_HEADS + h:GDN_HEADS + h + 1, rs]
        g_last = g_col[CHUNK - 1:CHUNK, :]
        k_beta = k * beta
        e_g = jnp.exp(g_col)
        kd_ref[rs, hs] = k * jnp.exp(g_last - g_col)
        eg_ref[c:c + 1, hs] = jnp.broadcast_to(jnp.exp(g_last), (1, HEAD_W))
        st[c, h] = dict(
            kb16=k_beta.astype(BF16), k16=k.astype(BF16), q16=q.astype(BF16),
            decay=jnp.exp(jnp.where(incl, g_col - g_row, NEG_BIG)),
            rhs=jnp.concatenate([v * beta, k_beta * e_g], axis=1).astype(BF16),
            qg=q * e_g)
    for p in pairs:
        s = st[p]
        s["lmat"] = jnp.where(strict, _mm_nt(s["kb16"], s["k16"]) * s["decay"], 0.0)
        s["attn"] = jnp.where(incl, _mm_nt(s["q16"], s["k16"]) * s["decay"], 0.0).astype(BF16)
        s["t"] = eye - jnp.where((row // 2) == (col // 2), s["lmat"], 0.0)
    b = 2
    while b < CHUNK:
        off = ((row // (2 * b)) == (col // (2 * b))) & ((row // b) % 2 == 1) & ((col // b) % 2 == 0)
        for p in pairs:
            s = st[p]
            s["ct"] = _mm(jnp.where(off, s["lmat"], 0.0), s["t"])
        for p in pairs:
            s = st[p]
            s["t"] = s["t"] - _mm(s["t"], s["ct"])
        b *= 2
    for p in pairs:
        s = st[p]
        s["uw"] = _mm(s["t"], s["rhs"])
    for c, h in pairs:
        s = st[c, h]
        rs = slice(c * CHUNK, (c + 1) * CHUNK)
        hs = slice(h * HEAD_W, (h + 1) * HEAD_W)
        a_uw = _mm(s["attn"], s["uw"])
        u_ref[rs, hs] = s["uw"][:, :HEAD_W]
        w_ref[rs, hs] = s["uw"][:, HEAD_W:].astype(BF16)
        ot_ref[rs, hs] = a_uw[:, :HEAD_W]
        qt_ref[rs, hs] = (s["qg"] - a_uw[:, HEAD_W:]).astype(BF16)


def _gdn_intra(qkv, gb, tc):
    n = qkv.shape[0]
    hw = GDN_HEADS * HEAD_W
    nc = tc // CHUNK
    blk = pl.BlockSpec((tc, hw), lambda i: (i, 0))
    return pl.pallas_call(
        functools.partial(_gdn_intra_kernel, tc=tc),
        grid=(n // tc,),
        in_specs=[pl.BlockSpec((tc, 3 * hw), lambda i: (i, 0)),
                  pl.BlockSpec((tc, LANES), lambda i: (i, 0))],
        out_specs=[blk, blk, blk, blk, blk, pl.BlockSpec((nc, hw), lambda i: (i, 0))],
        out_shape=[jax.ShapeDtypeStruct((n, hw), F32), jax.ShapeDtypeStruct((n, hw), BF16),
                   jax.ShapeDtypeStruct((n, hw), BF16), jax.ShapeDtypeStruct((n, hw), F32),
                   jax.ShapeDtypeStruct((n, hw), F32), jax.ShapeDtypeStruct((n // CHUNK, hw), F32)],
        compiler_params=_cparams("parallel"),
        name="gdn_intra",
    )(qkv, gb)


def _gdn_scan_kernel(u_ref, w_ref, qt_ref, ot_ref, kd_ref, eg_ref, z_ref, ow_ref, o_ref, state_ref, *, tc, batch):
    @pl.when(pl.program_id(0) == 0)
    def _():
        state_ref[...] = jnp.zeros(state_ref.shape, F32)

    chains = [(b, h) for b in range(batch) for h in range(GDN_HEADS)]
    for c in range(tc // CHUNK):
        rs = slice(c * CHUNK, (c + 1) * CHUNK)
        prod, v_new = {}, {}
        for b, h in chains:
            hs = slice(h * HEAD_W, (h + 1) * HEAD_W)
            lhs = jnp.concatenate([w_ref[b, rs, hs], qt_ref[b, rs, hs]], axis=0)
            prod[b, h] = jnp.dot(lhs, state_ref[b * GDN_HEADS + h].astype(BF16), preferred_element_type=F32)
        for b, h in chains:
            hs = slice(h * HEAD_W, (h + 1) * HEAD_W)
            v_new[b, h] = (u_ref[b, rs, hs] - prod[b, h][:CHUNK]).astype(BF16)
        for b, h in chains:
            hs = slice(h * HEAD_W, (h + 1) * HEAD_W)
            idx = b * GDN_HEADS + h
            k_t = kd_ref[b, rs, hs].T.astype(BF16)
            state_ref[idx] = (state_ref[idx] * eg_ref[b, c:c + 1, hs]
                              + jnp.dot(k_t, v_new[b, h], preferred_element_type=F32))
        for b, h in chains:
            hs = slice(h * HEAD_W, (h + 1) * HEAD_W)
            o = prod[b, h][CHUNK:] + ot_ref[b, rs, hs]
            o = o * lax.rsqrt(jnp.mean(o * o, axis=-1, keepdims=True) + NORM_EPS) * ow_ref[...]
            z = z_ref[b, rs, hs].astype(F32)
            o_ref[b, rs, hs] = (o * (z * jax.nn.sigmoid(z))).astype(o_ref.dtype)


def _gdn_scan(u, w, qt, ot, kd, eg, proj, onorm_w, col_z, batch, seq, tc):
    hw = GDN_HEADS * HEAD_W
    nc = tc // CHUNK
    r3 = lambda t: t.reshape(batch, t.shape[0] // batch, t.shape[1])
    blk = pl.BlockSpec((batch, tc, hw), lambda i: (0, i, 0))
    out = pl.pallas_call(
        functools.partial(_gdn_scan_kernel, tc=tc, batch=batch),
        grid=(seq // tc,),
        in_specs=[blk, blk, blk, blk, blk,
                  pl.BlockSpec((batch, nc, hw), lambda i: (0, i, 0)),
                  pl.BlockSpec((batch, tc, hw), lambda i: (0, i, col_z // hw)),
                  pl.BlockSpec((1, HEAD_W), lambda i: (0, 0))],
        out_specs=blk,
        out_shape=jax.ShapeDtypeStruct((batch, seq, hw), BF16),
        scratch_shapes=[pltpu.VMEM((batch * GDN_HEADS, GDN_K_DIM, GDN_V_DIM), F32)],
        compiler_params=_cparams("arbitrary"),
        name="gdn_scan",
    )(r3(u), r3(w), r3(qt), r3(ot), r3(kd), r3(eg), r3(proj), onorm_w)
    return out.reshape(batch * seq, hw)


def _merge_kernel(x_ref, oa_ref, ob_ref, oc_ref, gate_ref, wa_ref, wb_ref, wc_ref, wout_ref, mod_ref, o_ref):
    d = x_ref.shape[1]
    merged = None
    for idx, (o_r, w_r) in enumerate(((oa_ref, wa_ref), (ob_ref, wb_ref), (oc_ref, wc_ref))):
        y = jnp.dot(o_r[...], w_r[...], preferred_element_type=F32)
        g = jax.nn.sigmoid(gate_ref[:, idx * d:(idx + 1) * d].astype(F32))
        merged = g * y if merged is None else merged + g * y
    upd = jnp.dot(merged.astype(BF16), wout_ref[...], preferred_element_type=F32)
    o_ref[...] = x_ref[...] + mod_ref[2:3, :] * upd


def _merge(x2, oa, ob, oc, proj, wa, wb, wc, wout, mod, col_gate, seq, tm):
    n, d = x2.shape
    w = DIFF_HEADS * HEAD_W
    tiles_per_batch = seq // tm
    row = lambda i: (i, 0)
    const = lambda i: (0, 0)
    return pl.pallas_call(
        _merge_kernel,
        grid=(n // tm,),
        in_specs=[pl.BlockSpec((tm, d), row), pl.BlockSpec((tm, w), row), pl.BlockSpec((tm, w), row),
                  pl.BlockSpec((tm, w), row),
                  pl.BlockSpec((tm, N_BRANCHES * d), lambda i: (i, col_gate // (N_BRANCHES * d))),
                  pl.BlockSpec((w, d), const), pl.BlockSpec((w, d), const), pl.BlockSpec((w, d), const),
                  pl.BlockSpec((d, d), const),
                  pl.BlockSpec((None, 6, d), lambda i: (i // tiles_per_batch, 0, 0))],
        out_specs=pl.BlockSpec((tm, d), row),
        out_shape=jax.ShapeDtypeStruct((n, d), F32),
        compiler_params=_cparams("parallel"),
        name="merge",
    )(x2, oa, ob, oc, proj, wa, wb, wc, wout, mod)


def _ffn_kernel(x_ref, nw_ref, mod_ref, wg_ref, wu_ref, wd_ref, o_ref, h_ref, acc_ref):
    j = pl.program_id(1)

    @pl.when(j == 0)
    def _():
        h_ref[...] = _norm_mod(x_ref[...], nw_ref[...], mod_ref[...], 3, 4).astype(BF16)
        acc_ref[...] = jnp.zeros(acc_ref.shape, F32)

    h = h_ref[...]
    gate = jnp.dot(h, wg_ref[...], preferred_element_type=F32)
    up = jnp.dot(h, wu_ref[...], preferred_element_type=F32)
    act = (gate * jax.nn.sigmoid(gate) * up).astype(BF16)
    acc_ref[...] += jnp.dot(act, wd_ref[...], preferred_element_type=F32)

    @pl.when(j == pl.num_programs(1) - 1)
    def _():
        o_ref[...] = x_ref[...] + mod_ref[5:6, :] * acc_ref[...]


def _ffn(x2, nw, mod, w_gu, w_down, seq, tm, th):
    n, d = x2.shape
    hidden = w_down.shape[0]
    nh = hidden // th
    tiles_per_batch = seq // tm
    return pl.pallas_call(
        _ffn_kernel,
        grid=(n // tm, nh),
        in_specs=[pl.BlockSpec((tm, d), lambda i, j: (i, 0)),
                  pl.BlockSpec((1, d), lambda i, j: (0, 0)),
                  pl.BlockSpec((None, 6, d), lambda i, j: (i // tiles_per_batch, 0, 0)),
                  pl.BlockSpec((d, th), lambda i, j: (0, j)),
                  pl.BlockSpec((d, th), lambda i, j: (0, nh + j)),
                  pl.BlockSpec((th, d), lambda i, j: (j, 0))],
        out_specs=pl.BlockSpec((tm, d), lambda i, j: (i, 0)),
        out_shape=jax.ShapeDtypeStruct((n, d), F32),
        scratch_shapes=[pltpu.VMEM((tm, d), BF16), pltpu.VMEM((tm, d), F32)],
        compiler_params=_cparams("parallel", "arbitrary"),
        name="ffn",
    )(x2, nw, mod, w_gu, w_gu, w_down)


def _pad_cols(w, width):
    return jnp.pad(w, ((0, 0), (0, width - w.shape[1])))


def _in_proj_layout(d_model):
    a = DIFF_HEADS * HEAD_W
    g = GDN_HEADS * HEAD_W
    cols = {}
    off = 0
    for name, width in (("gate", N_BRANCHES * d_model), ("c_qkv", 3 * g), ("a_q", a), ("a_k", a), ("a_v", a),
                        ("c_z", g), ("c_ba", LANES), ("b_q", MLA_Q_LORA), ("b_kv", MLA_Q_LORA)):
        if off % width:
            off += width - off % width
        cols[name] = off
        off += width
    return cols, off


def _build_w_in(w_in, d_model, total):
    a = DIFF_HEADS * 2 * DIFF_QK_DIM
    g = GDN_HEADS * GDN_K_DIM
    widths = (a, a, DIFF_HEADS * DIFF_V_DIM, MLA_Q_LORA, MLA_KV_LORA + MLA_ROPE, g, g, GDN_HEADS * GDN_V_DIM,
              GDN_HEADS * GDN_V_DIM, GDN_HEADS, GDN_HEADS, N_BRANCHES * d_model)
    offs = np.concatenate([[0], np.cumsum(widths)])
    part = [w_in[:, int(offs[i]):int(offs[i + 1])] for i in range(len(widths))]
    cols, _ = _in_proj_layout(d_model)
    out = jnp.zeros((w_in.shape[0], total), w_in.dtype)
    place = (("a_q", part[0]), ("a_k", part[1]), ("a_v", part[2]), ("b_q", part[3]), ("b_kv", part[4]),
             ("c_qkv", jnp.concatenate(part[5:8], axis=1)), ("c_z", part[8]), ("gate", part[11]),
             ("c_ba", jnp.concatenate(part[9:11], axis=1)))
    for name, p in place:
        out = lax.dynamic_update_slice(out, p, (0, cols[name]))
    return out.astype(BF16)


def _head_pad(w, heads, per_head, lane_off=0):
    r = w.shape[0]
    w = w.reshape(r, heads, per_head)
    w = jnp.pad(w, ((0, 0), (0, 0), (lane_off, HEAD_W - per_head - lane_off)))
    return w.reshape(r, heads * HEAD_W)


def kernel(x, c, positions, ada_w, ada_b, norm1_w, w_in, diff_qnorm_w, diff_knorm_w, diff_lambda, diff_subln_w, w_o_diff, mla_qa_norm_w, mla_q_up, mla_kva_norm_w, mla_kv_up, mla_qnorm_w, mla_knorm_w, w_o_mla, gdn_conv_w, gdn_a_log, gdn_dt_bias, gdn_onorm_w, w_o_gdn, w_out, norm2_w, ffn_w_gu, ffn_w_down):
    batch, seq, d = x.shape
    depth = ada_w.shape[0]
    n = batch * seq
    qk_dim = MLA_NOPE + MLA_ROPE

    tm = min(1024, seq)
    tq = min(512, seq)
    tc = min(512, seq)

    c_pad = jnp.pad(c, ((0, SUBLANES - batch), (0, 0)))
    mod_all = _ada_mod(c_pad, ada_w, ada_b)[:, :batch].reshape(depth, batch, 6, d)

    cols, total = _in_proj_layout(d)
    tn = 512
    total = -(-total // tn) * tn

    pos2 = positions.reshape(n, 1)
    half = MLA_ROPE // 2
    inv_freq = ROPE_THETA ** (-jnp.arange(half, dtype=F32) / half)
    invf = jnp.zeros((1, HEAD_W), F32).at[0, MLA_NOPE:qk_dim].set(jnp.concatenate([inv_freq, inv_freq]))

    x2 = x.reshape(n, d)
    for l in range(depth):
        lambda_init = 0.8 - 0.6 * math.exp(-0.3 * l)
        mod = mod_all[l]
        proj = _in_proj(x2, norm1_w[l][None], mod, _build_w_in(w_in[l], d, total), seq, tm, tn)

        qw2 = jnp.tile(diff_qnorm_w[l], 2)[None]
        kw2 = jnp.tile(diff_knorm_w[l], 2)[None]
        a_q, a_k = _diff_prep(proj, qw2, kw2, cols["a_q"], cols["a_k"], tm)
        o_a = _attention(a_q, a_k, proj, (diff_lambda[l], diff_subln_w[l][None]), heads=DIFF_HEADS, batch=batch,
                         seq=seq, tq=min(2 * tq, seq), tk=tq, rg=256, n_maps=2, v_col=cols["a_v"],
                         lambda_init=lambda_init)

        kv_up = mla_kv_up[l].reshape(MLA_KV_LORA, MLA_HEADS, MLA_NOPE + MLA_V_DIM)
        wk_nope = _head_pad(kv_up[:, :, :MLA_NOPE].reshape(MLA_KV_LORA, -1), MLA_HEADS, MLA_NOPE)
        place_rope = _head_pad(jnp.tile(jnp.eye(MLA_ROPE, dtype=F32), (1, MLA_HEADS)), MLA_HEADS, MLA_ROPE, MLA_NOPE)
        pad_rows = MLA_Q_LORA - MLA_KV_LORA - MLA_ROPE
        wk = jnp.concatenate([wk_nope, place_rope, jnp.zeros((pad_rows, MLA_HEADS * HEAD_W), F32)], axis=0)
        wv = jnp.pad(kv_up[:, :, MLA_NOPE:].reshape(MLA_KV_LORA, -1), ((0, MLA_Q_LORA - MLA_KV_LORA), (0, 0)))
        kva_w = jnp.pad(mla_kva_norm_w[l], (0, MLA_Q_LORA - MLA_KV_LORA))[None]
        b_q, b_k, b_v = _mla_prep(
            proj, pos2, mla_qa_norm_w[l][None], _head_pad(mla_q_up[l], MLA_HEADS, qk_dim).astype(BF16),
            kva_w, wk.astype(BF16), wv.astype(BF16),
            jnp.pad(mla_qnorm_w[l], (0, HEAD_W - qk_dim))[None], jnp.pad(mla_knorm_w[l], (0, HEAD_W - qk_dim))[None],
            invf, cols["b_q"], cols["b_kv"], tm)
        o_b = _attention(b_q, b_k, b_v, (), heads=MLA_HEADS, batch=batch, seq=seq, tq=min(4 * tq, seq), tk=tq,
                         rg=256, n_maps=1)

        exp_a = jnp.zeros((1, LANES), F32).at[0, GDN_HEADS:2 * GDN_HEADS].set(gdn_a_log[l])
        dt_b = jnp.zeros((1, LANES), F32).at[0, GDN_HEADS:2 * GDN_HEADS].set(gdn_dt_bias[l])
        c_qkv, c_gb = _gdn_prep(proj, gdn_conv_w[l], exp_a, dt_b, cols["c_qkv"], cols["c_ba"], seq, tm)
        g_u, g_w, g_qt, g_ot, g_kd, g_eg = _gdn_intra(c_qkv, c_gb, tc)
        o_c = _gdn_scan(g_u, g_w, g_qt, g_ot, g_kd, g_eg, proj, gdn_onorm_w[l][None], cols["c_z"], batch, seq, tc)

        x2 = _merge(x2, o_a, o_b, o_c, proj, w_o_diff[l].astype(BF16), w_o_mla[l].astype(BF16),
                    w_o_gdn[l].astype(BF16), w_out[l].astype(BF16), mod, cols["gate"], seq, min(512, seq))
        x2 = _ffn(x2, norm2_w[l][None], mod, ffn_w_gu[l].astype(BF16), ffn_w_down[l].astype(BF16), seq, tm, 256)
    return x2.reshape(batch, seq, d)
```

```python
import functools
import math

import jax
import jax.numpy as jnp
import numpy as np
from jax import lax
from jax.experimental import pallas as pl
from jax.experimental.pallas import tpu as pltpu

F32 = jnp.float32
BF16 = jnp.bfloat16

CHUNK = 64
NORM_EPS = 1e-6
ROPE_THETA = 10000.0
DIFF_HEADS = 4
DIFF_QK_DIM = 64
DIFF_V_DIM = 128
MLA_HEADS = 4
MLA_Q_LORA = 384
MLA_KV_LORA = 256
MLA_NOPE = 64
MLA_ROPE = 32
MLA_V_DIM = 128
GDN_HEADS = 4
GDN_K_DIM = 128
GDN_V_DIM = 128
GDN_CONV = 4
N_BRANCHES = 3

LANES = 128
SUBLANES = 8
HEAD_W = 128
LOG2E = 1.4426950408889634
NEG_BIG = -1e30
VMEM_LIMIT = 56 * 1024 * 1024


def _cparams(*sem):
    return pltpu.CompilerParams(dimension_semantics=sem, vmem_limit_bytes=VMEM_LIMIT)


def _mm(a, b):
    return jnp.dot(a.astype(BF16), b.astype(BF16), preferred_element_type=F32)


def _mm_nt(a, b):
    return lax.dot_general(a.astype(BF16), b.astype(BF16), (((1,), (1,)), ((), ())),
                           preferred_element_type=F32)


def _mod_kernel(c_ref, w_ref, b_ref, o_ref):
    o_ref[...] = jnp.dot(c_ref[...], w_ref[...], preferred_element_type=F32) + b_ref[...]


def _ada_mod(c_pad, ada_w, ada_b):
    depth, d, n6 = ada_w.shape
    tn = 1536
    return pl.pallas_call(
        _mod_kernel,
        grid=(depth, n6 // tn),
        in_specs=[pl.BlockSpec((SUBLANES, d), lambda l, j: (0, 0)),
                  pl.BlockSpec((None, d, tn), lambda l, j: (l, 0, j)),
                  pl.BlockSpec((None, 1, tn), lambda l, j: (l, 0, j))],
        out_specs=pl.BlockSpec((None, SUBLANES, tn), lambda l, j: (l, 0, j)),
        out_shape=jax.ShapeDtypeStruct((depth, SUBLANES, n6), F32),
        compiler_params=_cparams("parallel", "parallel"),
        name="ada_mod",
    )(c_pad, ada_w, ada_b.reshape(depth, 1, n6))


def _norm_mod(x, nw, mod, shift_row, scale_row):
    r = lax.rsqrt(jnp.mean(x * x, axis=-1, keepdims=True) + NORM_EPS)
    h = x * r * nw
    return h * (1.0 + mod[scale_row:scale_row + 1, :]) + mod[shift_row:shift_row + 1, :]


def _in_proj_kernel(x_ref, nw_ref, mod_ref, w_ref, o_ref, h_ref):
    @pl.when(pl.program_id(1) == 0)
    def _():
        h_ref[...] = _norm_mod(x_ref[...], nw_ref[...], mod_ref[...], 0, 1).astype(BF16)

    o_ref[...] = jnp.dot(h_ref[...], w_ref[...], preferred_element_type=F32).astype(o_ref.dtype)


def _in_proj(x2, nw, mod, w, seq, tm, tn):
    n, d = x2.shape
    cols = w.shape[1]
    tiles_per_batch = seq // tm
    return pl.pallas_call(
        _in_proj_kernel,
        grid=(n // tm, cols // tn),
        in_specs=[pl.BlockSpec((tm, d), lambda i, j: (i, 0)),
                  pl.BlockSpec((1, d), lambda i, j: (0, 0)),
                  pl.BlockSpec((None, 6, d), lambda i, j: (i // tiles_per_batch, 0, 0)),
                  pl.BlockSpec((d, tn), lambda i, j: (0, j))],
        out_specs=pl.BlockSpec((tm, tn), lambda i, j: (i, j)),
        out_shape=jax.ShapeDtypeStruct((n, cols), BF16),
        scratch_shapes=[pltpu.VMEM((tm, d), BF16)],
        compiler_params=_cparams("parallel", "arbitrary"),
        name="in_proj",
    )(x2, nw, mod, w)


def _half_rms(t, w):
    lane = lax.broadcasted_iota(jnp.int32, t.shape, 1)
    lo = lane < DIFF_QK_DIM
    sq = t * t
    s_lo = jnp.sum(jnp.where(lo, sq, 0.0), axis=-1, keepdims=True)
    s_hi = jnp.sum(jnp.where(lo, 0.0, sq), axis=-1, keepdims=True)
    r = lax.rsqrt(jnp.where(lo, s_lo, s_hi) * (1.0 / DIFF_QK_DIM) + NORM_EPS)
    return t * r * w


def _diff_prep_kernel(q_ref, k_ref, qw_ref, kw_ref, qo_ref, ko_ref):
    qscale = (DIFF_QK_DIM ** -0.5) * LOG2E
    for h in range(DIFF_HEADS):
        sl = slice(h * HEAD_W, (h + 1) * HEAD_W)
        qo_ref[:, sl] = (_half_rms(q_ref[:, sl].astype(F32), qw_ref[...]) * qscale).astype(BF16)
        ko_ref[:, sl] = _half_rms(k_ref[:, sl].astype(F32), kw_ref[...]).astype(BF16)


def _diff_prep(proj, qw2, kw2, col_q, col_k, tm):
    n = proj.shape[0]
    w = DIFF_HEADS * HEAD_W
    return pl.pallas_call(
        _diff_prep_kernel,
        grid=(n // tm,),
        in_specs=[pl.BlockSpec((tm, w), lambda i: (i, col_q // w)),
                  pl.BlockSpec((tm, w), lambda i: (i, col_k // w)),
                  pl.BlockSpec((1, HEAD_W), lambda i: (0, 0)),
                  pl.BlockSpec((1, HEAD_W), lambda i: (0, 0))],
        out_specs=[pl.BlockSpec((tm, w), lambda i: (i, 0)),
                   pl.BlockSpec((tm, w), lambda i: (i, 0))],
        out_shape=[jax.ShapeDtypeStruct((n, w), BF16), jax.ShapeDtypeStruct((n, w), BF16)],
        compiler_params=_cparams("parallel"),
        name="diff_prep",
    )(proj, proj, qw2, kw2)


def _rope_tile(t, cos_t, sin_lo, sin_hi):
    half = MLA_ROPE // 2
    return (t * cos_t + pltpu.roll(t, HEAD_W - half, 1) * sin_lo + pltpu.roll(t, half, 1) * sin_hi)


def _mla_prep_kernel(q_ref, kv_ref, pos_ref, qa_ref, qup_ref, kva_ref, wk_ref, wv_ref,
                     qn_ref, kn_ref, invf_ref, qo_ref, ko_ref, vo_ref):
    qk_dim = MLA_NOPE + MLA_ROPE
    qscale = (qk_dim ** -0.5) * LOG2E
    half = MLA_ROPE // 2

    xq = q_ref[...].astype(F32)
    rq = lax.rsqrt(jnp.mean(xq * xq, axis=-1, keepdims=True) + NORM_EPS)
    q = _mm(xq * rq * qa_ref[...], qup_ref[...])

    xkv = kv_ref[...].astype(F32)
    lane_kv = lax.broadcasted_iota(jnp.int32, xkv.shape, 1)
    is_lat = lane_kv < MLA_KV_LORA
    ssq = jnp.sum(jnp.where(is_lat, xkv * xkv, 0.0), axis=-1, keepdims=True)
    rkv = lax.rsqrt(ssq * (1.0 / MLA_KV_LORA) + NORM_EPS)
    lhs = jnp.where(is_lat, xkv * rkv * kva_ref[...], xkv).astype(BF16)
    k = jnp.dot(lhs, wk_ref[...], preferred_element_type=F32)
    vo_ref[...] = jnp.dot(lhs, wv_ref[...], preferred_element_type=F32).astype(BF16)

    ang = pos_ref[...].astype(F32) * invf_ref[...]
    lane = lax.broadcasted_iota(jnp.int32, ang.shape, 1)
    in_lo = (lane >= MLA_NOPE) & (lane < MLA_NOPE + half)
    in_hi = (lane >= MLA_NOPE + half) & (lane < qk_dim)
    cos_a, sin_a = jnp.cos(ang), jnp.sin(ang)
    cos_t = jnp.where(in_lo | in_hi, cos_a, 1.0)
    sin_lo = jnp.where(in_lo, -sin_a, 0.0)
    sin_hi = jnp.where(in_hi, sin_a, 0.0)

    for h in range(MLA_HEADS):
        sl = slice(h * HEAD_W, (h + 1) * HEAD_W)
        qh = q[:, sl]
        qh = qh * lax.rsqrt(jnp.sum(qh * qh, axis=-1, keepdims=True) * (1.0 / qk_dim) + NORM_EPS)
        qh = _rope_tile(qh * qn_ref[...], cos_t, sin_lo, sin_hi)
        qo_ref[:, sl] = (qh * qscale).astype(BF16)
        kh = k[:, sl]
        kh = kh * lax.rsqrt(jnp.sum(kh * kh, axis=-1, keepdims=True) * (1.0 / qk_dim) + NORM_EPS)
        kh = _rope_tile(kh * kn_ref[...], cos_t, sin_lo, sin_hi)
        ko_ref[:, sl] = kh.astype(BF16)


def _mla_prep(proj, pos2, qa_w, qup, kva_w, wk, wv, qn_w, kn_w, invf, col_q, col_kv, tm):
    n = proj.shape[0]
    w = MLA_HEADS * HEAD_W
    lat = MLA_Q_LORA
    const = lambda i: (0, 0)
    out = jax.ShapeDtypeStruct((n, w), BF16)
    return pl.pallas_call(
        _mla_prep_kernel,
        grid=(n // tm,),
        in_specs=[pl.BlockSpec((tm, lat), lambda i: (i, col_q // lat)),
                  pl.BlockSpec((tm, lat), lambda i: (i, col_kv // lat)),
                  pl.BlockSpec((tm, 1), lambda i: (i, 0)),
                  pl.BlockSpec((1, lat), const),
                  pl.BlockSpec((lat, w), const),
                  pl.BlockSpec((1, lat), const),
                  pl.BlockSpec((lat, w), const),
                  pl.BlockSpec((lat, w), const),
                  pl.BlockSpec((1, HEAD_W), const),
                  pl.BlockSpec((1, HEAD_W), const),
                  pl.BlockSpec((1, HEAD_W), const)],
        out_specs=[pl.BlockSpec((tm, w), lambda i: (i, 0))] * 3,
        out_shape=[out, out, out],
        compiler_params=_cparams("parallel"),
        name="mla_prep",
    )(proj, proj, pos2, qa_w, qup, kva_w, wk, wv, qn_w, kn_w, invf)


def _attn_kernel(*refs, tq, tk, rg, n_maps, lambda_init):
    if n_maps == 2:
        q_ref, k_ref, v_ref, lam_ref, sw_ref, o_ref, qq_ref, s_ref, mx_ref, m_ref, l_ref, acc_ref = refs
    else:
        q_ref, k_ref, v_ref, o_ref, s_ref, mx_ref, m_ref, l_ref, acc_ref = refs
        qq_ref = q_ref
    i = pl.program_id(2)
    rows = n_maps * tq
    if n_maps == 2:
        q = q_ref[...]
        lane = lax.broadcasted_iota(jnp.int32, q.shape, 1)
        zero = jnp.zeros_like(q)
        qq_ref[:tq, :] = jnp.where(lane < DIFF_QK_DIM, q, zero)
        qq_ref[tq:, :] = jnp.where(lane >= DIFF_QK_DIM, q, zero)

    m_ref[...] = jnp.full(m_ref.shape, NEG_BIG, F32)
    l_ref[...] = jnp.zeros(l_ref.shape, F32)
    acc_ref[...] = jnp.zeros(acc_ref.shape, F32)

    n_groups = rows // rg
    n_full = (i * tq) // tk
    n_diag = max(tq // tk, 1)

    def produce(j, slot, g, masked):
        rsl = slice(g * rg, (g + 1) * rg)
        start = pl.multiple_of(j * tk, tk)
        s = lax.dot_general(qq_ref[rsl, :], k_ref[pl.ds(start, tk), :],
                            (((1,), (1,)), ((), ())), preferred_element_type=F32)
        if masked:
            r_idx = lax.broadcasted_iota(jnp.int32, (rg, tk), 0)
            c_idx = lax.broadcasted_iota(jnp.int32, (rg, tk), 1)
            q_chunk = (i * tq + (g * rg) % tq + r_idx) // CHUNK
            k_chunk = (j * tk + c_idx) // CHUNK
            s = jnp.where(k_chunk <= q_chunk, s, NEG_BIG)
        s_ref[slot, rsl, :] = s
        mx_ref[slot, rsl, :] = jnp.broadcast_to(jnp.max(s, axis=-1, keepdims=True), (rg, LANES))

    def consume(j, slot, g, v):
        rsl = slice(g * rg, (g + 1) * rg)
        m_prev = m_ref[rsl, :]
        m_new = jnp.maximum(m_prev, mx_ref[slot, rsl, :])
        alpha = jnp.exp2(m_prev - m_new)
        p = jnp.exp2(s_ref[slot, rsl, :] - jnp.tile(m_new, (1, tk // LANES)))
        p_sum = p[:, :LANES]
        for t in range(1, tk // LANES):
            p_sum = p_sum + p[:, t * LANES:(t + 1) * LANES]
        l_ref[rsl, :] = alpha * l_ref[rsl, :] + p_sum
        m_ref[rsl, :] = m_new
        acc_ref[rsl, :] = alpha * acc_ref[rsl, :] + jnp.dot(p.astype(BF16), v, preferred_element_type=F32)

    def step(j, slot, next_masked, cur_diag=None, next_diag=None):
        start = pl.multiple_of(j * tk, tk)
        v = v_ref[pl.ds(start, tk), :]
        for g in range(n_groups):
            if next_masked is not None and live(g, next_diag):
                produce(j + 1, 1 - slot, g, next_masked)
            if live(g, cur_diag):
                consume(j, slot, g, v)

    def live(g, diag):
        return diag is None or (g * rg) % tq + rg > diag * tk

    for g in range(n_groups):
        produce(0, 0, g, True)

    def body(jj, carry):
        step(2 * jj, 0, False)
        step(2 * jj + 1, 1, False)
        return carry

    n_pairs = jnp.maximum((n_full - 1) // 2, 0)
    lax.fori_loop(0, n_pairs, body, 0)
    left = n_full - 2 * n_pairs

    def tail(first_slot):
        for d in range(n_diag):
            more = d < n_diag - 1
            step(n_full + d, (first_slot + d) % 2, True if more else None, cur_diag=d,
                 next_diag=d + 1 if more else None)

    @pl.when(left == 0)
    def _():
        tail(0)

    if tq % (2 * tk):
        @pl.when(left == 1)
        def _():
            step(n_full - 1, 0, True)
            tail(1)

    @pl.when(left == 2)
    def _():
        step(n_full - 2, 0, False)
        step(n_full - 1, 1, True)
        tail(0)

    o = acc_ref[...] / jnp.sum(l_ref[...], axis=-1, keepdims=True)
    if n_maps == 2:
        lp = lam_ref[...]
        lam = (jnp.exp(jnp.sum(lp[0:1] * lp[1:2], axis=-1, keepdims=True))
               - jnp.exp(jnp.sum(lp[2:3] * lp[3:4], axis=-1, keepdims=True)) + lambda_init)
        o = o[:tq] - lam * o[tq:]
        r = lax.rsqrt(jnp.mean(o * o, axis=-1, keepdims=True) + NORM_EPS)
        o = o * r * sw_ref[...] * (1.0 - lambda_init)
    o_ref[...] = o.astype(o_ref.dtype)


def _attention(q, k, v, extra, *, heads, batch, seq, tq, tk, rg, n_maps, v_col=0, lambda_init=0.0):
    q3, k3, v3 = (t.reshape(batch, seq, t.shape[1]) for t in (q, k, v))
    rows = n_maps * tq
    rg = min(rg, tq)
    vb = v_col // HEAD_W
    scratch = [pltpu.VMEM((2, rows, tk), F32), pltpu.VMEM((2, rows, LANES), F32), pltpu.VMEM((rows, LANES), F32),
               pltpu.VMEM((rows, LANES), F32), pltpu.VMEM((rows, HEAD_W), F32)]
    if n_maps == 2:
        scratch.insert(0, pltpu.VMEM((rows, HEAD_W), BF16))
    in_specs = [pl.BlockSpec((None, tq, HEAD_W), lambda b, h, i: (b, i, h)),
                pl.BlockSpec((None, seq, HEAD_W), lambda b, h, i: (b, 0, h)),
                pl.BlockSpec((None, seq, HEAD_W), lambda b, h, i: (b, 0, vb + h))]
    for e in extra:
        in_specs.append(pl.BlockSpec(e.shape, lambda b, h, i: (0, 0)))
    out = pl.pallas_call(
        functools.partial(_attn_kernel, tq=tq, tk=tk, rg=rg, n_maps=n_maps, lambda_init=lambda_init),
        grid=(batch, heads, seq // tq),
        in_specs=in_specs,
        out_specs=pl.BlockSpec((None, tq, HEAD_W), lambda b, h, i: (b, i, h)),
        out_shape=jax.ShapeDtypeStruct((batch, seq, heads * HEAD_W), BF16),
        scratch_shapes=scratch,
        compiler_params=_cparams("parallel", "parallel", "arbitrary"),
        name="diff_attn" if n_maps == 2 else "mla_attn",
    )(q3, k3, v3, *extra)
    return out.reshape(batch * seq, heads * HEAD_W)


def _gdn_prep_kernel(x_ref, halo_ref, ba_ref, cw_ref, ea_ref, dtb_ref, qkv_ref, gb_ref, *, tiles_per_batch):
    i = pl.program_id(0)
    x = x_ref[...].astype(F32)
    halo = halo_ref[...].astype(F32)
    halo = jnp.where(i % tiles_per_batch == 0, 0.0, halo)
    cw = cw_ref[...]
    row8 = lax.broadcasted_iota(jnp.int32, halo.shape, 0)
    y = x * cw[GDN_CONV - 1:GDN_CONV]
    for s in range(1, GDN_CONV):
        xs = pltpu.roll(x, s, 0)
        hs = pltpu.roll(halo, s, 0)
        first = jnp.where(row8 < s, hs, xs[:SUBLANES])
        xs = jnp.concatenate([first, xs[SUBLANES:]], axis=0)
        y = y + xs * cw[GDN_CONV - 1 - s:GDN_CONV - s]
    y = y * jax.nn.sigmoid(y)
    nqk = 2 * GDN_HEADS
    for h in range(3 * GDN_HEADS):
        sl = slice(h * HEAD_W, (h + 1) * HEAD_W)
        t = y[:, sl]
        if h < nqk:
            t = t * lax.rsqrt(jnp.sum(t * t, axis=-1, keepdims=True) + NORM_EPS)
            if h < GDN_HEADS:
                t = t * (GDN_K_DIM ** -0.5)
        qkv_ref[:, sl] = t
    ba = ba_ref[...].astype(F32)
    lane = lax.broadcasted_iota(jnp.int32, ba.shape, 1)
    sp_in = ba + dtb_ref[...]
    softplus = jnp.maximum(sp_in, 0.0) + jnp.log1p(jnp.exp(-jnp.abs(sp_in)))
    gb_ref[...] = jnp.where(lane < GDN_HEADS, jax.nn.sigmoid(ba), -jnp.exp(ea_ref[...]) * softplus)


def _gdn_prep(proj, conv_w, exp_a, dt_b, col_qkv, col_ba, seq, tm):
    n = proj.shape[0]
    w = 3 * GDN_HEADS * HEAD_W
    tiles_per_batch = seq // tm
    hb = tm // SUBLANES
    return pl.pallas_call(
        functools.partial(_gdn_prep_kernel, tiles_per_batch=tiles_per_batch),
        grid=(n // tm,),
        in_specs=[pl.BlockSpec((tm, w), lambda i: (i, col_qkv // w)),
                  pl.BlockSpec((SUBLANES, w), lambda i: (jnp.maximum(i * hb - 1, 0), col_qkv // w)),
                  pl.BlockSpec((tm, LANES), lambda i: (i, col_ba // LANES)),
                  pl.BlockSpec((GDN_CONV, w), lambda i: (0, 0)),
                  pl.BlockSpec((1, LANES), lambda i: (0, 0)),
                  pl.BlockSpec((1, LANES), lambda i: (0, 0))],
        out_specs=[pl.BlockSpec((tm, w), lambda i: (i, 0)),
                   pl.BlockSpec((tm, LANES), lambda i: (i, 0))],
        out_shape=[jax.ShapeDtypeStruct((n, w), F32), jax.ShapeDtypeStruct((n, LANES), F32)],
        compiler_params=_cparams("parallel"),
        name="gdn_prep",
    )(proj, proj, proj, conv_w, exp_a, dt_b)


def _gdn_intra_kernel(qkv_ref, gb_ref, u_ref, w_ref, qt_ref, ot_ref, kd_ref, eg_ref, *, tc):
    hw = GDN_HEADS * HEAD_W
    nc = tc // CHUNK
    gb = gb_ref[...]
    r_in_chunk = lax.broadcasted_iota(jnp.int32, gb.shape, 0) % CHUNK
    gc = gb
    sh = 1
    while sh < CHUNK:
        gc = gc + jnp.where(r_in_chunk >= sh, pltpu.roll(gc, sh, 0), 0.0)
        sh *= 2
    gc_t = gc.T

    row = lax.broadcasted_iota(jnp.int32, (CHUNK, CHUNK), 0)
    col = lax.broadcasted_iota(jnp.int32, (CHUNK, CHUNK), 1)
    incl = row >= col
    strict = row > col
    eye = (row == col).astype(F32)

    pairs = [(c, h) for c in range(nc) for h in range(GDN_HEADS)]
    st = {}
    for c, h in pairs:
        rs = slice(c * CHUNK, (c + 1) * CHUNK)
        hs = slice(h * HEAD_W, (h + 1) * HEAD_W)
        q = qkv_ref[rs, h * HEAD_W:(h + 1) * HEAD_W]
        k = qkv_ref[rs, hw + h * HEAD_W:hw + (h + 1) * HEAD_W]
        v = qkv_ref[rs, 2 * hw + h * HEAD_W:2 * hw + (h + 1) * HEAD_W]
        beta = gb[rs, h:h + 1]
        g_col = gc[rs, GDN_HEADS + h:GDN_HEADS + h + 1]
        g_row = gc_t[GDN_HEADS + h:GDN_HEADS + h + 1, rs]
        g_last = g_col[CHUNK - 1:CHUNK, :]
        k_beta = k * beta
        e_g = jnp.exp(g_col)
        kd_ref[rs, hs] = k * jnp.exp(g_last - g_col)
        eg_ref[c:c + 1, hs] = jnp.broadcast_to(jnp.exp(g_last), (1, HEAD_W))
        st[c, h] = dict(
            kb16=k_beta.astype(BF16), k16=k.astype(BF16), q16=q.astype(BF16),
            decay=jnp.exp(jnp.where(incl, g_col - g_row, NEG_BIG)),
            rhs=jnp.concatenate([v * beta, k_beta * e_g], axis=1).astype(BF16),
            qg=q * e_g)
    for p in pairs:
        s = st[p]
        s["lmat"] = jnp.where(strict, _mm_nt(s["kb16"], s["k16"]) * s["decay"], 0.0)
        s["attn"] = jnp.where(incl, _mm_nt(s["q16"], s["k16"]) * s["decay"], 0.0).astype(BF16)
        s["t"] = eye - jnp.where((row // 2) == (col // 2), s["lmat"], 0.0)
    b = 2
    while b < CHUNK:
        off = ((row // (2 * b)) == (col // (2 * b))) & ((row // b) % 2 == 1) & ((col // b) % 2 == 0)
        for p in pairs:
            s = st[p]
            s["ct"] = _mm(jnp.where(off, s["lmat"], 0.0), s["t"])
        for p in pairs:
            s = st[p]
            s["t"] = s["t"] - _mm(s["t"], s["ct"])
        b *= 2
    for p in pairs:
        s = st[p]
        s["uw"] = _mm(s["t"], s["rhs"])
    for c, h in pairs:
        s = st[c, h]
        rs = slice(c * CHUNK, (c + 1) * CHUNK)
        hs = slice(h * HEAD_W, (h + 1) * HEAD_W)
        a_uw = _mm(s["attn"], s["uw"])
        u_ref[rs, hs] = s["uw"][:, :HEAD_W]
        w_ref[rs, hs] = s["uw"][:, HEAD_W:].astype(BF16)
        ot_ref[rs, hs] = a_uw[:, :HEAD_W]
        qt_ref[rs, hs] = (s["qg"] - a_uw[:, HEAD_W:]).astype(BF16)


def _gdn_intra(qkv, gb, tc):
    n = qkv.shape[0]
    hw = GDN_HEADS * HEAD_W
    nc = tc // CHUNK
    blk = pl.BlockSpec((tc, hw), lambda i: (i, 0))
    return pl.pallas_call(
        functools.partial(_gdn_intra_kernel, tc=tc),
        grid=(n // tc,),
        in_specs=[pl.BlockSpec((tc, 3 * hw), lambda i: (i, 0)),
                  pl.BlockSpec((tc, LANES), lambda i: (i, 0))],
        out_specs=[blk, blk, blk, blk, blk, pl.BlockSpec((nc, hw), lambda i: (i, 0))],
        out_shape=[jax.ShapeDtypeStruct((n, hw), F32), jax.ShapeDtypeStruct((n, hw), BF16),
                   jax.ShapeDtypeStruct((n, hw), BF16), jax.ShapeDtypeStruct((n, hw), F32),
                   jax.ShapeDtypeStruct((n, hw), F32), jax.ShapeDtypeStruct((n // CHUNK, hw), F32)],
        compiler_params=_cparams("parallel"),
        name="gdn_intra",
    )(qkv, gb)


def _gdn_scan_kernel(u_ref, w_ref, qt_ref, ot_ref, kd_ref, eg_ref, z_ref, ow_ref, o_ref, state_ref, *, tc, batch):
    @pl.when(pl.program_id(0) == 0)
    def _():
        state_ref[...] = jnp.zeros(state_ref.shape, F32)

    chains = [(b, h) for b in range(batch) for h in range(GDN_HEADS)]
    for c in range(tc // CHUNK):
        rs = slice(c * CHUNK, (c + 1) * CHUNK)
        prod, v_new = {}, {}
        for b, h in chains:
            hs = slice(h * HEAD_W, (h + 1) * HEAD_W)
            lhs = jnp.concatenate([w_ref[b, rs, hs], qt_ref[b, rs, hs]], axis=0)
            prod[b, h] = jnp.dot(lhs, state_ref[b * GDN_HEADS + h].astype(BF16), preferred_element_type=F32)
        for b, h in chains:
            hs = slice(h * HEAD_W, (h + 1) * HEAD_W)
            v_new[b, h] = (u_ref[b, rs, hs] - prod[b, h][:CHUNK]).astype(BF16)
        for b, h in chains:
            hs = slice(h * HEAD_W, (h + 1) * HEAD_W)
            idx = b * GDN_HEADS + h
            k_t = kd_ref[b, rs, hs].T.astype(BF16)
            state_ref[idx] = (state_ref[idx] * eg_ref[b, c:c + 1, hs]
                              + jnp.dot(k_t, v_new[b, h], preferred_element_type=F32))
        for b, h in chains:
            hs = slice(h * HEAD_W, (h + 1) * HEAD_W)
            o = prod[b, h][CHUNK:] + ot_ref[b, rs, hs]
            o = o * lax.rsqrt(jnp.mean(o * o, axis=-1, keepdims=True) + NORM_EPS) * ow_ref[...]
            z = z_ref[b, rs, hs].astype(F32)
            o_ref[b, rs, hs] = (o * (z * jax.nn.sigmoid(z))).astype(o_ref.dtype)


def _gdn_scan(u, w, qt, ot, kd, eg, proj, onorm_w, col_z, batch, seq, tc):
    hw = GDN_HEADS * HEAD_W
    nc = tc // CHUNK
    r3 = lambda t: t.reshape(batch, t.shape[0] // batch, t.shape[1])
    blk = pl.BlockSpec((batch, tc, hw), lambda i: (0, i, 0))
    out = pl.pallas_call(
        functools.partial(_gdn_scan_kernel, tc=tc, batch=batch),
        grid=(seq // tc,),
        in_specs=[blk, blk, blk, blk, blk,
                  pl.BlockSpec((batch, nc, hw), lambda i: (0, i, 0)),
                  pl.BlockSpec((batch, tc, hw), lambda i: (0, i, col_z // hw)),
                  pl.BlockSpec((1, HEAD_W), lambda i: (0, 0))],
        out_specs=blk,
        out_shape=jax.ShapeDtypeStruct((batch, seq, hw), BF16),
        scratch_shapes=[pltpu.VMEM((batch * GDN_HEADS, GDN_K_DIM, GDN_V_DIM), F32)],
        compiler_params=_cparams("arbitrary"),
        name="gdn_scan",
    )(r3(u), r3(w), r3(qt), r3(ot), r3(kd), r3(eg), r3(proj), onorm_w)
    return out.reshape(batch * seq, hw)


def _merge_kernel(x_ref, oa_ref, ob_ref, oc_ref, gate_ref, wa_ref, wb_ref, wc_ref, wout_ref, mod_ref, o_ref):
    d = x_ref.shape[1]
    merged = None
    for idx, (o_r, w_r) in enumerate(((oa_ref, wa_ref), (ob_ref, wb_ref), (oc_ref, wc_ref))):
        y = jnp.dot(o_r[...], w_r[...], preferred_element_type=F32)
        g = jax.nn.sigmoid(gate_ref[:, idx * d:(idx + 1) * d].astype(F32))
        merged = g * y if merged is None else merged + g * y
    upd = jnp.dot(merged.astype(BF16), wout_ref[...], preferred_element_type=F32)
    o_ref[...] = x_ref[...] + mod_ref[2:3, :] * upd


def _merge(x2, oa, ob, oc, proj, wa, wb, wc, wout, mod, col_gate, seq, tm):
    n, d = x2.shape
    w = DIFF_HEADS * HEAD_W
    tiles_per_batch = seq // tm
    row = lambda i: (i, 0)
    const = lambda i: (0, 0)
    return pl.pallas_call(
        _merge_kernel,
        grid=(n // tm,),
        in_specs=[pl.BlockSpec((tm, d), row), pl.BlockSpec((tm, w), row), pl.BlockSpec((tm, w), row),
                  pl.BlockSpec((tm, w), row),
                  pl.BlockSpec((tm, N_BRANCHES * d), lambda i: (i, col_gate // (N_BRANCHES * d))),
                  pl.BlockSpec((w, d), const), pl.BlockSpec((w, d), const), pl.BlockSpec((w, d), const),
                  pl.BlockSpec((d, d), const),
                  pl.BlockSpec((None, 6, d), lambda i: (i // tiles_per_batch, 0, 0))],
        out_specs=pl.BlockSpec((tm, d), row),
        out_shape=jax.ShapeDtypeStruct((n, d), F32),
        compiler_params=_cparams("parallel"),
        name="merge",
    )(x2, oa, ob, oc, proj, wa, wb, wc, wout, mod)


def _ffn_kernel(x_ref, nw_ref, mod_ref, wg_ref, wu_ref, wd_ref, o_ref, h_ref, acc_ref):
    j = pl.program_id(1)

    @pl.when(j == 0)
    def _():
        h_ref[...] = _norm_mod(x_ref[...], nw_ref[...], mod_ref[...], 3, 4).astype(BF16)
        acc_ref[...] = jnp.zeros(acc_ref.shape, F32)

    h = h_ref[...]
    gate = jnp.dot(h, wg_ref[...], preferred_element_type=F32)
    up = jnp.dot(h, wu_ref[...], preferred_element_type=F32)
    act = (gate * jax.nn.sigmoid(gate) * up).astype(BF16)
    acc_ref[...] += jnp.dot(act, wd_ref[...], preferred_element_type=F32)

    @pl.when(j == pl.num_programs(1) - 1)
    def _():
        o_ref[...] = x_ref[...] + mod_ref[5:6, :] * acc_ref[...]


def _ffn(x2, nw, mod, w_gu, w_down, seq, tm, th):
    n, d = x2.shape
    hidden = w_down.shape[0]
    nh = hidden // th
    tiles_per_batch = seq // tm
    return pl.pallas_call(
        _ffn_kernel,
        grid=(n // tm, nh),
        in_specs=[pl.BlockSpec((tm, d), lambda i, j: (i, 0)),
                  pl.BlockSpec((1, d), lambda i, j: (0, 0)),
                  pl.BlockSpec((None, 6, d), lambda i, j: (i // tiles_per_batch, 0, 0)),
                  pl.BlockSpec((d, th), lambda i, j: (0, j)),
                  pl.BlockSpec((d, th), lambda i, j: (0, nh + j)),
                  pl.BlockSpec((th, d), lambda i, j: (j, 0))],
        out_specs=pl.BlockSpec((tm, d), lambda i, j: (i, 0)),
        out_shape=jax.ShapeDtypeStruct((n, d), F32),
        scratch_shapes=[pltpu.VMEM((tm, d), BF16), pltpu.VMEM((tm, d), F32)],
        compiler_params=_cparams("parallel", "arbitrary"),
        name="ffn",
    )(x2, nw, mod, w_gu, w_gu, w_down)


def _pad_cols(w, width):
    return jnp.pad(w, ((0, 0), (0, width - w.shape[1])))


def _in_proj_layout(d_model):
    a = DIFF_HEADS * HEAD_W
    g = GDN_HEADS * HEAD_W
    cols = {}
    off = 0
    for name, width in (("gate", N_BRANCHES * d_model), ("c_qkv", 3 * g), ("a_q", a), ("a_k", a), ("a_v", a),
                        ("c_z", g), ("c_ba", LANES), ("b_q", MLA_Q_LORA), ("b_kv", MLA_Q_LORA)):
        if off % width:
            off += width - off % width
        cols[name] = off
        off += width
    return cols, off


def _build_w_in(w_in, d_model, total):
    a = DIFF_HEADS * 2 * DIFF_QK_DIM
    g = GDN_HEADS * GDN_K_DIM
    widths = (a, a, DIFF_HEADS * DIFF_V_DIM, MLA_Q_LORA, MLA_KV_LORA + MLA_ROPE, g, g, GDN_HEADS * GDN_V_DIM,
              GDN_HEADS * GDN_V_DIM, GDN_HEADS, GDN_HEADS, N_BRANCHES * d_model)
    offs = np.concatenate([[0], np.cumsum(widths)])
    part = [w_in[:, int(offs[i]):int(offs[i + 1])] for i in range(len(widths))]
    cols, _ = _in_proj_layout(d_model)
    out = jnp.zeros((w_in.shape[0], total), w_in.dtype)
    place = (("a_q", part[0]), ("a_k", part[1]), ("a_v", part[2]), ("b_q", part[3]), ("b_kv", part[4]),
             ("c_qkv", jnp.concatenate(part[5:8], axis=1)), ("c_z", part[8]), ("gate", part[11]),
             ("c_ba", jnp.concatenate(part[9:11], axis=1)))
    for name, p in place:
        out = lax.dynamic_update_slice(out, p, (0, cols[name]))
    return out.astype(BF16)


def _head_pad(w, heads, per_head, lane_off=0):
    r = w.shape[0]
    w = w.reshape(r, heads, per_head)
    w = jnp.pad(w, ((0, 0), (0, 0), (lane_off, HEAD_W - per_head - lane_off)))
    return w.reshape(r, heads * HEAD_W)


def kernel(x, c, positions, ada_w, ada_b, norm1_w, w_in, diff_qnorm_w, diff_knorm_w, diff_lambda, diff_subln_w, w_o_diff, mla_qa_norm_w, mla_q_up, mla_kva_norm_w, mla_kv_up, mla_qnorm_w, mla_knorm_w, w_o_mla, gdn_conv_w, gdn_a_log, gdn_dt_bias, gdn_onorm_w, w_o_gdn, w_out, norm2_w, ffn_w_gu, ffn_w_down):
    batch, seq, d = x.shape
    depth = ada_w.shape[0]
    n = batch * seq
    qk_dim = MLA_NOPE + MLA_ROPE

    tm = min(1024, seq)
    tq = min(512, seq)
    tc = min(512, seq)

    c_pad = jnp.pad(c, ((0, SUBLANES - batch), (0, 0)))
    mod_all = _ada_mod(c_pad, ada_w, ada_b)[:, :batch].reshape(depth, batch, 6, d)

    cols, total = _in_proj_layout(d)
    tn = 512
    total = -(-total // tn) * tn

    pos2 = positions.reshape(n, 1)
    half = MLA_ROPE // 2
    inv_freq = ROPE_THETA ** (-jnp.arange(half, dtype=F32) / half)
    invf = jnp.zeros((1, HEAD_W), F32).at[0, MLA_NOPE:qk_dim].set(jnp.concatenate([inv_freq, inv_freq]))

    x2 = x.reshape(n, d)
    for l in range(depth):
        lambda_init = 0.8 - 0.6 * math.exp(-0.3 * l)
        mod = mod_all[l]
        proj = _in_proj(x2, norm1_w[l][None], mod, _build_w_in(w_in[l], d, total), seq, tm, tn)

        qw2 = jnp.tile(diff_qnorm_w[l], 2)[None]
        kw2 = jnp.tile(diff_knorm_w[l], 2)[None]
        a_q, a_k = _diff_prep(proj, qw2, kw2, cols["a_q"], cols["a_k"], tm)
        o_a = _attention(a_q, a_k, proj, (diff_lambda[l], diff_subln_w[l][None]), heads=DIFF_HEADS, batch=batch,
                         seq=seq, tq=min(2 * tq, seq), tk=tq, rg=256, n_maps=2, v_col=cols["a_v"],
                         lambda_init=lambda_init)

        kv_up = mla_kv_up[l].reshape(MLA_KV_LORA, MLA_HEADS, MLA_NOPE + MLA_V_DIM)
        wk_nope = _head_pad(kv_up[:, :, :MLA_NOPE].reshape(MLA_KV_LORA, -1), MLA_HEADS, MLA_NOPE)
        place_rope = _head_pad(jnp.tile(jnp.eye(MLA_ROPE, dtype=F32), (1, MLA_HEADS)), MLA_HEADS, MLA_ROPE, MLA_NOPE)
        pad_rows = MLA_Q_LORA - MLA_KV_LORA - MLA_ROPE
        wk = jnp.concatenate([wk_nope, place_rope, jnp.zeros((pad_rows, MLA_HEADS * HEAD_W), F32)], axis=0)
        wv = jnp.pad(kv_up[:, :, MLA_NOPE:].reshape(MLA_KV_LORA, -1), ((0, MLA_Q_LORA - MLA_KV_LORA), (0, 0)))
        kva_w = jnp.pad(mla_kva_norm_w[l], (0, MLA_Q_LORA - MLA_KV_LORA))[None]
        b_q, b_k, b_v = _mla_prep(
            proj, pos2, mla_qa_norm_w[l][None], _head_pad(mla_q_up[l], MLA_HEADS, qk_dim).astype(BF16),
            kva_w, wk.astype(BF16), wv.astype(BF16),
            jnp.pad(mla_qnorm_w[l], (0, HEAD_W - qk_dim))[None], jnp.pad(mla_knorm_w[l], (0, HEAD_W - qk_dim))[None],
            invf, cols["b_q"], cols["b_kv"], tm)
        o_b = _attention(b_q, b_k, b_v, (), heads=MLA_HEADS, batch=batch, seq=seq, tq=min(4 * tq, seq), tk=tq,
                         rg=256, n_maps=1)

        exp_a = jnp.zeros((1, LANES), F32).at[0, GDN_HEADS:2 * GDN_HEADS].set(gdn_a_log[l])
        dt_b = jnp.zeros((1, LANES), F32).at[0, GDN_HEADS:2 * GDN_HEADS].set(gdn_dt_bias[l])
        c_qkv, c_gb = _gdn_prep(proj, gdn_conv_w[l], exp_a, dt_b, cols["c_qkv"], cols["c_ba"], seq, tm)
        g_u, g_w, g_qt, g_ot, g_kd, g_eg = _gdn_intra(c_qkv, c_gb, tc)
        o_c = _gdn_scan(g_u, g_w, g_qt, g_ot, g_kd, g_eg, proj, gdn_onorm_w[l][None], cols["c_z"], batch, seq, tc)

        x2 = _merge(x2, o_a, o_b, o_c, proj, w_o_diff[l].astype(BF16), w_o_mla[l].astype(BF16),
                    w_o_gdn[l].astype(BF16), w_out[l].astype(BF16), mod, cols["gate"], seq, min(512, seq))
        x2 = _ffn(x2, norm2_w[l][None], mod, ffn_w_gu[l].astype(BF16), ffn_w_down[l].astype(BF16), seq, tm, 256)
    return x2.reshape(batch, seq, d)
```

```python
import functools
import math

import jax
import jax.numpy as jnp
import numpy as np
from jax import lax
from jax.experimental import pallas as pl
from jax.experimental.pallas import tpu as pltpu

F32 = jnp.float32
BF16 = jnp.bfloat16

CHUNK = 64
NORM_EPS = 1e-6
ROPE_THETA = 10000.0
DIFF_HEADS = 4
DIFF_QK_DIM = 64
DIFF_V_DIM = 128
MLA_HEADS = 4
MLA_Q_LORA = 384
MLA_KV_LORA = 256
MLA_NOPE = 64
MLA_ROPE = 32
MLA_V_DIM = 128
GDN_HEADS = 4
GDN_K_DIM = 128
GDN_V_DIM = 128
GDN_CONV = 4
N_BRANCHES = 3

LANES = 128
SUBLANES = 8
HEAD_W = 128
LOG2E = 1.4426950408889634
NEG_BIG = -1e30
VMEM_LIMIT = 56 * 1024 * 1024


def _cparams(*sem):
    return pltpu.CompilerParams(dimension_semantics=sem, vmem_limit_bytes=VMEM_LIMIT)


def _mm(a, b):
    return jnp.dot(a.astype(BF16), b.astype(BF16), preferred_element_type=F32)


def _mm_nt(a, b):
    return lax.dot_general(a.astype(BF16), b.astype(BF16), (((1,), (1,)), ((), ())),
                           preferred_element_type=F32)


def _mod_kernel(c_ref, w_ref, b_ref, o_ref):
    o_ref[...] = jnp.dot(c_ref[...], w_ref[...], preferred_element_type=F32) + b_ref[...]


def _ada_mod(c_pad, ada_w, ada_b):
    depth, d, n6 = ada_w.shape
    tn = 1536
    return pl.pallas_call(
        _mod_kernel,
        grid=(depth, n6 // tn),
        in_specs=[pl.BlockSpec((SUBLANES, d), lambda l, j: (0, 0)),
                  pl.BlockSpec((None, d, tn), lambda l, j: (l, 0, j)),
                  pl.BlockSpec((None, 1, tn), lambda l, j: (l, 0, j))],
        out_specs=pl.BlockSpec((None, SUBLANES, tn), lambda l, j: (l, 0, j)),
        out_shape=jax.ShapeDtypeStruct((depth, SUBLANES, n6), F32),
        compiler_params=_cparams("parallel", "parallel"),
        name="ada_mod",
    )(c_pad, ada_w, ada_b.reshape(depth, 1, n6))


def _norm_mod(x, nw, mod, shift_row, scale_row):
    r = lax.rsqrt(jnp.mean(x * x, axis=-1, keepdims=True) + NORM_EPS)
    h = x * r * nw
    return h * (1.0 + mod[scale_row:scale_row + 1, :]) + mod[shift_row:shift_row + 1, :]


def _in_proj_kernel(x_ref, nw_ref, mod_ref, w_ref, o_ref, h_ref):
    @pl.when(pl.program_id(1) == 0)
    def _():
        h_ref[...] = _norm_mod(x_ref[...], nw_ref[...], mod_ref[...], 0, 1).astype(BF16)

    o_ref[...] = jnp.dot(h_ref[...], w_ref[...], preferred_element_type=F32).astype(o_ref.dtype)


def _in_proj(x2, nw, mod, w, seq, tm, tn):
    n, d = x2.shape
    cols = w.shape[1]
    tiles_per_batch = seq // tm
    return pl.pallas_call(
        _in_proj_kernel,
        grid=(n // tm, cols // tn),
        in_specs=[pl.BlockSpec((tm, d), lambda i, j: (i, 0)),
                  pl.BlockSpec((1, d), lambda i, j: (0, 0)),
                  pl.BlockSpec((None, 6, d), lambda i, j: (i // tiles_per_batch, 0, 0)),
                  pl.BlockSpec((d, tn), lambda i, j: (0, j))],
        out_specs=pl.BlockSpec((tm, tn), lambda i, j: (i, j)),
        out_shape=jax.ShapeDtypeStruct((n, cols), BF16),
        scratch_shapes=[pltpu.VMEM((tm, d), BF16)],
        compiler_params=_cparams("parallel", "arbitrary"),
        name="in_proj",
    )(x2, nw, mod, w)


def _half_rms(t, w):
    lane = lax.broadcasted_iota(jnp.int32, t.shape, 1)
    lo = lane < DIFF_QK_DIM
    sq = t * t
    s_lo = jnp.sum(jnp.where(lo, sq, 0.0), axis=-1, keepdims=True)
    s_hi = jnp.sum(jnp.where(lo, 0.0, sq), axis=-1, keepdims=True)
    r = lax.rsqrt(jnp.where(lo, s_lo, s_hi) * (1.0 / DIFF_QK_DIM) + NORM_EPS)
    return t * r * w


def _diff_prep_kernel(q_ref, k_ref, qw_ref, kw_ref, qo_ref, ko_ref):
    qscale = (DIFF_QK_DIM ** -0.5) * LOG2E
    for h in range(DIFF_HEADS):
        sl = slice(h * HEAD_W, (h + 1) * HEAD_W)
        qo_ref[:, sl] = (_half_rms(q_ref[:, sl].astype(F32), qw_ref[...]) * qscale).astype(BF16)
        ko_ref[:, sl] = _half_rms(k_ref[:, sl].astype(F32), kw_ref[...]).astype(BF16)


def _diff_prep(proj, qw2, kw2, col_q, col_k, tm):
    n = proj.shape[0]
    w = DIFF_HEADS * HEAD_W
    return pl.pallas_call(
        _diff_prep_kernel,
        grid=(n // tm,),
        in_specs=[pl.BlockSpec((tm, w), lambda i: (i, col_q // w)),
                  pl.BlockSpec((tm, w), lambda i: (i, col_k // w)),
                  pl.BlockSpec((1, HEAD_W), lambda i: (0, 0)),
                  pl.BlockSpec((1, HEAD_W), lambda i: (0, 0))],
        out_specs=[pl.BlockSpec((tm, w), lambda i: (i, 0)),
                   pl.BlockSpec((tm, w), lambda i: (i, 0))],
        out_shape=[jax.ShapeDtypeStruct((n, w), BF16), jax.ShapeDtypeStruct((n, w), BF16)],
        compiler_params=_cparams("parallel"),
        name="diff_prep",
    )(proj, proj, qw2, kw2)


def _rope_tile(t, cos_t, sin_lo, sin_hi):
    half = MLA_ROPE // 2
    return (t * cos_t + pltpu.roll(t, HEAD_W - half, 1) * sin_lo + pltpu.roll(t, half, 1) * sin_hi)


def _mla_prep_kernel(q_ref, kv_ref, pos_ref, qa_ref, qup_ref, kva_ref, wk_ref, wv_ref,
                     qn_ref, kn_ref, invf_ref, qo_ref, ko_ref, vo_ref):
    qk_dim = MLA_NOPE + MLA_ROPE
    qscale = (qk_dim ** -0.5) * LOG2E
    half = MLA_ROPE // 2

    xq = q_ref[...].astype(F32)
    rq = lax.rsqrt(jnp.mean(xq * xq, axis=-1, keepdims=True) + NORM_EPS)
    q = _mm(xq * rq * qa_ref[...], qup_ref[...])

    xkv = kv_ref[...].astype(F32)
    lane_kv = lax.broadcasted_iota(jnp.int32, xkv.shape, 1)
    is_lat = lane_kv < MLA_KV_LORA
    ssq = jnp.sum(jnp.where(is_lat, xkv * xkv, 0.0), axis=-1, keepdims=True)
    rkv = lax.rsqrt(ssq * (1.0 / MLA_KV_LORA) + NORM_EPS)
    lhs = jnp.where(is_lat, xkv * rkv * kva_ref[...], xkv).astype(BF16)
    k = jnp.dot(lhs, wk_ref[...], preferred_element_type=F32)
    vo_ref[...] = jnp.dot(lhs, wv_ref[...], preferred_element_type=F32).astype(BF16)

    ang = pos_ref[...].astype(F32) * invf_ref[...]
    lane = lax.broadcasted_iota(jnp.int32, ang.shape, 1)
    in_lo = (lane >= MLA_NOPE) & (lane < MLA_NOPE + half)
    in_hi = (lane >= MLA_NOPE + half) & (lane < qk_dim)
    cos_a, sin_a = jnp.cos(ang), jnp.sin(ang)
    cos_t = jnp.where(in_lo | in_hi, cos_a, 1.0)
    sin_lo = jnp.where(in_lo, -sin_a, 0.0)
    sin_hi = jnp.where(in_hi, sin_a, 0.0)

    for h in range(MLA_HEADS):
        sl = slice(h * HEAD_W, (h + 1) * HEAD_W)
        qh = q[:, sl]
        qh = qh * lax.rsqrt(jnp.sum(qh * qh, axis=-1, keepdims=True) * (1.0 / qk_dim) + NORM_EPS)
        qh = _rope_tile(qh * qn_ref[...], cos_t, sin_lo, sin_hi)
        qo_ref[:, sl] = (qh * qscale).astype(BF16)
        kh = k[:, sl]
        kh = kh * lax.rsqrt(jnp.sum(kh * kh, axis=-1, keepdims=True) * (1.0 / qk_dim) + NORM_EPS)
        kh = _rope_tile(kh * kn_ref[...], cos_t, sin_lo, sin_hi)
        ko_ref[:, sl] = kh.astype(BF16)


def _mla_prep(proj, pos2, qa_w, qup, kva_w, wk, wv, qn_w, kn_w, invf, col_q, col_kv, tm):
    n = proj.shape[0]
    w = MLA_HEADS * HEAD_W
    lat = MLA_Q_LORA
    const = lambda i: (0, 0)
    out = jax.ShapeDtypeStruct((n, w), BF16)
    return pl.pallas_call(
        _mla_prep_kernel,
        grid=(n // tm,),
        in_specs=[pl.BlockSpec((tm, lat), lambda i: (i, col_q // lat)),
                  pl.BlockSpec((tm, lat), lambda i: (i, col_kv // lat)),
                  pl.BlockSpec((tm, 1), lambda i: (i, 0)),
                  pl.BlockSpec((1, lat), const),
                  pl.BlockSpec((lat, w), const),
                  pl.BlockSpec((1, lat), const),
                  pl.BlockSpec((lat, w), const),
                  pl.BlockSpec((lat, w), const),
                  pl.BlockSpec((1, HEAD_W), const),
                  pl.BlockSpec((1, HEAD_W), const),
                  pl.BlockSpec((1, HEAD_W), const)],
        out_specs=[pl.BlockSpec((tm, w), lambda i: (i, 0))] * 3,
        out_shape=[out, out, out],
        compiler_params=_cparams("parallel"),
        name="mla_prep",
    )(proj, proj, pos2, qa_w, qup, kva_w, wk, wv, qn_w, kn_w, invf)


def _attn_kernel(*refs, tq, tk, rg, n_maps, lambda_init):
    if n_maps == 2:
        q_ref, k_ref, v_ref, lam_ref, sw_ref, o_ref, qq_ref, s_ref, cdiff_ref, mx_ref, m_ref, l_ref, acc_ref = refs
    else:
        q_ref, k_ref, v_ref, o_ref, s_ref, cdiff_ref, mx_ref, m_ref, l_ref, acc_ref = refs
        qq_ref = q_ref
    i = pl.program_id(2)
    rows = n_maps * tq
    if n_maps == 2:
        q = q_ref[...]
        lane = lax.broadcasted_iota(jnp.int32, q.shape, 1)
        zero = jnp.zeros_like(q)
        qq_ref[:tq, :] = jnp.where(lane < DIFF_QK_DIM, q, zero)
        qq_ref[tq:, :] = jnp.where(lane >= DIFF_QK_DIM, q, zero)

    m_ref[...] = jnp.full(m_ref.shape, NEG_BIG, F32)
    l_ref[...] = jnp.zeros(l_ref.shape, F32)
    acc_ref[...] = jnp.zeros(acc_ref.shape, F32)

    n_groups = rows // rg
    n_full = (i * tq) // tk
    n_diag = max(tq // tk, 1)

    cdiff_ref[...] = (lax.broadcasted_iota(jnp.int32, (rg, tk), 1) // CHUNK
                      - lax.broadcasted_iota(jnp.int32, (rg, tk), 0) // CHUNK)

    def produce(j, slot, g, diag):
        rsl = slice(g * rg, (g + 1) * rg)
        start = pl.multiple_of(j * tk, tk)
        s = lax.dot_general(qq_ref[rsl, :], k_ref[pl.ds(start, tk), :],
                            (((1,), (1,)), ((), ())), preferred_element_type=F32)
        if diag is not None:
            s = jnp.where(cdiff_ref[...] <= ((g * rg) % tq - diag * tk) // CHUNK, s, NEG_BIG)
        s_ref[slot, rsl, :] = s
        mx_ref[slot, rsl, :] = jnp.broadcast_to(jnp.max(s, axis=-1, keepdims=True), (rg, LANES))

    def consume(j, slot, g, v):
        rsl = slice(g * rg, (g + 1) * rg)
        m_prev = m_ref[rsl, :]
        m_new = jnp.maximum(m_prev, mx_ref[slot, rsl, :])
        alpha = jnp.exp2(m_prev - m_new)
        p = jnp.exp2(s_ref[slot, rsl, :] - jnp.tile(m_new, (1, tk // LANES)))
        p_sum = p[:, :LANES]
        for t in range(1, tk // LANES):
            p_sum = p_sum + p[:, t * LANES:(t + 1) * LANES]
        l_ref[rsl, :] = alpha * l_ref[rsl, :] + p_sum
        m_ref[rsl, :] = m_new
        acc_ref[rsl, :] = alpha * acc_ref[rsl, :] + jnp.dot(p.astype(BF16), v, preferred_element_type=F32)

    FULL = "full"

    def step(j, slot, nxt, cur_diag=None):
        start = pl.multiple_of(j * tk, tk)
        v = v_ref[pl.ds(start, tk), :]
        next_diag = None if nxt == FULL else nxt
        for g in range(n_groups):
            if nxt is not None and live(g, next_diag):
                produce(j + 1, 1 - slot, g, next_diag)
            if live(g, cur_diag):
                consume(j, slot, g, v)

    def live(g, diag):
        return diag is None or (g * rg) % tq + rg > diag * tk

    @pl.when(n_full == 0)
    def _():
        for g in range(n_groups):
            produce(0, 0, g, 0)

    @pl.when(n_full > 0)
    def _():
        for g in range(n_groups):
            produce(0, 0, g, None)

    def body(jj, carry):
        step(2 * jj, 0, FULL)
        step(2 * jj + 1, 1, FULL)
        return carry

    n_pairs = jnp.maximum((n_full - 1) // 2, 0)
    lax.fori_loop(0, n_pairs, body, 0)
    left = n_full - 2 * n_pairs

    def tail(first_slot):
        for d in range(n_diag):
            step(n_full + d, (first_slot + d) % 2, d + 1 if d < n_diag - 1 else None, cur_diag=d)

    @pl.when(left == 0)
    def _():
        tail(0)

    if tq % (2 * tk):
        @pl.when(left == 1)
        def _():
            step(n_full - 1, 0, 0)
            tail(1)

    @pl.when(left == 2)
    def _():
        step(n_full - 2, 0, FULL)
        step(n_full - 1, 1, 0)
        tail(0)

    o = acc_ref[...] / jnp.sum(l_ref[...], axis=-1, keepdims=True)
    if n_maps == 2:
        lp = lam_ref[...]
        lam = (jnp.exp(jnp.sum(lp[0:1] * lp[1:2], axis=-1, keepdims=True))
               - jnp.exp(jnp.sum(lp[2:3] * lp[3:4], axis=-1, keepdims=True)) + lambda_init)
        o = o[:tq] - lam * o[tq:]
        r = lax.rsqrt(jnp.mean(o * o, axis=-1, keepdims=True) + NORM_EPS)
        o = o * r * sw_ref[...] * (1.0 - lambda_init)
    o_ref[...] = o.astype(o_ref.dtype)


def _attention(q, k, v, extra, *, heads, batch, seq, tq, tk, rg, n_maps, v_col=0, lambda_init=0.0):
    q3, k3, v3 = (t.reshape(batch, seq, t.shape[1]) for t in (q, k, v))
    rows = n_maps * tq
    rg = min(rg, tq)
    vb = v_col // HEAD_W
    scratch = [pltpu.VMEM((2, rows, tk), F32), pltpu.VMEM((rg, tk), jnp.int32), pltpu.VMEM((2, rows, LANES), F32), pltpu.VMEM((rows, LANES), F32),
               pltpu.VMEM((rows, LANES), F32), pltpu.VMEM((rows, HEAD_W), F32)]
    if n_maps == 2:
        scratch.insert(0, pltpu.VMEM((rows, HEAD_W), BF16))
    in_specs = [pl.BlockSpec((None, tq, HEAD_W), lambda b, h, i: (b, i, h)),
                pl.BlockSpec((None, seq, HEAD_W), lambda b, h, i: (b, 0, h)),
                pl.BlockSpec((None, seq, HEAD_W), lambda b, h, i: (b, 0, vb + h))]
    for e in extra:
        in_specs.append(pl.BlockSpec(e.shape, lambda b, h, i: (0, 0)))
    out = pl.pallas_call(
        functools.partial(_attn_kernel, tq=tq, tk=tk, rg=rg, n_maps=n_maps, lambda_init=lambda_init),
        grid=(batch, heads, seq // tq),
        in_specs=in_specs,
        out_specs=pl.BlockSpec((None, tq, HEAD_W), lambda b, h, i: (b, i, h)),
        out_shape=jax.ShapeDtypeStruct((batch, seq, heads * HEAD_W), BF16),
        scratch_shapes=scratch,
        compiler_params=_cparams("parallel", "parallel", "arbitrary"),
        name="diff_attn" if n_maps == 2 else "mla_attn",
    )(q3, k3, v3, *extra)
    return out.reshape(batch * seq, heads * HEAD_W)


def _gdn_prep_kernel(x_ref, halo_ref, ba_ref, cw_ref, ea_ref, dtb_ref, qkv_ref, gb_ref, *, tiles_per_batch):
    i = pl.program_id(0)
    x = x_ref[...].astype(F32)
    halo = halo_ref[...].astype(F32)
    halo = jnp.where(i % tiles_per_batch == 0, 0.0, halo)
    cw = cw_ref[...]
    row8 = lax.broadcasted_iota(jnp.int32, halo.shape, 0)
    y = x * cw[GDN_CONV - 1:GDN_CONV]
    for s in range(1, GDN_CONV):
        xs = pltpu.roll(x, s, 0)
        hs = pltpu.roll(halo, s, 0)
        first = jnp.where(row8 < s, hs, xs[:SUBLANES])
        xs = jnp.concatenate([first, xs[SUBLANES:]], axis=0)
        y = y + xs * cw[GDN_CONV - 1 - s:GDN_CONV - s]
    y = y * jax.nn.sigmoid(y)
    nqk = 2 * GDN_HEADS
    for h in range(3 * GDN_HEADS):
        sl = slice(h * HEAD_W, (h + 1) * HEAD_W)
        t = y[:, sl]
        if h < nqk:
            t = t * lax.rsqrt(jnp.sum(t * t, axis=-1, keepdims=True) + NORM_EPS)
            if h < GDN_HEADS:
                t = t * (GDN_K_DIM ** -0.5)
        qkv_ref[:, sl] = t
    ba = ba_ref[...].astype(F32)
    lane = lax.broadcasted_iota(jnp.int32, ba.shape, 1)
    sp_in = ba + dtb_ref[...]
    softplus = jnp.maximum(sp_in, 0.0) + jnp.log1p(jnp.exp(-jnp.abs(sp_in)))
    gb_ref[...] = jnp.where(lane < GDN_HEADS, jax.nn.sigmoid(ba), -jnp.exp(ea_ref[...]) * softplus)


def _gdn_prep(proj, conv_w, exp_a, dt_b, col_qkv, col_ba, seq, tm):
    n = proj.shape[0]
    w = 3 * GDN_HEADS * HEAD_W
    tiles_per_batch = seq // tm
    hb = tm // SUBLANES
    return pl.pallas_call(
        functools.partial(_gdn_prep_kernel, tiles_per_batch=tiles_per_batch),
        grid=(n // tm,),
        in_specs=[pl.BlockSpec((tm, w), lambda i: (i, col_qkv // w)),
                  pl.BlockSpec((SUBLANES, w), lambda i: (jnp.maximum(i * hb - 1, 0), col_qkv // w)),
                  pl.BlockSpec((tm, LANES), lambda i: (i, col_ba // LANES)),
                  pl.BlockSpec((GDN_CONV, w), lambda i: (0, 0)),
                  pl.BlockSpec((1, LANES), lambda i: (0, 0)),
                  pl.BlockSpec((1, LANES), lambda i: (0, 0))],
        out_specs=[pl.BlockSpec((tm, w), lambda i: (i, 0)),
                   pl.BlockSpec((tm, LANES), lambda i: (i, 0))],
        out_shape=[jax.ShapeDtypeStruct((n, w), F32), jax.ShapeDtypeStruct((n, LANES), F32)],
        compiler_params=_cparams("parallel"),
        name="gdn_prep",
    )(proj, proj, proj, conv_w, exp_a, dt_b)


def _gdn_intra_kernel(qkv_ref, gb_ref, u_ref, w_ref, qt_ref, ot_ref, kd_ref, eg_ref, *, tc):
    hw = GDN_HEADS * HEAD_W
    nc = tc // CHUNK
    gb = gb_ref[...]
    r_in_chunk = lax.broadcasted_iota(jnp.int32, gb.shape, 0) % CHUNK
    gc = gb
    sh = 1
    while sh < CHUNK:
        gc = gc + jnp.where(r_in_chunk >= sh, pltpu.roll(gc, sh, 0), 0.0)
        sh *= 2
    gc_t = gc.T

    row = lax.broadcasted_iota(jnp.int32, (CHUNK, CHUNK), 0)
    col = lax.broadcasted_iota(jnp.int32, (CHUNK, CHUNK), 1)
    incl = row >= col
    strict = row > col
    eye = (row == col).astype(F32)

    pairs = [(c, h) for c in range(nc) for h in range(GDN_HEADS)]
    st = {}
    for c, h in pairs:
        rs = slice(c * CHUNK, (c + 1) * CHUNK)
        hs = slice(h * HEAD_W, (h + 1) * HEAD_W)
        q = qkv_ref[rs, h * HEAD_W:(h + 1) * HEAD_W]
        k = qkv_ref[rs, hw + h * HEAD_W:hw + (h + 1) * HEAD_W]
        v = qkv_ref[rs, 2 * hw + h * HEAD_W:2 * hw + (h + 1) * HEAD_W]
        beta = gb[rs, h:h + 1]
        g_col = gc[rs, GDN_HEADS + h:GDN_HEADS + h + 1]
        g_row = gc_t[GDN_HEADS + h:GDN_HEADS + h + 1, rs]
        g_last = g_col[CHUNK - 1:CHUNK, :]
        k_beta = k * beta
        e_g = jnp.exp(g_col)
        kd_ref[rs, hs] = k * jnp.exp(g_last - g_col)
        eg_ref[c:c + 1, hs] = jnp.broadcast_to(jnp.exp(g_last), (1, HEAD_W))
        st[c, h] = dict(
            kb16=k_beta.astype(BF16), k16=k.astype(BF16), q16=q.astype(BF16),
            decay=jnp.exp(jnp.where(incl, g_col - g_row, NEG_BIG)),
            rhs=jnp.concatenate([v * beta, k_beta * e_g], axis=1).astype(BF16),
            qg=q * e_g)
    for p in pairs:
        s = st[p]
        s["lmat"] = jnp.where(strict, _mm_nt(s["kb16"], s["k16"]) * s["decay"], 0.0)
        s["attn"] = jnp.where(incl, _mm_nt(s["q16"], s["k16"]) * s["decay"], 0.0).astype(BF16)
        s["t"] = eye - jnp.where((row // 2) == (col // 2), s["lmat"], 0.0)
    b = 2
    while b < CHUNK:
        off = ((row // (2 * b)) == (col // (2 * b))) & ((row // b) % 2 == 1) & ((col // b) % 2 == 0)
        for p in pairs:
            s = st[p]
            s["ct"] = _mm(jnp.where(off, s["lmat"], 0.0), s["t"])
        for p in pairs:
            s = st[p]
            s["t"] = s["t"] - _mm(s["t"], s["ct"])
        b *= 2
    for p in pairs:
        s = st[p]
        s["uw"] = _mm(s["t"], s["rhs"])
    for c, h in pairs:
        s = st[c, h]
        rs = slice(c * CHUNK, (c + 1) * CHUNK)
        hs = slice(h * HEAD_W, (h + 1) * HEAD_W)
        a_uw = _mm(s["attn"], s["uw"])
        u_ref[rs, hs] = s["uw"][:, :HEAD_W]
        w_ref[rs, hs] = s["uw"][:, HEAD_W:].astype(BF16)
        ot_ref[rs, hs] = a_uw[:, :HEAD_W]
        qt_ref[rs, hs] = (s["qg"] - a_uw[:, HEAD_W:]).astype(BF16)


def _gdn_intra(qkv, gb, tc):
    n = qkv.shape[0]
    hw = GDN_HEADS * HEAD_W
    nc = tc // CHUNK
    blk = pl.BlockSpec((tc, hw), lambda i: (i, 0))
    return pl.pallas_call(
        functools.partial(_gdn_intra_kernel, tc=tc),
        grid=(n // tc,),
        in_specs=[pl.BlockSpec((tc, 3 * hw), lambda i: (i, 0)),
                  pl.BlockSpec((tc, LANES), lambda i: (i, 0))],
        out_specs=[blk, blk, blk, blk, blk, pl.BlockSpec((nc, hw), lambda i: (i, 0))],
        out_shape=[jax.ShapeDtypeStruct((n, hw), F32), jax.ShapeDtypeStruct((n, hw), BF16),
                   jax.ShapeDtypeStruct((n, hw), BF16), jax.ShapeDtypeStruct((n, hw), F32),
                   jax.ShapeDtypeStruct((n, hw), F32), jax.ShapeDtypeStruct((n // CHUNK, hw), F32)],
        compiler_params=_cparams("parallel"),
        name="gdn_intra",
    )(qkv, gb)


def _gdn_scan_kernel(u_ref, w_ref, qt_ref, ot_ref, kd_ref, eg_ref, z_ref, ow_ref, o_ref, state_ref, *, tc, batch):
    @pl.when(pl.program_id(0) == 0)
    def _():
        state_ref[...] = jnp.zeros(state_ref.shape, F32)

    chains = [(b, h) for b in range(batch) for h in range(GDN_HEADS)]
    for c in range(tc // CHUNK):
        rs = slice(c * CHUNK, (c + 1) * CHUNK)
        prod, v_new = {}, {}
        for b, h in chains:
            hs = slice(h * HEAD_W, (h + 1) * HEAD_W)
            lhs = jnp.concatenate([w_ref[b, rs, hs], qt_ref[b, rs, hs]], axis=0)
            prod[b, h] = jnp.dot(lhs, state_ref[b * GDN_HEADS + h].astype(BF16), preferred_element_type=F32)
        for b, h in chains:
            hs = slice(h * HEAD_W, (h + 1) * HEAD_W)
            v_new[b, h] = (u_ref[b, rs, hs] - prod[b, h][:CHUNK]).astype(BF16)
        for b, h in chains:
            hs = slice(h * HEAD_W, (h + 1) * HEAD_W)
            idx = b * GDN_HEADS + h
            k_t = kd_ref[b, rs, hs].T.astype(BF16)
            state_ref[idx] = (state_ref[idx] * eg_ref[b, c:c + 1, hs]
                              + jnp.dot(k_t, v_new[b, h], preferred_element_type=F32))
        for b, h in chains:
            hs = slice(h * HEAD_W, (h + 1) * HEAD_W)
            o = prod[b, h][CHUNK:] + ot_ref[b, rs, hs]
            o = o * lax.rsqrt(jnp.mean(o * o, axis=-1, keepdims=True) + NORM_EPS) * ow_ref[...]
            z = z_ref[b, rs, hs].astype(F32)
            o_ref[b, rs, hs] = (o * (z * jax.nn.sigmoid(z))).astype(o_ref.dtype)


def _gdn_scan(u, w, qt, ot, kd, eg, proj, onorm_w, col_z, batch, seq, tc):
    hw = GDN_HEADS * HEAD_W
    nc = tc // CHUNK
    r3 = lambda t: t.reshape(batch, t.shape[0] // batch, t.shape[1])
    blk = pl.BlockSpec((batch, tc, hw), lambda i: (0, i, 0))
    out = pl.pallas_call(
        functools.partial(_gdn_scan_kernel, tc=tc, batch=batch),
        grid=(seq // tc,),
        in_specs=[blk, blk, blk, blk, blk,
                  pl.BlockSpec((batch, nc, hw), lambda i: (0, i, 0)),
                  pl.BlockSpec((batch, tc, hw), lambda i: (0, i, col_z // hw)),
                  pl.BlockSpec((1, HEAD_W), lambda i: (0, 0))],
        out_specs=blk,
        out_shape=jax.ShapeDtypeStruct((batch, seq, hw), BF16),
        scratch_shapes=[pltpu.VMEM((batch * GDN_HEADS, GDN_K_DIM, GDN_V_DIM), F32)],
        compiler_params=_cparams("arbitrary"),
        name="gdn_scan",
    )(r3(u), r3(w), r3(qt), r3(ot), r3(kd), r3(eg), r3(proj), onorm_w)
    return out.reshape(batch * seq, hw)


def _merge_kernel(x_ref, oa_ref, ob_ref, oc_ref, gate_ref, wa_ref, wb_ref, wc_ref, wout_ref, mod_ref, o_ref):
    d = x_ref.shape[1]
    merged = None
    for idx, (o_r, w_r) in enumerate(((oa_ref, wa_ref), (ob_ref, wb_ref), (oc_ref, wc_ref))):
        y = jnp.dot(o_r[...], w_r[...], preferred_element_type=F32)
        g = jax.nn.sigmoid(gate_ref[:, idx * d:(idx + 1) * d].astype(F32))
        merged = g * y if merged is None else merged + g * y
    upd = jnp.dot(merged.astype(BF16), wout_ref[...], preferred_element_type=F32)
    o_ref[...] = x_ref[...] + mod_ref[2:3, :] * upd


def _merge(x2, oa, ob, oc, proj, wa, wb, wc, wout, mod, col_gate, seq, tm):
    n, d = x2.shape
    w = DIFF_HEADS * HEAD_W
    tiles_per_batch = seq // tm
    row = lambda i: (i, 0)
    const = lambda i: (0, 0)
    return pl.pallas_call(
        _merge_kernel,
        grid=(n // tm,),
        in_specs=[pl.BlockSpec((tm, d), row), pl.BlockSpec((tm, w), row), pl.BlockSpec((tm, w), row),
                  pl.BlockSpec((tm, w), row),
                  pl.BlockSpec((tm, N_BRANCHES * d), lambda i: (i, col_gate // (N_BRANCHES * d))),
                  pl.BlockSpec((w, d), const), pl.BlockSpec((w, d), const), pl.BlockSpec((w, d), const),
                  pl.BlockSpec((d, d), const),
                  pl.BlockSpec((None, 6, d), lambda i: (i // tiles_per_batch, 0, 0))],
        out_specs=pl.BlockSpec((tm, d), row),
        out_shape=jax.ShapeDtypeStruct((n, d), F32),
        compiler_params=_cparams("parallel"),
        name="merge",
    )(x2, oa, ob, oc, proj, wa, wb, wc, wout, mod)


def _ffn_kernel(x_ref, nw_ref, mod_ref, wg_ref, wu_ref, wd_ref, o_ref, h_ref, acc_ref):
    j = pl.program_id(1)

    @pl.when(j == 0)
    def _():
        h_ref[...] = _norm_mod(x_ref[...], nw_ref[...], mod_ref[...], 3, 4).astype(BF16)
        acc_ref[...] = jnp.zeros(acc_ref.shape, F32)

    h = h_ref[...]
    gate = jnp.dot(h, wg_ref[...], preferred_element_type=F32)
    up = jnp.dot(h, wu_ref[...], preferred_element_type=F32)
    act = (gate * jax.nn.sigmoid(gate) * up).astype(BF16)
    acc_ref[...] += jnp.dot(act, wd_ref[...], preferred_element_type=F32)

    @pl.when(j == pl.num_programs(1) - 1)
    def _():
        o_ref[...] = x_ref[...] + mod_ref[5:6, :] * acc_ref[...]


def _ffn(x2, nw, mod, w_gu, w_down, seq, tm, th):
    n, d = x2.shape
    hidden = w_down.shape[0]
    nh = hidden // th
    tiles_per_batch = seq // tm
    return pl.pallas_call(
        _ffn_kernel,
        grid=(n // tm, nh),
        in_specs=[pl.BlockSpec((tm, d), lambda i, j: (i, 0)),
                  pl.BlockSpec((1, d), lambda i, j: (0, 0)),
                  pl.BlockSpec((None, 6, d), lambda i, j: (i // tiles_per_batch, 0, 0)),
                  pl.BlockSpec((d, th), lambda i, j: (0, j)),
                  pl.BlockSpec((d, th), lambda i, j: (0, nh + j)),
                  pl.BlockSpec((th, d), lambda i, j: (j, 0))],
        out_specs=pl.BlockSpec((tm, d), lambda i, j: (i, 0)),
        out_shape=jax.ShapeDtypeStruct((n, d), F32),
        scratch_shapes=[pltpu.VMEM((tm, d), BF16), pltpu.VMEM((tm, d), F32)],
        compiler_params=_cparams("parallel", "arbitrary"),
        name="ffn",
    )(x2, nw, mod, w_gu, w_gu, w_down)


def _pad_cols(w, width):
    return jnp.pad(w, ((0, 0), (0, width - w.shape[1])))


def _in_proj_layout(d_model):
    a = DIFF_HEADS * HEAD_W
    g = GDN_HEADS * HEAD_W
    cols = {}
    off = 0
    for name, width in (("gate", N_BRANCHES * d_model), ("c_qkv", 3 * g), ("a_q", a), ("a_k", a), ("a_v", a),
                        ("c_z", g), ("c_ba", LANES), ("b_q", MLA_Q_LORA), ("b_kv", MLA_Q_LORA)):
        if off % width:
            off += width - off % width
        cols[name] = off
        off += width
    return cols, off


def _build_w_in(w_in, d_model, total):
    a = DIFF_HEADS * 2 * DIFF_QK_DIM
    g = GDN_HEADS * GDN_K_DIM
    widths = (a, a, DIFF_HEADS * DIFF_V_DIM, MLA_Q_LORA, MLA_KV_LORA + MLA_ROPE, g, g, GDN_HEADS * GDN_V_DIM,
              GDN_HEADS * GDN_V_DIM, GDN_HEADS, GDN_HEADS, N_BRANCHES * d_model)
    offs = np.concatenate([[0], np.cumsum(widths)])
    part = [w_in[:, int(offs[i]):int(offs[i + 1])] for i in range(len(widths))]
    cols, _ = _in_proj_layout(d_model)
    out = jnp.zeros((w_in.shape[0], total), w_in.dtype)
    place = (("a_q", part[0]), ("a_k", part[1]), ("a_v", part[2]), ("b_q", part[3]), ("b_kv", part[4]),
             ("c_qkv", jnp.concatenate(part[5:8], axis=1)), ("c_z", part[8]), ("gate", part[11]),
             ("c_ba", jnp.concatenate(part[9:11], axis=1)))
    for name, p in place:
        out = lax.dynamic_update_slice(out, p, (0, cols[name]))
    return out.astype(BF16)


def _head_pad(w, heads, per_head, lane_off=0):
    r = w.shape[0]
    w = w.reshape(r, heads, per_head)
    w = jnp.pad(w, ((0, 0), (0, 0), (lane_off, HEAD_W - per_head - lane_off)))
    return w.reshape(r, heads * HEAD_W)


def kernel(x, c, positions, ada_w, ada_b, norm1_w, w_in, diff_qnorm_w, diff_knorm_w, diff_lambda, diff_subln_w, w_o_diff, mla_qa_norm_w, mla_q_up, mla_kva_norm_w, mla_kv_up, mla_qnorm_w, mla_knorm_w, w_o_mla, gdn_conv_w, gdn_a_log, gdn_dt_bias, gdn_onorm_w, w_o_gdn, w_out, norm2_w, ffn_w_gu, ffn_w_down):
    batch, seq, d = x.shape
    depth = ada_w.shape[0]
    n = batch * seq
    qk_dim = MLA_NOPE + MLA_ROPE

    tm = min(1024, seq)
    tq = min(512, seq)
    tc = min(512, seq)

    c_pad = jnp.pad(c, ((0, SUBLANES - batch), (0, 0)))
    mod_all = _ada_mod(c_pad, ada_w, ada_b)[:, :batch].reshape(depth, batch, 6, d)

    cols, total = _in_proj_layout(d)
    tn = 512
    total = -(-total // tn) * tn

    pos2 = positions.reshape(n, 1)
    half = MLA_ROPE // 2
    inv_freq = ROPE_THETA ** (-jnp.arange(half, dtype=F32) / half)
    invf = jnp.zeros((1, HEAD_W), F32).at[0, MLA_NOPE:qk_dim].set(jnp.concatenate([inv_freq, inv_freq]))

    x2 = x.reshape(n, d)
    for l in range(depth):
        lambda_init = 0.8 - 0.6 * math.exp(-0.3 * l)
        mod = mod_all[l]
        proj = _in_proj(x2, norm1_w[l][None], mod, _build_w_in(w_in[l], d, total), seq, tm, tn)

        qw2 = jnp.tile(diff_qnorm_w[l], 2)[None]
        kw2 = jnp.tile(diff_knorm_w[l], 2)[None]
        a_q, a_k = _diff_prep(proj, qw2, kw2, cols["a_q"], cols["a_k"], tm)
        o_a = _attention(a_q, a_k, proj, (diff_lambda[l], diff_subln_w[l][None]), heads=DIFF_HEADS, batch=batch,
                         seq=seq, tq=min(2 * tq, seq), tk=tq, rg=256, n_maps=2, v_col=cols["a_v"],
                         lambda_init=lambda_init)

        kv_up = mla_kv_up[l].reshape(MLA_KV_LORA, MLA_HEADS, MLA_NOPE + MLA_V_DIM)
        wk_nope = _head_pad(kv_up[:, :, :MLA_NOPE].reshape(MLA_KV_LORA, -1), MLA_HEADS, MLA_NOPE)
        place_rope = _head_pad(jnp.tile(jnp.eye(MLA_ROPE, dtype=F32), (1, MLA_HEADS)), MLA_HEADS, MLA_ROPE, MLA_NOPE)
        pad_rows = MLA_Q_LORA - MLA_KV_LORA - MLA_ROPE
        wk = jnp.concatenate([wk_nope, place_rope, jnp.zeros((pad_rows, MLA_HEADS * HEAD_W), F32)], axis=0)
        wv = jnp.pad(kv_up[:, :, MLA_NOPE:].reshape(MLA_KV_LORA, -1), ((0, MLA_Q_LORA - MLA_KV_LORA), (0, 0)))
        kva_w = jnp.pad(mla_kva_norm_w[l], (0, MLA_Q_LORA - MLA_KV_LORA))[None]
        b_q, b_k, b_v = _mla_prep(
            proj, pos2, mla_qa_norm_w[l][None], _head_pad(mla_q_up[l], MLA_HEADS, qk_dim).astype(BF16),
            kva_w, wk.astype(BF16), wv.astype(BF16),
            jnp.pad(mla_qnorm_w[l], (0, HEAD_W - qk_dim))[None], jnp.pad(mla_knorm_w[l], (0, HEAD_W - qk_dim))[None],
            invf, cols["b_q"], cols["b_kv"], tm)
        o_b = _attention(b_q, b_k, b_v, (), heads=MLA_HEADS, batch=batch, seq=seq, tq=min(4 * tq, seq), tk=tq,
                         rg=256, n_maps=1)

        exp_a = jnp.zeros((1, LANES), F32).at[0, GDN_HEADS:2 * GDN_HEADS].set(gdn_a_log[l])
        dt_b = jnp.zeros((1, LANES), F32).at[0, GDN_HEADS:2 * GDN_HEADS].set(gdn_dt_bias[l])
        c_qkv, c_gb = _gdn_prep(proj, gdn_conv_w[l], exp_a, dt_b, cols["c_qkv"], cols["c_ba"], seq, tm)
        g_u, g_w, g_qt, g_ot, g_kd, g_eg = _gdn_intra(c_qkv, c_gb, tc)
        o_c = _gdn_scan(g_u, g_w, g_qt, g_ot, g_kd, g_eg, proj, gdn_onorm_w[l][None], cols["c_z"], batch, seq, tc)

        x2 = _merge(x2, o_a, o_b, o_c, proj, w_o_diff[l].astype(BF16), w_o_mla[l].astype(BF16),
                    w_o_gdn[l].astype(BF16), w_out[l].astype(BF16), mod, cols["gate"], seq, min(512, seq))
        x2 = _ffn(x2, norm2_w[l][None], mod, ffn_w_gu[l].astype(BF16), ffn_w_down[l].astype(BF16), seq, tm, 256)
    return x2.reshape(batch, seq, d)
```

```python
import functools
import math

import jax
import jax.numpy as jnp
import numpy as np
from jax import lax
from jax.experimental import pallas as pl
from jax.experimental.pallas import tpu as pltpu

F32 = jnp.float32
BF16 = jnp.bfloat16

CHUNK = 64
NORM_EPS = 1e-6
ROPE_THETA = 10000.0
DIFF_HEADS = 4
DIFF_QK_DIM = 64
DIFF_V_DIM = 128
MLA_HEADS = 4
MLA_Q_LORA = 384
MLA_KV_LORA = 256
MLA_NOPE = 64
MLA_ROPE = 32
MLA_V_DIM = 128
GDN_HEADS = 4
GDN_K_DIM = 128
GDN_V_DIM = 128
GDN_CONV = 4
N_BRANCHES = 3

LANES = 128
SUBLANES = 8
HEAD_W = 128
LOG2E = 1.4426950408889634
NEG_BIG = -1e30
VMEM_LIMIT = 56 * 1024 * 1024
ATTN_UNROLL = 4


def _cparams(*sem):
    return pltpu.CompilerParams(dimension_semantics=sem, vmem_limit_bytes=VMEM_LIMIT)


def _mm(a, b):
    return jnp.dot(a.astype(BF16), b.astype(BF16), preferred_element_type=F32)


def _mm_nt(a, b):
    return lax.dot_general(a.astype(BF16), b.astype(BF16), (((1,), (1,)), ((), ())),
                           preferred_element_type=F32)


def _mod_kernel(c_ref, w_ref, b_ref, o_ref):
    o_ref[...] = jnp.dot(c_ref[...], w_ref[...], preferred_element_type=F32) + b_ref[...]


def _ada_mod(c_pad, ada_w, ada_b):
    depth, d, n6 = ada_w.shape
    tn = 1536
    return pl.pallas_call(
        _mod_kernel,
        grid=(depth, n6 // tn),
        in_specs=[pl.BlockSpec((SUBLANES, d), lambda l, j: (0, 0)),
                  pl.BlockSpec((None, d, tn), lambda l, j: (l, 0, j)),
                  pl.BlockSpec((None, 1, tn), lambda l, j: (l, 0, j))],
        out_specs=pl.BlockSpec((None, SUBLANES, tn), lambda l, j: (l, 0, j)),
        out_shape=jax.ShapeDtypeStruct((depth, SUBLANES, n6), F32),
        compiler_params=_cparams("parallel", "parallel"),
        name="ada_mod",
    )(c_pad, ada_w, ada_b.reshape(depth, 1, n6))


def _norm_mod(x, nw, mod, shift_row, scale_row):
    r = lax.rsqrt(jnp.mean(x * x, axis=-1, keepdims=True) + NORM_EPS)
    h = x * r * nw
    return h * (1.0 + mod[scale_row:scale_row + 1, :]) + mod[shift_row:shift_row + 1, :]


def _in_proj_kernel(x_ref, nw_ref, mod_ref, w_ref, o_ref, h_ref):
    @pl.when(pl.program_id(1) == 0)
    def _():
        h_ref[...] = _norm_mod(x_ref[...], nw_ref[...], mod_ref[...], 0, 1).astype(BF16)

    o_ref[...] = jnp.dot(h_ref[...], w_ref[...], preferred_element_type=F32).astype(o_ref.dtype)


def _in_proj(x2, nw, mod, w, seq, tm, tn):
    n, d = x2.shape
    cols = w.shape[1]
    tiles_per_batch = seq // tm
    return pl.pallas_call(
        _in_proj_kernel,
        grid=(n // tm, cols // tn),
        in_specs=[pl.BlockSpec((tm, d), lambda i, j: (i, 0)),
                  pl.BlockSpec((1, d), lambda i, j: (0, 0)),
                  pl.BlockSpec((None, 6, d), lambda i, j: (i // tiles_per_batch, 0, 0)),
                  pl.BlockSpec((d, tn), lambda i, j: (0, j))],
        out_specs=pl.BlockSpec((tm, tn), lambda i, j: (i, j)),
        out_shape=jax.ShapeDtypeStruct((n, cols), BF16),
        scratch_shapes=[pltpu.VMEM((tm, d), BF16)],
        compiler_params=_cparams("parallel", "arbitrary"),
        name="in_proj",
    )(x2, nw, mod, w)


def _half_rms(t, w):
    lane = lax.broadcasted_iota(jnp.int32, t.shape, 1)
    lo = lane < DIFF_QK_DIM
    sq = t * t
    s_lo = jnp.sum(jnp.where(lo, sq, 0.0), axis=-1, keepdims=True)
    s_hi = jnp.sum(jnp.where(lo, 0.0, sq), axis=-1, keepdims=True)
    r = lax.rsqrt(jnp.where(lo, s_lo, s_hi) * (1.0 / DIFF_QK_DIM) + NORM_EPS)
    return t * r * w


def _diff_prep_kernel(q_ref, k_ref, qw_ref, kw_ref, qo_ref, ko_ref):
    qscale = (DIFF_QK_DIM ** -0.5) * LOG2E
    for h in range(DIFF_HEADS):
        sl = slice(h * HEAD_W, (h + 1) * HEAD_W)
        qo_ref[:, sl] = (_half_rms(q_ref[:, sl].astype(F32), qw_ref[...]) * qscale).astype(BF16)
        ko_ref[:, sl] = _half_rms(k_ref[:, sl].astype(F32), kw_ref[...]).astype(BF16)


def _diff_prep(proj, qw2, kw2, col_q, col_k, tm):
    n = proj.shape[0]
    w = DIFF_HEADS * HEAD_W
    return pl.pallas_call(
        _diff_prep_kernel,
        grid=(n // tm,),
        in_specs=[pl.BlockSpec((tm, w), lambda i: (i, col_q // w)),
                  pl.BlockSpec((tm, w), lambda i: (i, col_k // w)),
                  pl.BlockSpec((1, HEAD_W), lambda i: (0, 0)),
                  pl.BlockSpec((1, HEAD_W), lambda i: (0, 0))],
        out_specs=[pl.BlockSpec((tm, w), lambda i: (i, 0)),
                   pl.BlockSpec((tm, w), lambda i: (i, 0))],
        out_shape=[jax.ShapeDtypeStruct((n, w), BF16), jax.ShapeDtypeStruct((n, w), BF16)],
        compiler_params=_cparams("parallel"),
        name="diff_prep",
    )(proj, proj, qw2, kw2)


def _rope_tile(t, cos_t, sin_lo, sin_hi):
    half = MLA_ROPE // 2
    return (t * cos_t + pltpu.roll(t, HEAD_W - half, 1) * sin_lo + pltpu.roll(t, half, 1) * sin_hi)


def _mla_prep_kernel(q_ref, kv_ref, pos_ref, qa_ref, qup_ref, kva_ref, wk_ref, wv_ref,
                     qn_ref, kn_ref, invf_ref, qo_ref, ko_ref, vo_ref):
    qk_dim = MLA_NOPE + MLA_ROPE
    qscale = (qk_dim ** -0.5) * LOG2E
    half = MLA_ROPE // 2

    xq = q_ref[...].astype(F32)
    rq = lax.rsqrt(jnp.mean(xq * xq, axis=-1, keepdims=True) + NORM_EPS)
    q = _mm(xq * rq * qa_ref[...], qup_ref[...])

    xkv = kv_ref[...].astype(F32)
    lane_kv = lax.broadcasted_iota(jnp.int32, xkv.shape, 1)
    is_lat = lane_kv < MLA_KV_LORA
    ssq = jnp.sum(jnp.where(is_lat, xkv * xkv, 0.0), axis=-1, keepdims=True)
    rkv = lax.rsqrt(ssq * (1.0 / MLA_KV_LORA) + NORM_EPS)
    lhs = jnp.where(is_lat, xkv * rkv * kva_ref[...], xkv).astype(BF16)
    k = jnp.dot(lhs, wk_ref[...], preferred_element_type=F32)
    vo_ref[...] = jnp.dot(lhs, wv_ref[...], preferred_element_type=F32).astype(BF16)

    ang = pos_ref[...].astype(F32) * invf_ref[...]
    lane = lax.broadcasted_iota(jnp.int32, ang.shape, 1)
    in_lo = (lane >= MLA_NOPE) & (lane < MLA_NOPE + half)
    in_hi = (lane >= MLA_NOPE + half) & (lane < qk_dim)
    cos_a, sin_a = jnp.cos(ang), jnp.sin(ang)
    cos_t = jnp.where(in_lo | in_hi, cos_a, 1.0)
    sin_lo = jnp.where(in_lo, -sin_a, 0.0)
    sin_hi = jnp.where(in_hi, sin_a, 0.0)

    for h in range(MLA_HEADS):
        sl = slice(h * HEAD_W, (h + 1) * HEAD_W)
        qh = q[:, sl]
        qh = qh * lax.rsqrt(jnp.sum(qh * qh, axis=-1, keepdims=True) * (1.0 / qk_dim) + NORM_EPS)
        qh = _rope_tile(qh * qn_ref[...], cos_t, sin_lo, sin_hi)
        qo_ref[:, sl] = (qh * qscale).astype(BF16)
        kh = k[:, sl]
        kh = kh * lax.rsqrt(jnp.sum(kh * kh, axis=-1, keepdims=True) * (1.0 / qk_dim) + NORM_EPS)
        kh = _rope_tile(kh * kn_ref[...], cos_t, sin_lo, sin_hi)
        ko_ref[:, sl] = kh.astype(BF16)


def _mla_prep(proj, pos2, qa_w, qup, kva_w, wk, wv, qn_w, kn_w, invf, col_q, col_kv, tm):
    n = proj.shape[0]
    w = MLA_HEADS * HEAD_W
    lat = MLA_Q_LORA
    const = lambda i: (0, 0)
    out = jax.ShapeDtypeStruct((n, w), BF16)
    return pl.pallas_call(
        _mla_prep_kernel,
        grid=(n // tm,),
        in_specs=[pl.BlockSpec((tm, lat), lambda i: (i, col_q // lat)),
                  pl.BlockSpec((tm, lat), lambda i: (i, col_kv // lat)),
                  pl.BlockSpec((tm, 1), lambda i: (i, 0)),
                  pl.BlockSpec((1, lat), const),
                  pl.BlockSpec((lat, w), const),
                  pl.BlockSpec((1, lat), const),
                  pl.BlockSpec((lat, w), const),
                  pl.BlockSpec((lat, w), const),
                  pl.BlockSpec((1, HEAD_W), const),
                  pl.BlockSpec((1, HEAD_W), const),
                  pl.BlockSpec((1, HEAD_W), const)],
        out_specs=[pl.BlockSpec((tm, w), lambda i: (i, 0))] * 3,
        out_shape=[out, out, out],
        compiler_params=_cparams("parallel"),
        name="mla_prep",
    )(proj, proj, pos2, qa_w, qup, kva_w, wk, wv, qn_w, kn_w, invf)


def _attn_kernel(*refs, tq, tk, rg, n_maps, lambda_init, unroll):
    if n_maps == 2:
        q_ref, k_ref, v_ref, lam_ref, sw_ref, o_ref, qq_ref, s_ref, cdiff_ref, mx_ref, m_ref, l_ref, acc_ref = refs
    else:
        q_ref, k_ref, v_ref, o_ref, s_ref, cdiff_ref, mx_ref, m_ref, l_ref, acc_ref = refs
        qq_ref = q_ref
    i = pl.program_id(2)
    rows = n_maps * tq
    if n_maps == 2:
        q = q_ref[...]
        lane = lax.broadcasted_iota(jnp.int32, q.shape, 1)
        zero = jnp.zeros_like(q)
        qq_ref[:tq, :] = jnp.where(lane < DIFF_QK_DIM, q, zero)
        qq_ref[tq:, :] = jnp.where(lane >= DIFF_QK_DIM, q, zero)

    m_ref[...] = jnp.full(m_ref.shape, NEG_BIG, F32)
    l_ref[...] = jnp.zeros(l_ref.shape, F32)
    acc_ref[...] = jnp.zeros(acc_ref.shape, F32)

    n_groups = rows // rg
    n_full = (i * tq) // tk
    n_diag = max(tq // tk, 1)

    cdiff_ref[...] = (lax.broadcasted_iota(jnp.int32, (rg, tk), 1) // CHUNK
                      - lax.broadcasted_iota(jnp.int32, (rg, tk), 0) // CHUNK)

    def produce(j, slot, g, diag):
        rsl = slice(g * rg, (g + 1) * rg)
        start = pl.multiple_of(j * tk, tk)
        s = lax.dot_general(qq_ref[rsl, :], k_ref[pl.ds(start, tk), :],
                            (((1,), (1,)), ((), ())), preferred_element_type=F32)
        if diag is not None:
            s = jnp.where(cdiff_ref[...] <= ((g * rg) % tq - diag * tk) // CHUNK, s, NEG_BIG)
        s_ref[slot, rsl, :] = s
        mx_ref[slot, rsl, :] = jnp.broadcast_to(jnp.max(s, axis=-1, keepdims=True), (rg, LANES))

    def consume(j, slot, g, v):
        rsl = slice(g * rg, (g + 1) * rg)
        m_prev = m_ref[rsl, :]
        m_new = jnp.maximum(m_prev, mx_ref[slot, rsl, :])
        alpha = jnp.exp2(m_prev - m_new)
        p = jnp.exp2(s_ref[slot, rsl, :] - jnp.tile(m_new, (1, tk // LANES)))
        p_sum = p[:, :LANES]
        for t in range(1, tk // LANES):
            p_sum = p_sum + p[:, t * LANES:(t + 1) * LANES]
        l_ref[rsl, :] = alpha * l_ref[rsl, :] + p_sum
        m_ref[rsl, :] = m_new
        acc_ref[rsl, :] = alpha * acc_ref[rsl, :] + jnp.dot(p.astype(BF16), v, preferred_element_type=F32)

    FULL = "full"

    def step(j, slot, nxt, cur_diag=None):
        start = pl.multiple_of(j * tk, tk)
        v = v_ref[pl.ds(start, tk), :]
        next_diag = None if nxt == FULL else nxt
        for g in range(n_groups):
            if nxt is not None and live(g, next_diag):
                produce(j + 1, 1 - slot, g, next_diag)
            if live(g, cur_diag):
                consume(j, slot, g, v)

    def live(g, diag):
        return diag is None or (g * rg) % tq + rg > diag * tk

    @pl.when(n_full == 0)
    def _():
        for g in range(n_groups):
            produce(0, 0, g, 0)

    @pl.when(n_full > 0)
    def _():
        for g in range(n_groups):
            produce(0, 0, g, None)

    def body(jj, carry):
        for u in range(unroll):
            step(unroll * jj + u, u % 2, FULL)
        return carry

    n_loop = jnp.maximum((n_full - 1) // unroll, 0)
    lax.fori_loop(0, n_loop, body, 0)
    left = n_full - unroll * n_loop

    def tail(first_slot):
        for d in range(n_diag):
            step(n_full + d, (first_slot + d) % 2, d + 1 if d < n_diag - 1 else None, cur_diag=d)

    @pl.when(left == 0)
    def _():
        tail(0)

    stride = tq // tk if tq % tk == 0 else 1
    for cnt in sorted({m * stride - unroll * ((m * stride - 1) // unroll) for m in range(1, 2 * unroll + 1)}):
        @pl.when(left == cnt)
        def _(cnt=cnt):
            for u in range(cnt):
                step(n_full - cnt + u, u % 2, FULL if u < cnt - 1 else 0)
            tail(cnt % 2)

    o = acc_ref[...] / jnp.sum(l_ref[...], axis=-1, keepdims=True)
    if n_maps == 2:
        lp = lam_ref[...]
        lam = (jnp.exp(jnp.sum(lp[0:1] * lp[1:2], axis=-1, keepdims=True))
               - jnp.exp(jnp.sum(lp[2:3] * lp[3:4], axis=-1, keepdims=True)) + lambda_init)
        o = o[:tq] - lam * o[tq:]
        r = lax.rsqrt(jnp.mean(o * o, axis=-1, keepdims=True) + NORM_EPS)
        o = o * r * sw_ref[...] * (1.0 - lambda_init)
    o_ref[...] = o.astype(o_ref.dtype)


def _attention(q, k, v, extra, *, heads, batch, seq, tq, tk, rg, n_maps, v_col=0, lambda_init=0.0):
    q3, k3, v3 = (t.reshape(batch, seq, t.shape[1]) for t in (q, k, v))
    rows = n_maps * tq
    rg = min(rg, tq)
    vb = v_col // HEAD_W
    scratch = [pltpu.VMEM((2, rows, tk), F32), pltpu.VMEM((rg, tk), jnp.int32), pltpu.VMEM((2, rows, LANES), F32), pltpu.VMEM((rows, LANES), F32),
               pltpu.VMEM((rows, LANES), F32), pltpu.VMEM((rows, HEAD_W), F32)]
    if n_maps == 2:
        scratch.insert(0, pltpu.VMEM((rows, HEAD_W), BF16))
    in_specs = [pl.BlockSpec((None, tq, HEAD_W), lambda b, h, i: (b, i, h)),
                pl.BlockSpec((None, seq, HEAD_W), lambda b, h, i: (b, 0, h)),
                pl.BlockSpec((None, seq, HEAD_W), lambda b, h, i: (b, 0, vb + h))]
    for e in extra:
        in_specs.append(pl.BlockSpec(e.shape, lambda b, h, i: (0, 0)))
    out = pl.pallas_call(
        functools.partial(_attn_kernel, tq=tq, tk=tk, rg=rg, n_maps=n_maps, lambda_init=lambda_init,
                          unroll=ATTN_UNROLL),
        grid=(batch, heads, seq // tq),
        in_specs=in_specs,
        out_specs=pl.BlockSpec((None, tq, HEAD_W), lambda b, h, i: (b, i, h)),
        out_shape=jax.ShapeDtypeStruct((batch, seq, heads * HEAD_W), BF16),
        scratch_shapes=scratch,
        compiler_params=_cparams("parallel", "parallel", "arbitrary"),
        name="diff_attn" if n_maps == 2 else "mla_attn",
    )(q3, k3, v3, *extra)
    return out.reshape(batch * seq, heads * HEAD_W)


def _gdn_prep_kernel(x_ref, halo_ref, ba_ref, cw_ref, ea_ref, dtb_ref, qkv_ref, gb_ref, *, tiles_per_batch):
    i = pl.program_id(0)
    x = x_ref[...].astype(F32)
    halo = halo_ref[...].astype(F32)
    halo = jnp.where(i % tiles_per_batch == 0, 0.0, halo)
    cw = cw_ref[...]
    row8 = lax.broadcasted_iota(jnp.int32, halo.shape, 0)
    y = x * cw[GDN_CONV - 1:GDN_CONV]
    for s in range(1, GDN_CONV):
        xs = pltpu.roll(x, s, 0)
        hs = pltpu.roll(halo, s, 0)
        first = jnp.where(row8 < s, hs, xs[:SUBLANES])
        xs = jnp.concatenate([first, xs[SUBLANES:]], axis=0)
        y = y + xs * cw[GDN_CONV - 1 - s:GDN_CONV - s]
    y = y * jax.nn.sigmoid(y)
    nqk = 2 * GDN_HEADS
    for h in range(3 * GDN_HEADS):
        sl = slice(h * HEAD_W, (h + 1) * HEAD_W)
        t = y[:, sl]
        if h < nqk:
            t = t * lax.rsqrt(jnp.sum(t * t, axis=-1, keepdims=True) + NORM_EPS)
            if h < GDN_HEADS:
                t = t * (GDN_K_DIM ** -0.5)
        qkv_ref[:, sl] = t
    ba = ba_ref[...].astype(F32)
    lane = lax.broadcasted_iota(jnp.int32, ba.shape, 1)
    sp_in = ba + dtb_ref[...]
    softplus = jnp.maximum(sp_in, 0.0) + jnp.log1p(jnp.exp(-jnp.abs(sp_in)))
    gb_ref[...] = jnp.where(lane < GDN_HEADS, jax.nn.sigmoid(ba), -jnp.exp(ea_ref[...]) * softplus)


def _gdn_prep(proj, conv_w, exp_a, dt_b, col_qkv, col_ba, seq, tm):
    n = proj.shape[0]
    w = 3 * GDN_HEADS * HEAD_W
    tiles_per_batch = seq // tm
    hb = tm // SUBLANES
    return pl.pallas_call(
        functools.partial(_gdn_prep_kernel, tiles_per_batch=tiles_per_batch),
        grid=(n // tm,),
        in_specs=[pl.BlockSpec((tm, w), lambda i: (i, col_qkv // w)),
                  pl.BlockSpec((SUBLANES, w), lambda i: (jnp.maximum(i * hb - 1, 0), col_qkv // w)),
                  pl.BlockSpec((tm, LANES), lambda i: (i, col_ba // LANES)),
                  pl.BlockSpec((GDN_CONV, w), lambda i: (0, 0)),
                  pl.BlockSpec((1, LANES), lambda i: (0, 0)),
                  pl.BlockSpec((1, LANES), lambda i: (0, 0))],
        out_specs=[pl.BlockSpec((tm, w), lambda i: (i, 0)),
                   pl.BlockSpec((tm, LANES), lambda i: (i, 0))],
        out_shape=[jax.ShapeDtypeStruct((n, w), F32), jax.ShapeDtypeStruct((n, LANES), F32)],
        compiler_params=_cparams("parallel"),
        name="gdn_prep",
    )(proj, proj, proj, conv_w, exp_a, dt_b)


def _gdn_intra_kernel(qkv_ref, gb_ref, u_ref, w_ref, qt_ref, ot_ref, kd_ref, eg_ref, *, tc):
    hw = GDN_HEADS * HEAD_W
    nc = tc // CHUNK
    gb = gb_ref[...]
    r_in_chunk = lax.broadcasted_iota(jnp.int32, gb.shape, 0) % CHUNK
    gc = gb
    sh = 1
    while sh < CHUNK:
        gc = gc + jnp.where(r_in_chunk >= sh, pltpu.roll(gc, sh, 0), 0.0)
        sh *= 2
    gc_t = gc.T

    row = lax.broadcasted_iota(jnp.int32, (CHUNK, CHUNK), 0)
    col = lax.broadcasted_iota(jnp.int32, (CHUNK, CHUNK), 1)
    incl = row >= col
    strict = row > col
    eye = (row == col).astype(F32)

    pairs = [(c, h) for c in range(nc) for h in range(GDN_HEADS)]
    st = {}
    for c, h in pairs:
        rs = slice(c * CHUNK, (c + 1) * CHUNK)
        hs = slice(h * HEAD_W, (h + 1) * HEAD_W)
        q = qkv_ref[rs, h * HEAD_W:(h + 1) * HEAD_W]
        k = qkv_ref[rs, hw + h * HEAD_W:hw + (h + 1) * HEAD_W]
        v = qkv_ref[rs, 2 * hw + h * HEAD_W:2 * hw + (h + 1) * HEAD_W]
        beta = gb[rs, h:h + 1]
        g_col = gc[rs, GDN_HEADS + h:GDN_HEADS + h + 1]
        g_row = gc_t[GDN_HEADS + h:GDN_HEADS + h + 1, rs]
        g_last = g_col[CHUNK - 1:CHUNK, :]
        k_beta = k * beta
        e_g = jnp.exp(g_col)
        kd_ref[rs, hs] = k * jnp.exp(g_last - g_col)
        eg_ref[c:c + 1, hs] = jnp.broadcast_to(jnp.exp(g_last), (1, HEAD_W))
        st[c, h] = dict(
            kb16=k_beta.astype(BF16), k16=k.astype(BF16), q16=q.astype(BF16),
            decay=jnp.exp(jnp.where(incl, g_col - g_row, NEG_BIG)),
            rhs=jnp.concatenate([v * beta, k_beta * e_g], axis=1).astype(BF16),
            qg=q * e_g)
    for p in pairs:
        s = st[p]
        s["lmat"] = jnp.where(strict, _mm_nt(s["kb16"], s["k16"]) * s["decay"], 0.0)
        s["attn"] = jnp.where(incl, _mm_nt(s["q16"], s["k16"]) * s["decay"], 0.0).astype(BF16)
        s["t"] = eye - jnp.where((row // 2) == (col // 2), s["lmat"], 0.0)
    b = 2
    while b < CHUNK:
        off = ((row // (2 * b)) == (col // (2 * b))) & ((row // b) % 2 == 1) & ((col // b) % 2 == 0)
        for p in pairs:
            s = st[p]
            s["ct"] = _mm(jnp.where(off, s["lmat"], 0.0), s["t"])
        for p in pairs:
            s = st[p]
            s["t"] = s["t"] - _mm(s["t"], s["ct"])
        b *= 2
    for p in pairs:
        s = st[p]
        s["uw"] = _mm(s["t"], s["rhs"])
    for c, h in pairs:
        s = st[c, h]
        rs = slice(c * CHUNK, (c + 1) * CHUNK)
        hs = slice(h * HEAD_W, (h + 1) * HEAD_W)
        a_uw = _mm(s["attn"], s["uw"])
        u_ref[rs, hs] = s["uw"][:, :HEAD_W]
        w_ref[rs, hs] = s["uw"][:, HEAD_W:].astype(BF16)
        ot_ref[rs, hs] = a_uw[:, :HEAD_W]
        qt_ref[rs, hs] = (s["qg"] - a_uw[:, HEAD_W:]).astype(BF16)


def _gdn_intra(qkv, gb, tc):
    n = qkv.shape[0]
    hw = GDN_HEADS * HEAD_W
    nc = tc // CHUNK
    blk = pl.BlockSpec((tc, hw), lambda i: (i, 0))
    return pl.pallas_call(
        functools.partial(_gdn_intra_kernel, tc=tc),
        grid=(n // tc,),
        in_specs=[pl.BlockSpec((tc, 3 * hw), lambda i: (i, 0)),
                  pl.BlockSpec((tc, LANES), lambda i: (i, 0))],
        out_specs=[blk, blk, blk, blk, blk, pl.BlockSpec((nc, hw), lambda i: (i, 0))],
        out_shape=[jax.ShapeDtypeStruct((n, hw), F32), jax.ShapeDtypeStruct((n, hw), BF16),
                   jax.ShapeDtypeStruct((n, hw), BF16), jax.ShapeDtypeStruct((n, hw), F32),
                   jax.ShapeDtypeStruct((n, hw), F32), jax.ShapeDtypeStruct((n // CHUNK, hw), F32)],
        compiler_params=_cparams("parallel"),
        name="gdn_intra",
    )(qkv, gb)


def _gdn_scan_kernel(u_ref, w_ref, qt_ref, ot_ref, kd_ref, eg_ref, z_ref, ow_ref, o_ref, state_ref, *, tc, batch):
    @pl.when(pl.program_id(0) == 0)
    def _():
        state_ref[...] = jnp.zeros(state_ref.shape, F32)

    chains = [(b, h) for b in range(batch) for h in range(GDN_HEADS)]
    for c in range(tc // CHUNK):
        rs = slice(c * CHUNK, (c + 1) * CHUNK)
        prod, v_new = {}, {}
        for b, h in chains:
            hs = slice(h * HEAD_W, (h + 1) * HEAD_W)
            lhs = jnp.concatenate([w_ref[b, rs, hs], qt_ref[b, rs, hs]], axis=0)
            prod[b, h] = jnp.dot(lhs, state_ref[b * GDN_HEADS + h].astype(BF16), preferred_element_type=F32)
        for b, h in chains:
            hs = slice(h * HEAD_W, (h + 1) * HEAD_W)
            v_new[b, h] = (u_ref[b, rs, hs] - prod[b, h][:CHUNK]).astype(BF16)
        for b, h in chains:
            hs = slice(h * HEAD_W, (h + 1) * HEAD_W)
            idx = b * GDN_HEADS + h
            k_t = kd_ref[b, rs, hs].T.astype(BF16)
            state_ref[idx] = (state_ref[idx] * eg_ref[b, c:c + 1, hs]
                              + jnp.dot(k_t, v_new[b, h], preferred_element_type=F32))
        for b, h in chains:
            hs = slice(h * HEAD_W, (h + 1) * HEAD_W)
            o = prod[b, h][CHUNK:] + ot_ref[b, rs, hs]
            o = o * lax.rsqrt(jnp.mean(o * o, axis=-1, keepdims=True) + NORM_EPS) * ow_ref[...]
            z = z_ref[b, rs, hs].astype(F32)
            o_ref[b, rs, hs] = (o * (z * jax.nn.sigmoid(z))).astype(o_ref.dtype)


def _gdn_scan(u, w, qt, ot, kd, eg, proj, onorm_w, col_z, batch, seq, tc):
    hw = GDN_HEADS * HEAD_W
    nc = tc // CHUNK
    r3 = lambda t: t.reshape(batch, t.shape[0] // batch, t.shape[1])
    blk = pl.BlockSpec((batch, tc, hw), lambda i: (0, i, 0))
    out = pl.pallas_call(
        functools.partial(_gdn_scan_kernel, tc=tc, batch=batch),
        grid=(seq // tc,),
        in_specs=[blk, blk, blk, blk, blk,
                  pl.BlockSpec((batch, nc, hw), lambda i: (0, i, 0)),
                  pl.BlockSpec((batch, tc, hw), lambda i: (0, i, col_z // hw)),
                  pl.BlockSpec((1, HEAD_W), lambda i: (0, 0))],
        out_specs=blk,
        out_shape=jax.ShapeDtypeStruct((batch, seq, hw), BF16),
        scratch_shapes=[pltpu.VMEM((batch * GDN_HEADS, GDN_K_DIM, GDN_V_DIM), F32)],
        compiler_params=_cparams("arbitrary"),
        name="gdn_scan",
    )(r3(u), r3(w), r3(qt), r3(ot), r3(kd), r3(eg), r3(proj), onorm_w)
    return out.reshape(batch * seq, hw)


def _merge_kernel(x_ref, oa_ref, ob_ref, oc_ref, gate_ref, wa_ref, wb_ref, wc_ref, wout_ref, mod_ref, o_ref):
    d = x_ref.shape[1]
    merged = None
    for idx, (o_r, w_r) in enumerate(((oa_ref, wa_ref), (ob_ref, wb_ref), (oc_ref, wc_ref))):
        y = jnp.dot(o_r[...], w_r[...], preferred_element_type=F32)
        g = jax.nn.sigmoid(gate_ref[:, idx * d:(idx + 1) * d].astype(F32))
        merged = g * y if merged is None else merged + g * y
    upd = jnp.dot(merged.astype(BF16), wout_ref[...], preferred_element_type=F32)
    o_ref[...] = x_ref[...] + mod_ref[2:3, :] * upd


def _merge(x2, oa, ob, oc, proj, wa, wb, wc, wout, mod, col_gate, seq, tm):
    n, d = x2.shape
    w = DIFF_HEADS * HEAD_W
    tiles_per_batch = seq // tm
    row = lambda i: (i, 0)
    const = lambda i: (0, 0)
    return pl.pallas_call(
        _merge_kernel,
        grid=(n // tm,),
        in_specs=[pl.BlockSpec((tm, d), row), pl.BlockSpec((tm, w), row), pl.BlockSpec((tm, w), row),
                  pl.BlockSpec((tm, w), row),
                  pl.BlockSpec((tm, N_BRANCHES * d), lambda i: (i, col_gate // (N_BRANCHES * d))),
                  pl.BlockSpec((w, d), const), pl.BlockSpec((w, d), const), pl.BlockSpec((w, d), const),
                  pl.BlockSpec((d, d), const),
                  pl.BlockSpec((None, 6, d), lambda i: (i // tiles_per_batch, 0, 0))],
        out_specs=pl.BlockSpec((tm, d), row),
        out_shape=jax.ShapeDtypeStruct((n, d), F32),
        compiler_params=_cparams("parallel"),
        name="merge",
    )(x2, oa, ob, oc, proj, wa, wb, wc, wout, mod)


def _ffn_kernel(x_ref, nw_ref, mod_ref, wg_ref, wu_ref, wd_ref, o_ref, h_ref, acc_ref):
    j = pl.program_id(1)

    @pl.when(j == 0)
    def _():
        h_ref[...] = _norm_mod(x_ref[...], nw_ref[...], mod_ref[...], 3, 4).astype(BF16)
        acc_ref[...] = jnp.zeros(acc_ref.shape, F32)

    h = h_ref[...]
    gate = jnp.dot(h, wg_ref[...], preferred_element_type=F32)
    up = jnp.dot(h, wu_ref[...], preferred_element_type=F32)
    act = (gate * jax.nn.sigmoid(gate) * up).astype(BF16)
    acc_ref[...] += jnp.dot(act, wd_ref[...], preferred_element_type=F32)

    @pl.when(j == pl.num_programs(1) - 1)
    def _():
        o_ref[...] = x_ref[...] + mod_ref[5:6, :] * acc_ref[...]


def _ffn(x2, nw, mod, w_gu, w_down, seq, tm, th):
    n, d = x2.shape
    hidden = w_down.shape[0]
    nh = hidden // th
    tiles_per_batch = seq // tm
    return pl.pallas_call(
        _ffn_kernel,
        grid=(n // tm, nh),
        in_specs=[pl.BlockSpec((tm, d), lambda i, j: (i, 0)),
                  pl.BlockSpec((1, d), lambda i, j: (0, 0)),
                  pl.BlockSpec((None, 6, d), lambda i, j: (i // tiles_per_batch, 0, 0)),
                  pl.BlockSpec((d, th), lambda i, j: (0, j)),
                  pl.BlockSpec((d, th), lambda i, j: (0, nh + j)),
                  pl.BlockSpec((th, d), lambda i, j: (j, 0))],
        out_specs=pl.BlockSpec((tm, d), lambda i, j: (i, 0)),
        out_shape=jax.ShapeDtypeStruct((n, d), F32),
        scratch_shapes=[pltpu.VMEM((tm, d), BF16), pltpu.VMEM((tm, d), F32)],
        compiler_params=_cparams("parallel", "arbitrary"),
        name="ffn",
    )(x2, nw, mod, w_gu, w_gu, w_down)


def _pad_cols(w, width):
    return jnp.pad(w, ((0, 0), (0, width - w.shape[1])))


def _in_proj_layout(d_model):
    a = DIFF_HEADS * HEAD_W
    g = GDN_HEADS * HEAD_W
    cols = {}
    off = 0
    for name, width in (("gate", N_BRANCHES * d_model), ("c_qkv", 3 * g), ("a_q", a), ("a_k", a), ("a_v", a),
                        ("c_z", g), ("c_ba", LANES), ("b_q", MLA_Q_LORA), ("b_kv", MLA_Q_LORA)):
        if off % width:
            off += width - off % width
        cols[name] = off
        off += width
    return cols, off


def _build_w_in(w_in, d_model, total):
    a = DIFF_HEADS * 2 * DIFF_QK_DIM
    g = GDN_HEADS * GDN_K_DIM
    widths = (a, a, DIFF_HEADS * DIFF_V_DIM, MLA_Q_LORA, MLA_KV_LORA + MLA_ROPE, g, g, GDN_HEADS * GDN_V_DIM,
              GDN_HEADS * GDN_V_DIM, GDN_HEADS, GDN_HEADS, N_BRANCHES * d_model)
    offs = np.concatenate([[0], np.cumsum(widths)])
    part = [w_in[:, int(offs[i]):int(offs[i + 1])] for i in range(len(widths))]
    cols, _ = _in_proj_layout(d_model)
    out = jnp.zeros((w_in.shape[0], total), w_in.dtype)
    place = (("a_q", part[0]), ("a_k", part[1]), ("a_v", part[2]), ("b_q", part[3]), ("b_kv", part[4]),
             ("c_qkv", jnp.concatenate(part[5:8], axis=1)), ("c_z", part[8]), ("gate", part[11]),
             ("c_ba", jnp.concatenate(part[9:11], axis=1)))
    for name, p in place:
        out = lax.dynamic_update_slice(out, p, (0, cols[name]))
    return out.astype(BF16)


def _head_pad(w, heads, per_head, lane_off=0):
    r = w.shape[0]
    w = w.reshape(r, heads, per_head)
    w = jnp.pad(w, ((0, 0), (0, 0), (lane_off, HEAD_W - per_head - lane_off)))
    return w.reshape(r, heads * HEAD_W)


def kernel(x, c, positions, ada_w, ada_b, norm1_w, w_in, diff_qnorm_w, diff_knorm_w, diff_lambda, diff_subln_w, w_o_diff, mla_qa_norm_w, mla_q_up, mla_kva_norm_w, mla_kv_up, mla_qnorm_w, mla_knorm_w, w_o_mla, gdn_conv_w, gdn_a_log, gdn_dt_bias, gdn_onorm_w, w_o_gdn, w_out, norm2_w, ffn_w_gu, ffn_w_down):
    batch, seq, d = x.shape
    depth = ada_w.shape[0]
    n = batch * seq
    qk_dim = MLA_NOPE + MLA_ROPE

    tm = min(1024, seq)
    tq = min(512, seq)
    tc = min(512, seq)

    c_pad = jnp.pad(c, ((0, SUBLANES - batch), (0, 0)))
    mod_all = _ada_mod(c_pad, ada_w, ada_b)[:, :batch].reshape(depth, batch, 6, d)

    cols, total = _in_proj_layout(d)
    tn = 1536
    total = -(-total // tn) * tn

    pos2 = positions.reshape(n, 1)
    half = MLA_ROPE // 2
    inv_freq = ROPE_THETA ** (-jnp.arange(half, dtype=F32) / half)
    invf = jnp.zeros((1, HEAD_W), F32).at[0, MLA_NOPE:qk_dim].set(jnp.concatenate([inv_freq, inv_freq]))

    x2 = x.reshape(n, d)
    for l in range(depth):
        lambda_init = 0.8 - 0.6 * math.exp(-0.3 * l)
        mod = mod_all[l]
        proj = _in_proj(x2, norm1_w[l][None], mod, _build_w_in(w_in[l], d, total), seq, tm, tn)

        qw2 = jnp.tile(diff_qnorm_w[l], 2)[None]
        kw2 = jnp.tile(diff_knorm_w[l], 2)[None]
        a_q, a_k = _diff_prep(proj, qw2, kw2, cols["a_q"], cols["a_k"], tm)
        o_a = _attention(a_q, a_k, proj, (diff_lambda[l], diff_subln_w[l][None]), heads=DIFF_HEADS, batch=batch,
                         seq=seq, tq=min(2 * tq, seq), tk=tq, rg=256, n_maps=2, v_col=cols["a_v"],
                         lambda_init=lambda_init)

        kv_up = mla_kv_up[l].reshape(MLA_KV_LORA, MLA_HEADS, MLA_NOPE + MLA_V_DIM)
        wk_nope = _head_pad(kv_up[:, :, :MLA_NOPE].reshape(MLA_KV_LORA, -1), MLA_HEADS, MLA_NOPE)
        place_rope = _head_pad(jnp.tile(jnp.eye(MLA_ROPE, dtype=F32), (1, MLA_HEADS)), MLA_HEADS, MLA_ROPE, MLA_NOPE)
        pad_rows = MLA_Q_LORA - MLA_KV_LORA - MLA_ROPE
        wk = jnp.concatenate([wk_nope, place_rope, jnp.zeros((pad_rows, MLA_HEADS * HEAD_W), F32)], axis=0)
        wv = jnp.pad(kv_up[:, :, MLA_NOPE:].reshape(MLA_KV_LORA, -1), ((0, MLA_Q_LORA - MLA_KV_LORA), (0, 0)))
        kva_w = jnp.pad(mla_kva_norm_w[l], (0, MLA_Q_LORA - MLA_KV_LORA))[None]
        b_q, b_k, b_v = _mla_prep(
            proj, pos2, mla_qa_norm_w[l][None], _head_pad(mla_q_up[l], MLA_HEADS, qk_dim).astype(BF16),
            kva_w, wk.astype(BF16), wv.astype(BF16),
            jnp.pad(mla_qnorm_w[l], (0, HEAD_W - qk_dim))[None], jnp.pad(mla_knorm_w[l], (0, HEAD_W - qk_dim))[None],
            invf, cols["b_q"], cols["b_kv"], tm)
        o_b = _attention(b_q, b_k, b_v, (), heads=MLA_HEADS, batch=batch, seq=seq, tq=min(4 * tq, seq), tk=tq,
                         rg=256, n_maps=1)

        exp_a = jnp.zeros((1, LANES), F32).at[0, GDN_HEADS:2 * GDN_HEADS].set(gdn_a_log[l])
        dt_b = jnp.zeros((1, LANES), F32).at[0, GDN_HEADS:2 * GDN_HEADS].set(gdn_dt_bias[l])
        c_qkv, c_gb = _gdn_prep(proj, gdn_conv_w[l], exp_a, dt_b, cols["c_qkv"], cols["c_ba"], seq, tm)
        g_u, g_w, g_qt, g_ot, g_kd, g_eg = _gdn_intra(c_qkv, c_gb, tc)
        o_c = _gdn_scan(g_u, g_w, g_qt, g_ot, g_kd, g_eg, proj, gdn_onorm_w[l][None], cols["c_z"], batch, seq, tc)

        x2 = _merge(x2, o_a, o_b, o_c, proj, w_o_diff[l].astype(BF16), w_o_mla[l].astype(BF16),
                    w_o_gdn[l].astype(BF16), w_out[l].astype(BF16), mod, cols["gate"], seq, min(512, seq))
        x2 = _ffn(x2, norm2_w[l][None], mod, ffn_w_gu[l].astype(BF16), ffn_w_down[l].astype(BF16), seq, tm, 256)
    return x2.reshape(batch, seq, d)
```

```python
import functools
import math

import jax
import jax.numpy as jnp
import numpy as np
from jax import lax
from jax.experimental import pallas as pl
from jax.experimental.pallas import tpu as pltpu

F32 = jnp.float32
BF16 = jnp.bfloat16

CHUNK = 64
NORM_EPS = 1e-6
ROPE_THETA = 10000.0
DIFF_HEADS = 4
DIFF_QK_DIM = 64
DIFF_V_DIM = 128
MLA_HEADS = 4
MLA_Q_LORA = 384
MLA_KV_LORA = 256
MLA_NOPE = 64
MLA_ROPE = 32
MLA_V_DIM = 128
GDN_HEADS = 4
GDN_K_DIM = 128
GDN_V_DIM = 128
GDN_CONV = 4
N_BRANCHES = 3

LANES = 128
SUBLANES = 8
HEAD_W = 128
LOG2E = 1.4426950408889634
NEG_BIG = -1e30
VMEM_LIMIT = 56 * 1024 * 1024
ATTN_UNROLL = 4


def _cparams(*sem):
    return pltpu.CompilerParams(dimension_semantics=sem, vmem_limit_bytes=VMEM_LIMIT)


def _mm(a, b):
    return jnp.dot(a.astype(BF16), b.astype(BF16), preferred_element_type=F32)


def _mm_nt(a, b):
    return lax.dot_general(a.astype(BF16), b.astype(BF16), (((1,), (1,)), ((), ())),
                           preferred_element_type=F32)


def _mod_kernel(c_ref, w_ref, b_ref, o_ref):
    o_ref[...] = jnp.dot(c_ref[...], w_ref[...], preferred_element_type=F32) + b_ref[...]


def _ada_mod(c_pad, ada_w, ada_b):
    depth, d, n6 = ada_w.shape
    tn = 1536
    return pl.pallas_call(
        _mod_kernel,
        grid=(depth, n6 // tn),
        in_specs=[pl.BlockSpec((SUBLANES, d), lambda l, j: (0, 0)),
                  pl.BlockSpec((None, d, tn), lambda l, j: (l, 0, j)),
                  pl.BlockSpec((None, 1, tn), lambda l, j: (l, 0, j))],
        out_specs=pl.BlockSpec((None, SUBLANES, tn), lambda l, j: (l, 0, j)),
        out_shape=jax.ShapeDtypeStruct((depth, SUBLANES, n6), F32),
        compiler_params=_cparams("parallel", "parallel"),
        name="ada_mod",
    )(c_pad, ada_w, ada_b.reshape(depth, 1, n6))


def _norm_mod(x, nw, mod, shift_row, scale_row):
    r = lax.rsqrt(jnp.mean(x * x, axis=-1, keepdims=True) + NORM_EPS)
    h = x * r * nw
    return h * (1.0 + mod[scale_row:scale_row + 1, :]) + mod[shift_row:shift_row + 1, :]


def _in_proj_kernel(x_ref, nw_ref, mod_ref, w_ref, o_ref, h_ref):
    @pl.when(pl.program_id(1) == 0)
    def _():
        h_ref[...] = _norm_mod(x_ref[...], nw_ref[...], mod_ref[...], 0, 1).astype(BF16)

    o_ref[...] = jnp.dot(h_ref[...], w_ref[...], preferred_element_type=F32).astype(o_ref.dtype)


def _in_proj(x2, nw, mod, w, seq, tm, tn):
    n, d = x2.shape
    cols = w.shape[1]
    tiles_per_batch = seq // tm
    return pl.pallas_call(
        _in_proj_kernel,
        grid=(n // tm, cols // tn),
        in_specs=[pl.BlockSpec((tm, d), lambda i, j: (i, 0)),
                  pl.BlockSpec((1, d), lambda i, j: (0, 0)),
                  pl.BlockSpec((None, 6, d), lambda i, j: (i // tiles_per_batch, 0, 0)),
                  pl.BlockSpec((d, tn), lambda i, j: (0, j))],
        out_specs=pl.BlockSpec((tm, tn), lambda i, j: (i, j)),
        out_shape=jax.ShapeDtypeStruct((n, cols), BF16),
        scratch_shapes=[pltpu.VMEM((tm, d), BF16)],
        compiler_params=_cparams("parallel", "arbitrary"),
        name="in_proj",
    )(x2, nw, mod, w)


def _half_rms(t, w):
    lane = lax.broadcasted_iota(jnp.int32, t.shape, 1)
    lo = lane < DIFF_QK_DIM
    sq = t * t
    s_lo = jnp.sum(jnp.where(lo, sq, 0.0), axis=-1, keepdims=True)
    s_hi = jnp.sum(jnp.where(lo, 0.0, sq), axis=-1, keepdims=True)
    r = lax.rsqrt(jnp.where(lo, s_lo, s_hi) * (1.0 / DIFF_QK_DIM) + NORM_EPS)
    return t * r * w


def _diff_prep_kernel(q_ref, k_ref, qw_ref, kw_ref, qo_ref, ko_ref):
    qscale = (DIFF_QK_DIM ** -0.5) * LOG2E
    for h in range(DIFF_HEADS):
        sl = slice(h * HEAD_W, (h + 1) * HEAD_W)
        qo_ref[:, sl] = (_half_rms(q_ref[:, sl].astype(F32), qw_ref[...]) * qscale).astype(BF16)
        ko_ref[:, sl] = _half_rms(k_ref[:, sl].astype(F32), kw_ref[...]).astype(BF16)


def _diff_prep(proj, qw2, kw2, col_q, col_k, tm):
    n = proj.shape[0]
    w = DIFF_HEADS * HEAD_W
    return pl.pallas_call(
        _diff_prep_kernel,
        grid=(n // tm,),
        in_specs=[pl.BlockSpec((tm, w), lambda i: (i, col_q // w)),
                  pl.BlockSpec((tm, w), lambda i: (i, col_k // w)),
                  pl.BlockSpec((1, HEAD_W), lambda i: (0, 0)),
                  pl.BlockSpec((1, HEAD_W), lambda i: (0, 0))],
        out_specs=[pl.BlockSpec((tm, w), lambda i: (i, 0)),
                   pl.BlockSpec((tm, w), lambda i: (i, 0))],
        out_shape=[jax.ShapeDtypeStruct((n, w), BF16), jax.ShapeDtypeStruct((n, w), BF16)],
        compiler_params=_cparams("parallel"),
        name="diff_prep",
    )(proj, proj, qw2, kw2)


def _rope_tile(t, cos_t, sin_lo, sin_hi):
    half = MLA_ROPE // 2
    return (t * cos_t + pltpu.roll(t, HEAD_W - half, 1) * sin_lo + pltpu.roll(t, half, 1) * sin_hi)


def _mla_prep_kernel(q_ref, kv_ref, pos_ref, qa_ref, qup_ref, kva_ref, wk_ref, wv_ref,
                     qn_ref, kn_ref, invf_ref, qo_ref, ko_ref, vo_ref):
    qk_dim = MLA_NOPE + MLA_ROPE
    qscale = (qk_dim ** -0.5) * LOG2E
    half = MLA_ROPE // 2

    xq = q_ref[...].astype(F32)
    rq = lax.rsqrt(jnp.mean(xq * xq, axis=-1, keepdims=True) + NORM_EPS)
    q = _mm(xq * rq * qa_ref[...], qup_ref[...])

    xkv = kv_ref[...].astype(F32)
    lane_kv = lax.broadcasted_iota(jnp.int32, xkv.shape, 1)
    is_lat = lane_kv < MLA_KV_LORA
    ssq = jnp.sum(jnp.where(is_lat, xkv * xkv, 0.0), axis=-1, keepdims=True)
    rkv = lax.rsqrt(ssq * (1.0 / MLA_KV_LORA) + NORM_EPS)
    lhs = jnp.where(is_lat, xkv * rkv * kva_ref[...], xkv).astype(BF16)
    k = jnp.dot(lhs, wk_ref[...], preferred_element_type=F32)
    vo_ref[...] = jnp.dot(lhs, wv_ref[...], preferred_element_type=F32).astype(BF16)

    ang = pos_ref[...].astype(F32) * invf_ref[...]
    lane = lax.broadcasted_iota(jnp.int32, ang.shape, 1)
    in_lo = (lane >= MLA_NOPE) & (lane < MLA_NOPE + half)
    in_hi = (lane >= MLA_NOPE + half) & (lane < qk_dim)
    cos_a, sin_a = jnp.cos(ang), jnp.sin(ang)
    cos_t = jnp.where(in_lo | in_hi, cos_a, 1.0)
    sin_lo = jnp.where(in_lo, -sin_a, 0.0)
    sin_hi = jnp.where(in_hi, sin_a, 0.0)

    for h in range(MLA_HEADS):
        sl = slice(h * HEAD_W, (h + 1) * HEAD_W)
        qh = q[:, sl]
        qh = qh * lax.rsqrt(jnp.sum(qh * qh, axis=-1, keepdims=True) * (1.0 / qk_dim) + NORM_EPS)
        qh = _rope_tile(qh * qn_ref[...], cos_t, sin_lo, sin_hi)
        qo_ref[:, sl] = (qh * qscale).astype(BF16)
        kh = k[:, sl]
        kh = kh * lax.rsqrt(jnp.sum(kh * kh, axis=-1, keepdims=True) * (1.0 / qk_dim) + NORM_EPS)
        kh = _rope_tile(kh * kn_ref[...], cos_t, sin_lo, sin_hi)
        ko_ref[:, sl] = kh.astype(BF16)


def _mla_prep(proj, pos2, qa_w, qup, kva_w, wk, wv, qn_w, kn_w, invf, col_q, col_kv, tm):
    n = proj.shape[0]
    w = MLA_HEADS * HEAD_W
    lat = MLA_Q_LORA
    const = lambda i: (0, 0)
    out = jax.ShapeDtypeStruct((n, w), BF16)
    return pl.pallas_call(
        _mla_prep_kernel,
        grid=(n // tm,),
        in_specs=[pl.BlockSpec((tm, lat), lambda i: (i, col_q // lat)),
                  pl.BlockSpec((tm, lat), lambda i: (i, col_kv // lat)),
                  pl.BlockSpec((tm, 1), lambda i: (i, 0)),
                  pl.BlockSpec((1, lat), const),
                  pl.BlockSpec((lat, w), const),
                  pl.BlockSpec((1, lat), const),
                  pl.BlockSpec((lat, w), const),
                  pl.BlockSpec((lat, w), const),
                  pl.BlockSpec((1, HEAD_W), const),
                  pl.BlockSpec((1, HEAD_W), const),
                  pl.BlockSpec((1, HEAD_W), const)],
        out_specs=[pl.BlockSpec((tm, w), lambda i: (i, 0))] * 3,
        out_shape=[out, out, out],
        compiler_params=_cparams("parallel"),
        name="mla_prep",
    )(proj, proj, pos2, qa_w, qup, kva_w, wk, wv, qn_w, kn_w, invf)


def _attn_kernel(*refs, tq, tk, rg, n_maps, lambda_init, unroll):
    if n_maps == 2:
        q_ref, k_ref, v_ref, lam_ref, sw_ref, o_ref, qq_ref, s_ref, cdiff_ref, mx_ref, m_ref, l_ref, acc_ref = refs
    else:
        q_ref, k_ref, v_ref, o_ref, s_ref, cdiff_ref, mx_ref, m_ref, l_ref, acc_ref = refs
        qq_ref = q_ref
    i = pl.program_id(2)
    rows = n_maps * tq
    if n_maps == 2:
        q = q_ref[...]
        lane = lax.broadcasted_iota(jnp.int32, q.shape, 1)
        zero = jnp.zeros_like(q)
        qq_ref[:tq, :] = jnp.where(lane < DIFF_QK_DIM, q, zero)
        qq_ref[tq:, :] = jnp.where(lane >= DIFF_QK_DIM, q, zero)

    m_ref[...] = jnp.full(m_ref.shape, NEG_BIG, F32)
    l_ref[...] = jnp.zeros(l_ref.shape, F32)
    acc_ref[...] = jnp.zeros(acc_ref.shape, F32)

    n_groups = rows // rg
    n_full = (i * tq) // tk
    n_diag = max(tq // tk, 1)

    cdiff_ref[...] = (lax.broadcasted_iota(jnp.int32, (rg, tk), 1) // CHUNK
                      - lax.broadcasted_iota(jnp.int32, (rg, tk), 0) // CHUNK)

    def produce(j, slot, g, diag):
        rsl = slice(g * rg, (g + 1) * rg)
        start = pl.multiple_of(j * tk, tk)
        s = lax.dot_general(qq_ref[rsl, :], k_ref[pl.ds(start, tk), :],
                            (((1,), (1,)), ((), ())), preferred_element_type=F32)
        if diag is not None:
            s = jnp.where(cdiff_ref[...] <= ((g * rg) % tq - diag * tk) // CHUNK, s, NEG_BIG)
        s_ref[slot, rsl, :] = s
        mx_ref[slot, rsl, :] = jnp.broadcast_to(jnp.max(s, axis=-1, keepdims=True), (rg, LANES))

    def consume(j, slot, g, v):
        rsl = slice(g * rg, (g + 1) * rg)
        m_prev = m_ref[rsl, :]
        m_new = jnp.maximum(m_prev, mx_ref[slot, rsl, :])
        alpha = jnp.exp2(m_prev - m_new)
        p = jnp.exp2(s_ref[slot, rsl, :] - jnp.tile(m_new, (1, tk // LANES)))
        p_sum = p[:, :LANES]
        for t in range(1, tk // LANES):
            p_sum = p_sum + p[:, t * LANES:(t + 1) * LANES]
        l_ref[rsl, :] = alpha * l_ref[rsl, :] + p_sum
        m_ref[rsl, :] = m_new
        acc_ref[rsl, :] = alpha * acc_ref[rsl, :] + jnp.dot(p.astype(BF16), v, preferred_element_type=F32)

    FULL = "full"

    def step(j, slot, nxt, cur_diag=None):
        start = pl.multiple_of(j * tk, tk)
        v = v_ref[pl.ds(start, tk), :]
        next_diag = None if nxt == FULL else nxt
        for g in range(n_groups):
            if nxt is not None and live(g, next_diag):
                produce(j + 1, 1 - slot, g, next_diag)
            if live(g, cur_diag):
                consume(j, slot, g, v)

    def live(g, diag):
        return diag is None or (g * rg) % tq + rg > diag * tk

    @pl.when(n_full == 0)
    def _():
        for g in range(n_groups):
            produce(0, 0, g, 0)

    @pl.when(n_full > 0)
    def _():
        for g in range(n_groups):
            produce(0, 0, g, None)

    def body(jj, carry):
        for u in range(unroll):
            step(unroll * jj + u, u % 2, FULL)
        return carry

    n_loop = jnp.maximum((n_full - 1) // unroll, 0)
    lax.fori_loop(0, n_loop, body, 0)
    left = n_full - unroll * n_loop

    def tail(first_slot):
        for d in range(n_diag):
            step(n_full + d, (first_slot + d) % 2, d + 1 if d < n_diag - 1 else None, cur_diag=d)

    @pl.when(left == 0)
    def _():
        tail(0)

    stride = tq // tk if tq % tk == 0 else 1
    for cnt in sorted({m * stride - unroll * ((m * stride - 1) // unroll) for m in range(1, 2 * unroll + 1)}):
        @pl.when(left == cnt)
        def _(cnt=cnt):
            for u in range(cnt):
                step(n_full - cnt + u, u % 2, FULL if u < cnt - 1 else 0)
            tail(cnt % 2)

    o = acc_ref[...] / jnp.sum(l_ref[...], axis=-1, keepdims=True)
    if n_maps == 2:
        lp = lam_ref[...]
        lam = (jnp.exp(jnp.sum(lp[0:1] * lp[1:2], axis=-1, keepdims=True))
               - jnp.exp(jnp.sum(lp[2:3] * lp[3:4], axis=-1, keepdims=True)) + lambda_init)
        o = o[:tq] - lam * o[tq:]
        r = lax.rsqrt(jnp.mean(o * o, axis=-1, keepdims=True) + NORM_EPS)
        o = o * r * sw_ref[...] * (1.0 - lambda_init)
    o_ref[...] = o.astype(o_ref.dtype)


def _attention(q, k, v, extra, *, heads, batch, seq, tq, tk, rg, n_maps, v_col=0, lambda_init=0.0):
    q3, k3, v3 = (t.reshape(batch, seq, t.shape[1]) for t in (q, k, v))
    rows = n_maps * tq
    rg = min(rg, tq)
    vb = v_col // HEAD_W
    scratch = [pltpu.VMEM((2, rows, tk), F32), pltpu.VMEM((rg, tk), jnp.int32), pltpu.VMEM((2, rows, LANES), F32), pltpu.VMEM((rows, LANES), F32),
               pltpu.VMEM((rows, LANES), F32), pltpu.VMEM((rows, HEAD_W), F32)]
    if n_maps == 2:
        scratch.insert(0, pltpu.VMEM((rows, HEAD_W), BF16))
    in_specs = [pl.BlockSpec((None, tq, HEAD_W), lambda b, h, i: (b, i, h)),
                pl.BlockSpec((None, seq, HEAD_W), lambda b, h, i: (b, 0, h)),
                pl.BlockSpec((None, seq, HEAD_W), lambda b, h, i: (b, 0, vb + h))]
    for e in extra:
        in_specs.append(pl.BlockSpec(e.shape, lambda b, h, i: (0, 0)))
    out = pl.pallas_call(
        functools.partial(_attn_kernel, tq=tq, tk=tk, rg=rg, n_maps=n_maps, lambda_init=lambda_init,
                          unroll=ATTN_UNROLL),
        grid=(batch, heads, seq // tq),
        in_specs=in_specs,
        out_specs=pl.BlockSpec((None, tq, HEAD_W), lambda b, h, i: (b, i, h)),
        out_shape=jax.ShapeDtypeStruct((batch, seq, heads * HEAD_W), BF16),
        scratch_shapes=scratch,
        compiler_params=_cparams("parallel", "parallel", "arbitrary"),
        name="diff_attn" if n_maps == 2 else "mla_attn",
    )(q3, k3, v3, *extra)
    return out.reshape(batch * seq, heads * HEAD_W)


def _gdn_prep_kernel(x_ref, halo_ref, ba_ref, cw_ref, ea_ref, dtb_ref, qkv_ref, gb_ref, *, tiles_per_batch):
    i = pl.program_id(0)
    x = x_ref[...].astype(F32)
    halo = halo_ref[...].astype(F32)
    halo = jnp.where(i % tiles_per_batch == 0, 0.0, halo)
    cw = cw_ref[...]
    row8 = lax.broadcasted_iota(jnp.int32, halo.shape, 0)
    y = x * cw[GDN_CONV - 1:GDN_CONV]
    for s in range(1, GDN_CONV):
        xs = pltpu.roll(x, s, 0)
        hs = pltpu.roll(halo, s, 0)
        first = jnp.where(row8 < s, hs, xs[:SUBLANES])
        xs = jnp.concatenate([first, xs[SUBLANES:]], axis=0)
        y = y + xs * cw[GDN_CONV - 1 - s:GDN_CONV - s]
    y = y * jax.nn.sigmoid(y)
    nqk = 2 * GDN_HEADS
    for h in range(3 * GDN_HEADS):
        sl = slice(h * HEAD_W, (h + 1) * HEAD_W)
        t = y[:, sl]
        if h < nqk:
            t = t * lax.rsqrt(jnp.sum(t * t, axis=-1, keepdims=True) + NORM_EPS)
            if h < GDN_HEADS:
                t = t * (GDN_K_DIM ** -0.5)
        qkv_ref[:, sl] = t
    ba = ba_ref[...].astype(F32)
    lane = lax.broadcasted_iota(jnp.int32, ba.shape, 1)
    sp_in = ba + dtb_ref[...]
    softplus = jnp.maximum(sp_in, 0.0) + jnp.log1p(jnp.exp(-jnp.abs(sp_in)))
    gb_ref[...] = jnp.where(lane < GDN_HEADS, jax.nn.sigmoid(ba), -jnp.exp(ea_ref[...]) * softplus)


def _gdn_prep(proj, conv_w, exp_a, dt_b, col_qkv, col_ba, seq, tm):
    n = proj.shape[0]
    w = 3 * GDN_HEADS * HEAD_W
    tiles_per_batch = seq // tm
    hb = tm // SUBLANES
    return pl.pallas_call(
        functools.partial(_gdn_prep_kernel, tiles_per_batch=tiles_per_batch),
        grid=(n // tm,),
        in_specs=[pl.BlockSpec((tm, w), lambda i: (i, col_qkv // w)),
                  pl.BlockSpec((SUBLANES, w), lambda i: (jnp.maximum(i * hb - 1, 0), col_qkv // w)),
                  pl.BlockSpec((tm, LANES), lambda i: (i, col_ba // LANES)),
                  pl.BlockSpec((GDN_CONV, w), lambda i: (0, 0)),
                  pl.BlockSpec((1, LANES), lambda i: (0, 0)),
                  pl.BlockSpec((1, LANES), lambda i: (0, 0))],
        out_specs=[pl.BlockSpec((tm, w), lambda i: (i, 0)),
                   pl.BlockSpec((tm, LANES), lambda i: (i, 0))],
        out_shape=[jax.ShapeDtypeStruct((n, w), F32), jax.ShapeDtypeStruct((n, LANES), F32)],
        compiler_params=_cparams("parallel"),
        name="gdn_prep",
    )(proj, proj, proj, conv_w, exp_a, dt_b)


def _gdn_intra_kernel(qkv_ref, gb_ref, u_ref, w_ref, qt_ref, ot_ref, kd_ref, eg_ref, *, tc):
    hw = GDN_HEADS * HEAD_W
    nc = tc // CHUNK
    gb = gb_ref[...]
    r_in_chunk = lax.broadcasted_iota(jnp.int32, gb.shape, 0) % CHUNK
    gc = gb
    sh = 1
    while sh < CHUNK:
        gc = gc + jnp.where(r_in_chunk >= sh, pltpu.roll(gc, sh, 0), 0.0)
        sh *= 2
    gc_t = gc.T

    row = lax.broadcasted_iota(jnp.int32, (CHUNK, CHUNK), 0)
    col = lax.broadcasted_iota(jnp.int32, (CHUNK, CHUNK), 1)
    incl = row >= col
    strict = row > col
    eye = (row == col).astype(F32)

    pairs = [(c, h) for c in range(nc) for h in range(GDN_HEADS)]
    st = {}
    for c, h in pairs:
        rs = slice(c * CHUNK, (c + 1) * CHUNK)
        hs = slice(h * HEAD_W, (h + 1) * HEAD_W)
        q = qkv_ref[rs, h * HEAD_W:(h + 1) * HEAD_W]
        k = qkv_ref[rs, hw + h * HEAD_W:hw + (h + 1) * HEAD_W]
        v = qkv_ref[rs, 2 * hw + h * HEAD_W:2 * hw + (h + 1) * HEAD_W]
        beta = gb[rs, h:h + 1]
        g_col = gc[rs, GDN_HEADS + h:GDN_HEADS + h + 1]
        g_row = gc_t[GDN_HEADS + h:GDN_HEADS + h + 1, rs]
        g_last = g_col[CHUNK - 1:CHUNK, :]
        k_beta = k * beta
        e_g = jnp.exp(g_col)
        kd_ref[rs, hs] = k * jnp.exp(g_last - g_col)
        eg_ref[c:c + 1, hs] = jnp.broadcast_to(jnp.exp(g_last), (1, HEAD_W))
        st[c, h] = dict(
            kb16=k_beta.astype(BF16), k16=k.astype(BF16), q16=q.astype(BF16),
            decay=jnp.exp(jnp.where(incl, g_col - g_row, NEG_BIG)),
            rhs=jnp.concatenate([v * beta, k_beta * e_g], axis=1).astype(BF16),
            qg=q * e_g)
    for p in pairs:
        s = st[p]
        s["lmat"] = jnp.where(strict, _mm_nt(s["kb16"], s["k16"]) * s["decay"], 0.0)
        s["attn"] = jnp.where(incl, _mm_nt(s["q16"], s["k16"]) * s["decay"], 0.0).astype(BF16)
        s["t"] = eye - jnp.where((row // 2) == (col // 2), s["lmat"], 0.0)
    b = 2
    while b < CHUNK:
        off = ((row // (2 * b)) == (col // (2 * b))) & ((row // b) % 2 == 1) & ((col // b) % 2 == 0)
        for p in pairs:
            s = st[p]
            s["ct"] = _mm(jnp.where(off, s["lmat"], 0.0), s["t"])
        for p in pairs:
            s = st[p]
            s["t"] = s["t"] - _mm(s["t"], s["ct"])
        b *= 2
    for p in pairs:
        s = st[p]
        s["uw"] = _mm(s["t"], s["rhs"])
    for c, h in pairs:
        s = st[c, h]
        rs = slice(c * CHUNK, (c + 1) * CHUNK)
        hs = slice(h * HEAD_W, (h + 1) * HEAD_W)
        a_uw = _mm(s["attn"], s["uw"])
        u_ref[rs, hs] = s["uw"][:, :HEAD_W]
        w_ref[rs, hs] = s["uw"][:, HEAD_W:].astype(BF16)
        ot_ref[rs, hs] = a_uw[:, :HEAD_W]
        qt_ref[rs, hs] = (s["qg"] - a_uw[:, HEAD_W:]).astype(BF16)


def _gdn_intra(qkv, gb, tc):
    n = qkv.shape[0]
    hw = GDN_HEADS * HEAD_W
    nc = tc // CHUNK
    blk = pl.BlockSpec((tc, hw), lambda i: (i, 0))
    return pl.pallas_call(
        functools.partial(_gdn_intra_kernel, tc=tc),
        grid=(n // tc,),
        in_specs=[pl.BlockSpec((tc, 3 * hw), lambda i: (i, 0)),
                  pl.BlockSpec((tc, LANES), lambda i: (i, 0))],
        out_specs=[blk, blk, blk, blk, blk, pl.BlockSpec((nc, hw), lambda i: (i, 0))],
        out_shape=[jax.ShapeDtypeStruct((n, hw), F32), jax.ShapeDtypeStruct((n, hw), BF16),
                   jax.ShapeDtypeStruct((n, hw), BF16), jax.ShapeDtypeStruct((n, hw), F32),
                   jax.ShapeDtypeStruct((n, hw), F32), jax.ShapeDtypeStruct((n // CHUNK, hw), F32)],
        compiler_params=_cparams("parallel"),
        name="gdn_intra",
    )(qkv, gb)


def _gdn_scan_kernel(u_ref, w_ref, qt_ref, ot_ref, kd_ref, eg_ref, z_ref, ow_ref, o_ref, state_ref, *, tc, batch):
    @pl.when(pl.program_id(0) == 0)
    def _():
        state_ref[...] = jnp.zeros(state_ref.shape, F32)

    chains = [(b, h) for b in range(batch) for h in range(GDN_HEADS)]
    for c in range(tc // CHUNK):
        rs = slice(c * CHUNK, (c + 1) * CHUNK)
        prod, v_new = {}, {}
        for b, h in chains:
            hs = slice(h * HEAD_W, (h + 1) * HEAD_W)
            lhs = jnp.concatenate([w_ref[b, rs, hs], qt_ref[b, rs, hs]], axis=0)
            prod[b, h] = jnp.dot(lhs, state_ref[b * GDN_HEADS + h].astype(BF16), preferred_element_type=F32)
        for b, h in chains:
            hs = slice(h * HEAD_W, (h + 1) * HEAD_W)
            v_new[b, h] = (u_ref[b, rs, hs] - prod[b, h][:CHUNK]).astype(BF16)
        for b, h in chains:
            hs = slice(h * HEAD_W, (h + 1) * HEAD_W)
            idx = b * GDN_HEADS + h
            k_t = kd_ref[b, rs, hs].T.astype(BF16)
            state_ref[idx] = (state_ref[idx] * eg_ref[b, c:c + 1, hs]
                              + jnp.dot(k_t, v_new[b, h], preferred_element_type=F32))
        for b, h in chains:
            hs = slice(h * HEAD_W, (h + 1) * HEAD_W)
            o = prod[b, h][CHUNK:] + ot_ref[b, rs, hs]
            o = o * lax.rsqrt(jnp.mean(o * o, axis=-1, keepdims=True) + NORM_EPS) * ow_ref[...]
            z = z_ref[b, rs, hs].astype(F32)
            o_ref[b, rs, hs] = (o * (z * jax.nn.sigmoid(z))).astype(o_ref.dtype)


def _gdn_scan(u, w, qt, ot, kd, eg, proj, onorm_w, col_z, batch, seq, tc):
    hw = GDN_HEADS * HEAD_W
    nc = tc // CHUNK
    r3 = lambda t: t.reshape(batch, t.shape[0] // batch, t.shape[1])
    blk = pl.BlockSpec((batch, tc, hw), lambda i: (0, i, 0))
    out = pl.pallas_call(
        functools.partial(_gdn_scan_kernel, tc=tc, batch=batch),
        grid=(seq // tc,),
        in_specs=[blk, blk, blk, blk, blk,
                  pl.BlockSpec((batch, nc, hw), lambda i: (0, i, 0)),
                  pl.BlockSpec((batch, tc, hw), lambda i: (0, i, col_z // hw)),
                  pl.BlockSpec((1, HEAD_W), lambda i: (0, 0))],
        out_specs=blk,
        out_shape=jax.ShapeDtypeStruct((batch, seq, hw), BF16),
        scratch_shapes=[pltpu.VMEM((batch * GDN_HEADS, GDN_K_DIM, GDN_V_DIM), F32)],
        compiler_params=_cparams("arbitrary"),
        name="gdn_scan",
    )(r3(u), r3(w), r3(qt), r3(ot), r3(kd), r3(eg), r3(proj), onorm_w)
    return out.reshape(batch * seq, hw)


def _merge_kernel(x_ref, oa_ref, ob_ref, oc_ref, gate_ref, wa_ref, wb_ref, wc_ref, wout_ref, mod_ref, o_ref):
    d = x_ref.shape[1]
    merged = None
    for idx, (o_r, w_r) in enumerate(((oa_ref, wa_ref), (ob_ref, wb_ref), (oc_ref, wc_ref))):
        y = jnp.dot(o_r[...], w_r[...], preferred_element_type=F32)
        g = jax.nn.sigmoid(gate_ref[:, idx * d:(idx + 1) * d].astype(F32))
        merged = g * y if merged is None else merged + g * y
    upd = jnp.dot(merged.astype(BF16), wout_ref[...], preferred_element_type=F32)
    o_ref[...] = x_ref[...] + mod_ref[2:3, :] * upd


def _merge(x2, oa, ob, oc, proj, wa, wb, wc, wout, mod, col_gate, seq, tm):
    n, d = x2.shape
    w = DIFF_HEADS * HEAD_W
    tiles_per_batch = seq // tm
    row = lambda i: (i, 0)
    const = lambda i: (0, 0)
    return pl.pallas_call(
        _merge_kernel,
        grid=(n // tm,),
        in_specs=[pl.BlockSpec((tm, d), row), pl.BlockSpec((tm, w), row), pl.BlockSpec((tm, w), row),
                  pl.BlockSpec((tm, w), row),
                  pl.BlockSpec((tm, N_BRANCHES * d), lambda i: (i, col_gate // (N_BRANCHES * d))),
                  pl.BlockSpec((w, d), const), pl.BlockSpec((w, d), const), pl.BlockSpec((w, d), const),
                  pl.BlockSpec((d, d), const),
                  pl.BlockSpec((None, 6, d), lambda i: (i // tiles_per_batch, 0, 0))],
        out_specs=pl.BlockSpec((tm, d), row),
        out_shape=jax.ShapeDtypeStruct((n, d), F32),
        compiler_params=_cparams("parallel"),
        name="merge",
    )(x2, oa, ob, oc, proj, wa, wb, wc, wout, mod)


def _ffn_kernel(x_ref, nw_ref, mod_ref, wgu_ref, wd_ref, o_ref, h_ref, *, th):
    hidden = wd_ref.shape[0]
    h_ref[...] = _norm_mod(x_ref[...], nw_ref[...], mod_ref[...], 3, 4).astype(BF16)

    def gate_up(c):
        h = h_ref[...]
        return (jnp.dot(h, wgu_ref[:, c * th:(c + 1) * th], preferred_element_type=F32),
                jnp.dot(h, wgu_ref[:, hidden + c * th:hidden + (c + 1) * th], preferred_element_type=F32))

    acc = None
    nxt = gate_up(0)
    for c in range(hidden // th):
        gate, up = nxt
        if c + 1 < hidden // th:
            nxt = gate_up(c + 1)
        act = (gate * jax.nn.sigmoid(gate) * up).astype(BF16)
        part = jnp.dot(act, wd_ref[c * th:(c + 1) * th, :], preferred_element_type=F32)
        acc = part if acc is None else acc + part
    o_ref[...] = x_ref[...] + mod_ref[5:6, :] * acc


def _ffn(x2, nw, mod, w_gu, w_down, seq, tm, th):
    n, d = x2.shape
    hidden = w_down.shape[0]
    tiles_per_batch = seq // tm
    resident = pl.Buffered(1)
    return pl.pallas_call(
        functools.partial(_ffn_kernel, th=th),
        grid=(n // tm,),
        in_specs=[pl.BlockSpec((tm, d), lambda i: (i, 0)),
                  pl.BlockSpec((1, d), lambda i: (0, 0)),
                  pl.BlockSpec((None, 6, d), lambda i: (i // tiles_per_batch, 0, 0)),
                  pl.BlockSpec((d, 2 * hidden), lambda i: (0, 0), pipeline_mode=resident),
                  pl.BlockSpec((hidden, d), lambda i: (0, 0), pipeline_mode=resident)],
        out_specs=pl.BlockSpec((tm, d), lambda i: (i, 0)),
        out_shape=jax.ShapeDtypeStruct((n, d), F32),
        scratch_shapes=[pltpu.VMEM((tm, d), BF16)],
        compiler_params=_cparams("parallel"),
        name="ffn",
    )(x2, nw, mod, w_gu, w_down)


def _pad_cols(w, width):
    return jnp.pad(w, ((0, 0), (0, width - w.shape[1])))


def _in_proj_layout(d_model):
    a = DIFF_HEADS * HEAD_W
    g = GDN_HEADS * HEAD_W
    cols = {}
    off = 0
    for name, width in (("gate", N_BRANCHES * d_model), ("c_qkv", 3 * g), ("a_q", a), ("a_k", a), ("a_v", a),
                        ("c_z", g), ("c_ba", LANES), ("b_q", MLA_Q_LORA), ("b_kv", MLA_Q_LORA)):
        if off % width:
            off += width - off % width
        cols[name] = off
        off += width
    return cols, off


def _build_w_in(w_in, d_model, total):
    a = DIFF_HEADS * 2 * DIFF_QK_DIM
    g = GDN_HEADS * GDN_K_DIM
    widths = (a, a, DIFF_HEADS * DIFF_V_DIM, MLA_Q_LORA, MLA_KV_LORA + MLA_ROPE, g, g, GDN_HEADS * GDN_V_DIM,
              GDN_HEADS * GDN_V_DIM, GDN_HEADS, GDN_HEADS, N_BRANCHES * d_model)
    offs = np.concatenate([[0], np.cumsum(widths)])
    part = [w_in[:, int(offs[i]):int(offs[i + 1])] for i in range(len(widths))]
    cols, _ = _in_proj_layout(d_model)
    out = jnp.zeros((w_in.shape[0], total), w_in.dtype)
    place = (("a_q", part[0]), ("a_k", part[1]), ("a_v", part[2]), ("b_q", part[3]), ("b_kv", part[4]),
             ("c_qkv", jnp.concatenate(part[5:8], axis=1)), ("c_z", part[8]), ("gate", part[11]),
             ("c_ba", jnp.concatenate(part[9:11], axis=1)))
    for name, p in place:
        out = lax.dynamic_update_slice(out, p, (0, cols[name]))
    return out.astype(BF16)


def _head_pad(w, heads, per_head, lane_off=0):
    r = w.shape[0]
    w = w.reshape(r, heads, per_head)
    w = jnp.pad(w, ((0, 0), (0, 0), (lane_off, HEAD_W - per_head - lane_off)))
    return w.reshape(r, heads * HEAD_W)


def kernel(x, c, positions, ada_w, ada_b, norm1_w, w_in, diff_qnorm_w, diff_knorm_w, diff_lambda, diff_subln_w, w_o_diff, mla_qa_norm_w, mla_q_up, mla_kva_norm_w, mla_kv_up, mla_qnorm_w, mla_knorm_w, w_o_mla, gdn_conv_w, gdn_a_log, gdn_dt_bias, gdn_onorm_w, w_o_gdn, w_out, norm2_w, ffn_w_gu, ffn_w_down):
    batch, seq, d = x.shape
    depth = ada_w.shape[0]
    n = batch * seq
    qk_dim = MLA_NOPE + MLA_ROPE

    tm = min(1024, seq)
    tq = min(512, seq)
    tc = min(512, seq)

    c_pad = jnp.pad(c, ((0, SUBLANES - batch), (0, 0)))
    mod_all = _ada_mod(c_pad, ada_w, ada_b)[:, :batch].reshape(depth, batch, 6, d)

    cols, total = _in_proj_layout(d)
    tn = 3840
    total = -(-total // tn) * tn

    pos2 = positions.reshape(n, 1)
    half = MLA_ROPE // 2
    inv_freq = ROPE_THETA ** (-jnp.arange(half, dtype=F32) / half)
    invf = jnp.zeros((1, HEAD_W), F32).at[0, MLA_NOPE:qk_dim].set(jnp.concatenate([inv_freq, inv_freq]))

    x2 = x.reshape(n, d)
    for l in range(depth):
        lambda_init = 0.8 - 0.6 * math.exp(-0.3 * l)
        mod = mod_all[l]
        proj = _in_proj(x2, norm1_w[l][None], mod, _build_w_in(w_in[l], d, total), seq, tm, tn)

        qw2 = jnp.tile(diff_qnorm_w[l], 2)[None]
        kw2 = jnp.tile(diff_knorm_w[l], 2)[None]
        a_q, a_k = _diff_prep(proj, qw2, kw2, cols["a_q"], cols["a_k"], tm)
        o_a = _attention(a_q, a_k, proj, (diff_lambda[l], diff_subln_w[l][None]), heads=DIFF_HEADS, batch=batch,
                         seq=seq, tq=min(2 * tq, seq), tk=tq, rg=256, n_maps=2, v_col=cols["a_v"],
                         lambda_init=lambda_init)

        kv_up = mla_kv_up[l].reshape(MLA_KV_LORA, MLA_HEADS, MLA_NOPE + MLA_V_DIM)
        wk_nope = _head_pad(kv_up[:, :, :MLA_NOPE].reshape(MLA_KV_LORA, -1), MLA_HEADS, MLA_NOPE)
        place_rope = _head_pad(jnp.tile(jnp.eye(MLA_ROPE, dtype=F32), (1, MLA_HEADS)), MLA_HEADS, MLA_ROPE, MLA_NOPE)
        pad_rows = MLA_Q_LORA - MLA_KV_LORA - MLA_ROPE
        wk = jnp.concatenate([wk_nope, place_rope, jnp.zeros((pad_rows, MLA_HEADS * HEAD_W), F32)], axis=0)
        wv = jnp.pad(kv_up[:, :, MLA_NOPE:].reshape(MLA_KV_LORA, -1), ((0, MLA_Q_LORA - MLA_KV_LORA), (0, 0)))
        kva_w = jnp.pad(mla_kva_norm_w[l], (0, MLA_Q_LORA - MLA_KV_LORA))[None]
        b_q, b_k, b_v = _mla_prep(
            proj, pos2, mla_qa_norm_w[l][None], _head_pad(mla_q_up[l], MLA_HEADS, qk_dim).astype(BF16),
            kva_w, wk.astype(BF16), wv.astype(BF16),
            jnp.pad(mla_qnorm_w[l], (0, HEAD_W - qk_dim))[None], jnp.pad(mla_knorm_w[l], (0, HEAD_W - qk_dim))[None],
            invf, cols["b_q"], cols["b_kv"], tm)
        o_b = _attention(b_q, b_k, b_v, (), heads=MLA_HEADS, batch=batch, seq=seq, tq=min(4 * tq, seq), tk=tq,
                         rg=256, n_maps=1)

        exp_a = jnp.zeros((1, LANES), F32).at[0, GDN_HEADS:2 * GDN_HEADS].set(gdn_a_log[l])
        dt_b = jnp.zeros((1, LANES), F32).at[0, GDN_HEADS:2 * GDN_HEADS].set(gdn_dt_bias[l])
        c_qkv, c_gb = _gdn_prep(proj, gdn_conv_w[l], exp_a, dt_b, cols["c_qkv"], cols["c_ba"], seq, tm)
        g_u, g_w, g_qt, g_ot, g_kd, g_eg = _gdn_intra(c_qkv, c_gb, tc)
        o_c = _gdn_scan(g_u, g_w, g_qt, g_ot, g_kd, g_eg, proj, gdn_onorm_w[l][None], cols["c_z"], batch, seq, tc)

        x2 = _merge(x2, o_a, o_b, o_c, proj, w_o_diff[l].astype(BF16), w_o_mla[l].astype(BF16),
                    w_o_gdn[l].astype(BF16), w_out[l].astype(BF16), mod, cols["gate"], seq, min(512, seq))
        x2 = _ffn(x2, norm2_w[l][None], mod, ffn_w_gu[l].astype(BF16), ffn_w_down[l].astype(BF16), seq,
                  min(512, seq), 256)
    return x2.reshape(batch, seq, d)
```

```python
import functools
import math

import jax
import jax.numpy as jnp
import numpy as np
from jax import lax
from jax.experimental import pallas as pl
from jax.experimental.pallas import tpu as pltpu

F32 = jnp.float32
BF16 = jnp.bfloat16

CHUNK = 64
NORM_EPS = 1e-6
ROPE_THETA = 10000.0
DIFF_HEADS = 4
DIFF_QK_DIM = 64
DIFF_V_DIM = 128
MLA_HEADS = 4
MLA_Q_LORA = 384
MLA_KV_LORA = 256
MLA_NOPE = 64
MLA_ROPE = 32
MLA_V_DIM = 128
GDN_HEADS = 4
GDN_K_DIM = 128
GDN_V_DIM = 128
GDN_CONV = 4
N_BRANCHES = 3

LANES = 128
SUBLANES = 8
HEAD_W = 128
LOG2E = 1.4426950408889634
NEG_BIG = -1e30
VMEM_LIMIT = 56 * 1024 * 1024
ATTN_UNROLL = 4


def _cparams(*sem):
    return pltpu.CompilerParams(dimension_semantics=sem, vmem_limit_bytes=VMEM_LIMIT)


def _mm(a, b):
    return jnp.dot(a.astype(BF16), b.astype(BF16), preferred_element_type=F32)


def _mm_nt(a, b):
    return lax.dot_general(a.astype(BF16), b.astype(BF16), (((1,), (1,)), ((), ())),
                           preferred_element_type=F32)


def _mod_kernel(c_ref, w_ref, b_ref, o_ref):
    o_ref[...] = jnp.dot(c_ref[...], w_ref[...], preferred_element_type=F32) + b_ref[...]


def _ada_mod(c_pad, ada_w, ada_b):
    depth, d, n6 = ada_w.shape
    tn = 1536
    return pl.pallas_call(
        _mod_kernel,
        grid=(depth, n6 // tn),
        in_specs=[pl.BlockSpec((SUBLANES, d), lambda l, j: (0, 0)),
                  pl.BlockSpec((None, d, tn), lambda l, j: (l, 0, j)),
                  pl.BlockSpec((None, 1, tn), lambda l, j: (l, 0, j))],
        out_specs=pl.BlockSpec((None, SUBLANES, tn), lambda l, j: (l, 0, j)),
        out_shape=jax.ShapeDtypeStruct((depth, SUBLANES, n6), F32),
        compiler_params=_cparams("parallel", "parallel"),
        name="ada_mod",
    )(c_pad, ada_w, ada_b.reshape(depth, 1, n6))


def _norm_mod(x, nw, mod, shift_row, scale_row):
    r = lax.rsqrt(jnp.mean(x * x, axis=-1, keepdims=True) + NORM_EPS)
    h = x * r * nw
    return h * (1.0 + mod[scale_row:scale_row + 1, :]) + mod[shift_row:shift_row + 1, :]


def _in_proj_kernel(x_ref, nw_ref, mod_ref, w_ref, o_ref, h_ref):
    @pl.when(pl.program_id(1) == 0)
    def _():
        h_ref[...] = _norm_mod(x_ref[...], nw_ref[...], mod_ref[...], 0, 1).astype(BF16)

    o_ref[...] = jnp.dot(h_ref[...], w_ref[...], preferred_element_type=F32).astype(o_ref.dtype)


def _in_proj(x2, nw, mod, w, seq, tm, tn):
    n, d = x2.shape
    cols = w.shape[1]
    tiles_per_batch = seq // tm
    return pl.pallas_call(
        _in_proj_kernel,
        grid=(n // tm, cols // tn),
        in_specs=[pl.BlockSpec((tm, d), lambda i, j: (i, 0)),
                  pl.BlockSpec((1, d), lambda i, j: (0, 0)),
                  pl.BlockSpec((None, 6, d), lambda i, j: (i // tiles_per_batch, 0, 0)),
                  pl.BlockSpec((d, tn), lambda i, j: (0, j))],
        out_specs=pl.BlockSpec((tm, tn), lambda i, j: (i, j)),
        out_shape=jax.ShapeDtypeStruct((n, cols), BF16),
        scratch_shapes=[pltpu.VMEM((tm, d), BF16)],
        compiler_params=_cparams("parallel", "arbitrary"),
        name="in_proj",
    )(x2, nw, mod, w)


def _half_rms(t, w):
    lane = lax.broadcasted_iota(jnp.int32, t.shape, 1)
    lo = lane < DIFF_QK_DIM
    sq = t * t
    s_lo = jnp.sum(jnp.where(lo, sq, 0.0), axis=-1, keepdims=True)
    s_hi = jnp.sum(jnp.where(lo, 0.0, sq), axis=-1, keepdims=True)
    r = lax.rsqrt(jnp.where(lo, s_lo, s_hi) * (1.0 / DIFF_QK_DIM) + NORM_EPS)
    return t * r * w


def _diff_prep_kernel(q_ref, k_ref, qw_ref, kw_ref, qo_ref, ko_ref):
    qscale = (DIFF_QK_DIM ** -0.5) * LOG2E
    for h in range(DIFF_HEADS):
        sl = slice(h * HEAD_W, (h + 1) * HEAD_W)
        qo_ref[:, sl] = (_half_rms(q_ref[:, sl].astype(F32), qw_ref[...]) * qscale).astype(BF16)
        ko_ref[:, sl] = _half_rms(k_ref[:, sl].astype(F32), kw_ref[...]).astype(BF16)


def _diff_prep(proj, qw2, kw2, col_q, col_k, tm):
    n = proj.shape[0]
    w = DIFF_HEADS * HEAD_W
    return pl.pallas_call(
        _diff_prep_kernel,
        grid=(n // tm,),
        in_specs=[pl.BlockSpec((tm, w), lambda i: (i, col_q // w)),
                  pl.BlockSpec((tm, w), lambda i: (i, col_k // w)),
                  pl.BlockSpec((1, HEAD_W), lambda i: (0, 0)),
                  pl.BlockSpec((1, HEAD_W), lambda i: (0, 0))],
        out_specs=[pl.BlockSpec((tm, w), lambda i: (i, 0)),
                   pl.BlockSpec((tm, w), lambda i: (i, 0))],
        out_shape=[jax.ShapeDtypeStruct((n, w), BF16), jax.ShapeDtypeStruct((n, w), BF16)],
        compiler_params=_cparams("parallel"),
        name="diff_prep",
    )(proj, proj, qw2, kw2)


def _rope_tile(t, cos_t, sin_lo, sin_hi):
    half = MLA_ROPE // 2
    return (t * cos_t + pltpu.roll(t, HEAD_W - half, 1) * sin_lo + pltpu.roll(t, half, 1) * sin_hi)


def _mla_prep_kernel(q_ref, kv_ref, pos_ref, qa_ref, qup_ref, kva_ref, wk_ref, wv_ref,
                     qn_ref, kn_ref, invf_ref, qo_ref, ko_ref, vo_ref):
    qk_dim = MLA_NOPE + MLA_ROPE
    qscale = (qk_dim ** -0.5) * LOG2E
    half = MLA_ROPE // 2

    xq = q_ref[...].astype(F32)
    rq = lax.rsqrt(jnp.mean(xq * xq, axis=-1, keepdims=True) + NORM_EPS)
    q = _mm(xq * rq * qa_ref[...], qup_ref[...])

    xkv = kv_ref[...].astype(F32)
    lane_kv = lax.broadcasted_iota(jnp.int32, xkv.shape, 1)
    is_lat = lane_kv < MLA_KV_LORA
    ssq = jnp.sum(jnp.where(is_lat, xkv * xkv, 0.0), axis=-1, keepdims=True)
    rkv = lax.rsqrt(ssq * (1.0 / MLA_KV_LORA) + NORM_EPS)
    lhs = jnp.where(is_lat, xkv * rkv * kva_ref[...], xkv).astype(BF16)
    k = jnp.dot(lhs, wk_ref[...], preferred_element_type=F32)
    vo_ref[...] = jnp.dot(lhs, wv_ref[...], preferred_element_type=F32).astype(BF16)

    ang = pos_ref[...].astype(F32) * invf_ref[...]
    lane = lax.broadcasted_iota(jnp.int32, ang.shape, 1)
    in_lo = (lane >= MLA_NOPE) & (lane < MLA_NOPE + half)
    in_hi = (lane >= MLA_NOPE + half) & (lane < qk_dim)
    cos_a, sin_a = jnp.cos(ang), jnp.sin(ang)
    cos_t = jnp.where(in_lo | in_hi, cos_a, 1.0)
    sin_lo = jnp.where(in_lo, -sin_a, 0.0)
    sin_hi = jnp.where(in_hi, sin_a, 0.0)

    for h in range(MLA_HEADS):
        sl = slice(h * HEAD_W, (h + 1) * HEAD_W)
        qh = q[:, sl]
        qh = qh * lax.rsqrt(jnp.sum(qh * qh, axis=-1, keepdims=True) * (1.0 / qk_dim) + NORM_EPS)
        qh = _rope_tile(qh * qn_ref[...], cos_t, sin_lo, sin_hi)
        qo_ref[:, sl] = (qh * qscale).astype(BF16)
        kh = k[:, sl]
        kh = kh * lax.rsqrt(jnp.sum(kh * kh, axis=-1, keepdims=True) * (1.0 / qk_dim) + NORM_EPS)
        kh = _rope_tile(kh * kn_ref[...], cos_t, sin_lo, sin_hi)
        ko_ref[:, sl] = kh.astype(BF16)


def _mla_prep(proj, pos2, qa_w, qup, kva_w, wk, wv, qn_w, kn_w, invf, col_q, col_kv, tm):
    n = proj.shape[0]
    w = MLA_HEADS * HEAD_W
    lat = MLA_Q_LORA
    const = lambda i: (0, 0)
    out = jax.ShapeDtypeStruct((n, w), BF16)
    return pl.pallas_call(
        _mla_prep_kernel,
        grid=(n // tm,),
        in_specs=[pl.BlockSpec((tm, lat), lambda i: (i, col_q // lat)),
                  pl.BlockSpec((tm, lat), lambda i: (i, col_kv // lat)),
                  pl.BlockSpec((tm, 1), lambda i: (i, 0)),
                  pl.BlockSpec((1, lat), const),
                  pl.BlockSpec((lat, w), const),
                  pl.BlockSpec((1, lat), const),
                  pl.BlockSpec((lat, w), const),
                  pl.BlockSpec((lat, w), const),
                  pl.BlockSpec((1, HEAD_W), const),
                  pl.BlockSpec((1, HEAD_W), const),
                  pl.BlockSpec((1, HEAD_W), const)],
        out_specs=[pl.BlockSpec((tm, w), lambda i: (i, 0))] * 3,
        out_shape=[out, out, out],
        compiler_params=_cparams("parallel"),
        name="mla_prep",
    )(proj, proj, pos2, qa_w, qup, kva_w, wk, wv, qn_w, kn_w, invf)


def _attn_kernel(*refs, tq, tk, rg, n_maps, lambda_init, unroll):
    if n_maps == 2:
        q_ref, k_ref, v_ref, lam_ref, sw_ref, o_ref, qq_ref, s_ref, cdiff_ref, mx_ref, m_ref, l_ref, acc_ref = refs
    else:
        q_ref, k_ref, v_ref, o_ref, s_ref, cdiff_ref, mx_ref, m_ref, l_ref, acc_ref = refs
        qq_ref = q_ref
    i = pl.program_id(2)
    rows = n_maps * tq
    if n_maps == 2:
        q = q_ref[...]
        lane = lax.broadcasted_iota(jnp.int32, q.shape, 1)
        zero = jnp.zeros_like(q)
        qq_ref[:tq, :] = jnp.where(lane < DIFF_QK_DIM, q, zero)
        qq_ref[tq:, :] = jnp.where(lane >= DIFF_QK_DIM, q, zero)

    m_ref[...] = jnp.full(m_ref.shape, NEG_BIG, F32)
    l_ref[...] = jnp.zeros(l_ref.shape, F32)
    acc_ref[...] = jnp.zeros(acc_ref.shape, F32)

    n_groups = rows // rg
    n_full = (i * tq) // tk
    n_diag = max(tq // tk, 1)

    cdiff_ref[...] = (lax.broadcasted_iota(jnp.int32, (rg, tk), 1) // CHUNK
                      - lax.broadcasted_iota(jnp.int32, (rg, tk), 0) // CHUNK)

    def produce(j, slot, g, diag):
        rsl = slice(g * rg, (g + 1) * rg)
        start = pl.multiple_of(j * tk, tk)
        s = lax.dot_general(qq_ref[rsl, :], k_ref[pl.ds(start, tk), :],
                            (((1,), (1,)), ((), ())), preferred_element_type=F32)
        if diag is not None:
            s = jnp.where(cdiff_ref[...] <= ((g * rg) % tq - diag * tk) // CHUNK, s, NEG_BIG)
        s_ref[slot, rsl, :] = s
        mx_ref[slot, rsl, :] = jnp.broadcast_to(jnp.max(s, axis=-1, keepdims=True), (rg, LANES))

    def consume(j, slot, g, v):
        rsl = slice(g * rg, (g + 1) * rg)
        m_prev = m_ref[rsl, :]
        m_new = jnp.maximum(m_prev, mx_ref[slot, rsl, :])
        alpha = jnp.exp2(m_prev - m_new)
        p = jnp.exp2(s_ref[slot, rsl, :] - jnp.tile(m_new, (1, tk // LANES)))
        p_sum = p[:, :LANES]
        for t in range(1, tk // LANES):
            p_sum = p_sum + p[:, t * LANES:(t + 1) * LANES]
        l_ref[rsl, :] = alpha * l_ref[rsl, :] + p_sum
        m_ref[rsl, :] = m_new
        acc_ref[rsl, :] = alpha * acc_ref[rsl, :] + jnp.dot(p.astype(BF16), v, preferred_element_type=F32)

    FULL = "full"

    def step(j, slot, nxt, cur_diag=None):
        start = pl.multiple_of(j * tk, tk)
        v = v_ref[pl.ds(start, tk), :]
        next_diag = None if nxt == FULL else nxt
        for g in range(n_groups):
            if nxt is not None and live(g, next_diag):
                produce(j + 1, 1 - slot, g, next_diag)
            if live(g, cur_diag):
                consume(j, slot, g, v)

    def live(g, diag):
        return diag is None or (g * rg) % tq + rg > diag * tk

    @pl.when(n_full == 0)
    def _():
        for g in range(n_groups):
            produce(0, 0, g, 0)

    @pl.when(n_full > 0)
    def _():
        for g in range(n_groups):
            produce(0, 0, g, None)

    def body(jj, carry):
        for u in range(unroll):
            step(unroll * jj + u, u % 2, FULL)
        return carry

    n_loop = jnp.maximum((n_full - 1) // unroll, 0)
    lax.fori_loop(0, n_loop, body, 0)
    left = n_full - unroll * n_loop

    def tail(first_slot):
        for d in range(n_diag):
            step(n_full + d, (first_slot + d) % 2, d + 1 if d < n_diag - 1 else None, cur_diag=d)

    @pl.when(left == 0)
    def _():
        tail(0)

    stride = tq // tk if tq % tk == 0 else 1
    for cnt in sorted({m * stride - unroll * ((m * stride - 1) // unroll) for m in range(1, 2 * unroll + 1)}):
        @pl.when(left == cnt)
        def _(cnt=cnt):
            for u in range(cnt):
                step(n_full - cnt + u, u % 2, FULL if u < cnt - 1 else 0)
            tail(cnt % 2)

    o = acc_ref[...] / jnp.sum(l_ref[...], axis=-1, keepdims=True)
    if n_maps == 2:
        lp = lam_ref[...]
        lam = (jnp.exp(jnp.sum(lp[0:1] * lp[1:2], axis=-1, keepdims=True))
               - jnp.exp(jnp.sum(lp[2:3] * lp[3:4], axis=-1, keepdims=True)) + lambda_init)
        o = o[:tq] - lam * o[tq:]
        r = lax.rsqrt(jnp.mean(o * o, axis=-1, keepdims=True) + NORM_EPS)
        o = o * r * sw_ref[...] * (1.0 - lambda_init)
    o_ref[...] = o.astype(o_ref.dtype)


def _attention(q, k, v, extra, *, heads, batch, seq, tq, tk, rg, n_maps, v_col=0, lambda_init=0.0):
    q3, k3, v3 = (t.reshape(batch, seq, t.shape[1]) for t in (q, k, v))
    rows = n_maps * tq
    rg = min(rg, tq)
    vb = v_col // HEAD_W
    scratch = [pltpu.VMEM((2, rows, tk), F32), pltpu.VMEM((rg, tk), jnp.int32), pltpu.VMEM((2, rows, LANES), F32), pltpu.VMEM((rows, LANES), F32),
               pltpu.VMEM((rows, LANES), F32), pltpu.VMEM((rows, HEAD_W), F32)]
    if n_maps == 2:
        scratch.insert(0, pltpu.VMEM((rows, HEAD_W), BF16))
    in_specs = [pl.BlockSpec((None, tq, HEAD_W), lambda b, h, i: (b, i, h)),
                pl.BlockSpec((None, seq, HEAD_W), lambda b, h, i: (b, 0, h)),
                pl.BlockSpec((None, seq, HEAD_W), lambda b, h, i: (b, 0, vb + h))]
    for e in extra:
        in_specs.append(pl.BlockSpec(e.shape, lambda b, h, i: (0, 0)))
    out = pl.pallas_call(
        functools.partial(_attn_kernel, tq=tq, tk=tk, rg=rg, n_maps=n_maps, lambda_init=lambda_init,
                          unroll=ATTN_UNROLL),
        grid=(batch, heads, seq // tq),
        in_specs=in_specs,
        out_specs=pl.BlockSpec((None, tq, HEAD_W), lambda b, h, i: (b, i, h)),
        out_shape=jax.ShapeDtypeStruct((batch, seq, heads * HEAD_W), BF16),
        scratch_shapes=scratch,
        compiler_params=_cparams("parallel", "parallel", "arbitrary"),
        name="diff_attn" if n_maps == 2 else "mla_attn",
    )(q3, k3, v3, *extra)
    return out.reshape(batch * seq, heads * HEAD_W)


def _gdn_prep_kernel(x_ref, halo_ref, ba_ref, cw_ref, ea_ref, dtb_ref, qkv_ref, gb_ref, *, tiles_per_batch):
    i = pl.program_id(0)
    x = x_ref[...].astype(F32)
    halo = halo_ref[...].astype(F32)
    halo = jnp.where(i % tiles_per_batch == 0, 0.0, halo)
    cw = cw_ref[...]
    row8 = lax.broadcasted_iota(jnp.int32, halo.shape, 0)
    y = x * cw[GDN_CONV - 1:GDN_CONV]
    for s in range(1, GDN_CONV):
        xs = pltpu.roll(x, s, 0)
        hs = pltpu.roll(halo, s, 0)
        first = jnp.where(row8 < s, hs, xs[:SUBLANES])
        xs = jnp.concatenate([first, xs[SUBLANES:]], axis=0)
        y = y + xs * cw[GDN_CONV - 1 - s:GDN_CONV - s]
    y = y * jax.nn.sigmoid(y)
    nqk = 2 * GDN_HEADS
    for h in range(3 * GDN_HEADS):
        sl = slice(h * HEAD_W, (h + 1) * HEAD_W)
        t = y[:, sl]
        if h < nqk:
            t = t * lax.rsqrt(jnp.sum(t * t, axis=-1, keepdims=True) + NORM_EPS)
            if h < GDN_HEADS:
                t = t * (GDN_K_DIM ** -0.5)
        qkv_ref[:, sl] = t
    ba = ba_ref[...].astype(F32)
    lane = lax.broadcasted_iota(jnp.int32, ba.shape, 1)
    sp_in = ba + dtb_ref[...]
    softplus = jnp.maximum(sp_in, 0.0) + jnp.log1p(jnp.exp(-jnp.abs(sp_in)))
    gb_ref[...] = jnp.where(lane < GDN_HEADS, jax.nn.sigmoid(ba), -jnp.exp(ea_ref[...]) * softplus)


def _gdn_prep(proj, conv_w, exp_a, dt_b, col_qkv, col_ba, seq, tm):
    n = proj.shape[0]
    w = 3 * GDN_HEADS * HEAD_W
    tiles_per_batch = seq // tm
    hb = tm // SUBLANES
    return pl.pallas_call(
        functools.partial(_gdn_prep_kernel, tiles_per_batch=tiles_per_batch),
        grid=(n // tm,),
        in_specs=[pl.BlockSpec((tm, w), lambda i: (i, col_qkv // w)),
                  pl.BlockSpec((SUBLANES, w), lambda i: (jnp.maximum(i * hb - 1, 0), col_qkv // w)),
                  pl.BlockSpec((tm, LANES), lambda i: (i, col_ba // LANES)),
                  pl.BlockSpec((GDN_CONV, w), lambda i: (0, 0)),
                  pl.BlockSpec((1, LANES), lambda i: (0, 0)),
                  pl.BlockSpec((1, LANES), lambda i: (0, 0))],
        out_specs=[pl.BlockSpec((tm, w), lambda i: (i, 0)),
                   pl.BlockSpec((tm, LANES), lambda i: (i, 0))],
        out_shape=[jax.ShapeDtypeStruct((n, w), F32), jax.ShapeDtypeStruct((n, LANES), F32)],
        compiler_params=_cparams("parallel"),
        name="gdn_prep",
    )(proj, proj, proj, conv_w, exp_a, dt_b)


def _gdn_intra_kernel(qkv_ref, gb_ref, u_ref, w_ref, qt_ref, ot_ref, kd_ref, eg_ref, *, tc):
    hw = GDN_HEADS * HEAD_W
    nc = tc // CHUNK
    gb = gb_ref[...]
    r_in_chunk = lax.broadcasted_iota(jnp.int32, gb.shape, 0) % CHUNK
    gc = gb
    sh = 1
    while sh < CHUNK:
        gc = gc + jnp.where(r_in_chunk >= sh, pltpu.roll(gc, sh, 0), 0.0)
        sh *= 2
    gc_t = gc.T

    row = lax.broadcasted_iota(jnp.int32, (CHUNK, CHUNK), 0)
    col = lax.broadcasted_iota(jnp.int32, (CHUNK, CHUNK), 1)
    incl = row >= col
    strict = row > col
    eye = (row == col).astype(F32)

    pairs = [(c, h) for c in range(nc) for h in range(GDN_HEADS)]
    st = {}
    for c, h in pairs:
        rs = slice(c * CHUNK, (c + 1) * CHUNK)
        hs = slice(h * HEAD_W, (h + 1) * HEAD_W)
        q = qkv_ref[rs, h * HEAD_W:(h + 1) * HEAD_W]
        k = qkv_ref[rs, hw + h * HEAD_W:hw + (h + 1) * HEAD_W]
        v = qkv_ref[rs, 2 * hw + h * HEAD_W:2 * hw + (h + 1) * HEAD_W]
        beta = gb[rs, h:h + 1]
        g_col = gc[rs, GDN_HEADS + h:GDN_HEADS + h + 1]
        g_row = gc_t[GDN_HEADS + h:GDN_HEADS + h + 1, rs]
        g_last = g_col[CHUNK - 1:CHUNK, :]
        k_beta = k * beta
        e_g = jnp.exp(g_col)
        kd_ref[rs, hs] = k * jnp.exp(g_last - g_col)
        eg_ref[c:c + 1, hs] = jnp.broadcast_to(jnp.exp(g_last), (1, HEAD_W))
        st[c, h] = dict(
            kb16=k_beta.astype(BF16), k16=k.astype(BF16), q16=q.astype(BF16),
            decay=jnp.exp(jnp.where(incl, g_col - g_row, NEG_BIG)),
            rhs=jnp.concatenate([v * beta, k_beta * e_g], axis=1).astype(BF16),
            qg=q * e_g)
    for p in pairs:
        s = st[p]
        s["lmat"] = jnp.where(strict, _mm_nt(s["kb16"], s["k16"]) * s["decay"], 0.0)
        s["attn"] = jnp.where(incl, _mm_nt(s["q16"], s["k16"]) * s["decay"], 0.0).astype(BF16)
        s["t"] = eye - jnp.where((row // 2) == (col // 2), s["lmat"], 0.0)
    b = 2
    while b < CHUNK:
        off = ((row // (2 * b)) == (col // (2 * b))) & ((row // b) % 2 == 1) & ((col // b) % 2 == 0)
        for p in pairs:
            s = st[p]
            s["ct"] = _mm(jnp.where(off, s["lmat"], 0.0), s["t"])
        for p in pairs:
            s = st[p]
            s["t"] = s["t"] - _mm(s["t"], s["ct"])
        b *= 2
    for p in pairs:
        s = st[p]
        s["uw"] = _mm(s["t"], s["rhs"])
    for c, h in pairs:
        s = st[c, h]
        rs = slice(c * CHUNK, (c + 1) * CHUNK)
        hs = slice(h * HEAD_W, (h + 1) * HEAD_W)
        a_uw = _mm(s["attn"], s["uw"])
        u_ref[rs, hs] = s["uw"][:, :HEAD_W]
        w_ref[rs, hs] = s["uw"][:, HEAD_W:].astype(BF16)
        ot_ref[rs, hs] = a_uw[:, :HEAD_W]
        qt_ref[rs, hs] = (s["qg"] - a_uw[:, HEAD_W:]).astype(BF16)


def _gdn_intra(qkv, gb, tc):
    n = qkv.shape[0]
    hw = GDN_HEADS * HEAD_W
    nc = tc // CHUNK
    blk = pl.BlockSpec((tc, hw), lambda i: (i, 0))
    return pl.pallas_call(
        functools.partial(_gdn_intra_kernel, tc=tc),
        grid=(n // tc,),
        in_specs=[pl.BlockSpec((tc, 3 * hw), lambda i: (i, 0)),
                  pl.BlockSpec((tc, LANES), lambda i: (i, 0))],
        out_specs=[blk, blk, blk, blk, blk, pl.BlockSpec((nc, hw), lambda i: (i, 0))],
        out_shape=[jax.ShapeDtypeStruct((n, hw), F32), jax.ShapeDtypeStruct((n, hw), BF16),
                   jax.ShapeDtypeStruct((n, hw), BF16), jax.ShapeDtypeStruct((n, hw), F32),
                   jax.ShapeDtypeStruct((n, hw), F32), jax.ShapeDtypeStruct((n // CHUNK, hw), F32)],
        compiler_params=_cparams("parallel"),
        name="gdn_intra",
    )(qkv, gb)


def _gdn_scan_kernel(u_ref, w_ref, qt_ref, ot_ref, kd_ref, eg_ref, z_ref, ow_ref, o_ref, state_ref, *, tc, batch):
    @pl.when(pl.program_id(0) == 0)
    def _():
        state_ref[...] = jnp.zeros(state_ref.shape, F32)

    chains = [(b, h) for b in range(batch) for h in range(GDN_HEADS)]
    for c in range(tc // CHUNK):
        rs = slice(c * CHUNK, (c + 1) * CHUNK)
        prod, v_new = {}, {}
        for b, h in chains:
            hs = slice(h * HEAD_W, (h + 1) * HEAD_W)
            lhs = jnp.concatenate([w_ref[b, rs, hs], qt_ref[b, rs, hs]], axis=0)
            prod[b, h] = jnp.dot(lhs, state_ref[b * GDN_HEADS + h].astype(BF16), preferred_element_type=F32)
        for b, h in chains:
            hs = slice(h * HEAD_W, (h + 1) * HEAD_W)
            v_new[b, h] = (u_ref[b, rs, hs] - prod[b, h][:CHUNK]).astype(BF16)
        for b, h in chains:
            hs = slice(h * HEAD_W, (h + 1) * HEAD_W)
            idx = b * GDN_HEADS + h
            k_t = kd_ref[b, rs, hs].T.astype(BF16)
            state_ref[idx] = (state_ref[idx] * eg_ref[b, c:c + 1, hs]
                              + jnp.dot(k_t, v_new[b, h], preferred_element_type=F32))
        for b, h in chains:
            hs = slice(h * HEAD_W, (h + 1) * HEAD_W)
            o = prod[b, h][CHUNK:] + ot_ref[b, rs, hs]
            o = o * lax.rsqrt(jnp.mean(o * o, axis=-1, keepdims=True) + NORM_EPS) * ow_ref[...]
            z = z_ref[b, rs, hs].astype(F32)
            o_ref[b, rs, hs] = (o * (z * jax.nn.sigmoid(z))).astype(o_ref.dtype)


def _gdn_scan(u, w, qt, ot, kd, eg, proj, onorm_w, col_z, batch, seq, tc):
    hw = GDN_HEADS * HEAD_W
    nc = tc // CHUNK
    r3 = lambda t: t.reshape(batch, t.shape[0] // batch, t.shape[1])
    blk = pl.BlockSpec((batch, tc, hw), lambda i: (0, i, 0))
    out = pl.pallas_call(
        functools.partial(_gdn_scan_kernel, tc=tc, batch=batch),
        grid=(seq // tc,),
        in_specs=[blk, blk, blk, blk, blk,
                  pl.BlockSpec((batch, nc, hw), lambda i: (0, i, 0)),
                  pl.BlockSpec((batch, tc, hw), lambda i: (0, i, col_z // hw)),
                  pl.BlockSpec((1, HEAD_W), lambda i: (0, 0))],
        out_specs=blk,
        out_shape=jax.ShapeDtypeStruct((batch, seq, hw), BF16),
        scratch_shapes=[pltpu.VMEM((batch * GDN_HEADS, GDN_K_DIM, GDN_V_DIM), F32)],
        compiler_params=_cparams("arbitrary"),
        name="gdn_scan",
    )(r3(u), r3(w), r3(qt), r3(ot), r3(kd), r3(eg), r3(proj), onorm_w)
    return out.reshape(batch * seq, hw)


def _merge_kernel(x_ref, oa_ref, ob_ref, oc_ref, gate_ref, wa_ref, wb_ref, wc_ref, wout_ref, mod_ref, o_ref):
    d = x_ref.shape[1]
    merged = None
    for idx, (o_r, w_r) in enumerate(((oa_ref, wa_ref), (ob_ref, wb_ref), (oc_ref, wc_ref))):
        y = jnp.dot(o_r[...], w_r[...], preferred_element_type=F32)
        g = jax.nn.sigmoid(gate_ref[:, idx * d:(idx + 1) * d].astype(F32))
        merged = g * y if merged is None else merged + g * y
    upd = jnp.dot(merged.astype(BF16), wout_ref[...], preferred_element_type=F32)
    o_ref[...] = x_ref[...] + mod_ref[2:3, :] * upd


def _merge(x2, oa, ob, oc, proj, wa, wb, wc, wout, mod, col_gate, seq, tm):
    n, d = x2.shape
    w = DIFF_HEADS * HEAD_W
    tiles_per_batch = seq // tm
    row = lambda i: (i, 0)
    const = lambda i: (0, 0)
    return pl.pallas_call(
        _merge_kernel,
        grid=(n // tm,),
        in_specs=[pl.BlockSpec((tm, d), row), pl.BlockSpec((tm, w), row), pl.BlockSpec((tm, w), row),
                  pl.BlockSpec((tm, w), row),
                  pl.BlockSpec((tm, N_BRANCHES * d), lambda i: (i, col_gate // (N_BRANCHES * d))),
                  pl.BlockSpec((w, d), const), pl.BlockSpec((w, d), const), pl.BlockSpec((w, d), const),
                  pl.BlockSpec((d, d), const),
                  pl.BlockSpec((None, 6, d), lambda i: (i // tiles_per_batch, 0, 0))],
        out_specs=pl.BlockSpec((tm, d), row),
        out_shape=jax.ShapeDtypeStruct((n, d), F32),
        compiler_params=_cparams("parallel"),
        name="merge",
    )(x2, oa, ob, oc, proj, wa, wb, wc, wout, mod)


def _ffn_kernel(x_ref, nw_ref, mod_ref, wgu_ref, wd_ref, o_ref, h_ref, *, th):
    hidden = wd_ref.shape[0]
    h_ref[...] = _norm_mod(x_ref[...], nw_ref[...], mod_ref[...], 3, 4).astype(BF16)

    def gate_up(c):
        h = h_ref[...]
        return (jnp.dot(h, wgu_ref[:, c * th:(c + 1) * th], preferred_element_type=F32),
                jnp.dot(h, wgu_ref[:, hidden + c * th:hidden + (c + 1) * th], preferred_element_type=F32))

    acc = None
    nxt = gate_up(0)
    for c in range(hidden // th):
        gate, up = nxt
        if c + 1 < hidden // th:
            nxt = gate_up(c + 1)
        act = (gate * jax.nn.sigmoid(gate) * up).astype(BF16)
        part = jnp.dot(act, wd_ref[c * th:(c + 1) * th, :], preferred_element_type=F32)
        acc = part if acc is None else acc + part
    o_ref[...] = x_ref[...] + mod_ref[5:6, :] * acc


def _ffn(x2, nw, mod, w_gu, w_down, seq, tm, th):
    n, d = x2.shape
    hidden = w_down.shape[0]
    tiles_per_batch = seq // tm
    resident = pl.Buffered(1)
    return pl.pallas_call(
        functools.partial(_ffn_kernel, th=th),
        grid=(n // tm,),
        in_specs=[pl.BlockSpec((tm, d), lambda i: (i, 0)),
                  pl.BlockSpec((1, d), lambda i: (0, 0)),
                  pl.BlockSpec((None, 6, d), lambda i: (i // tiles_per_batch, 0, 0)),
                  pl.BlockSpec((d, 2 * hidden), lambda i: (0, 0), pipeline_mode=resident),
                  pl.BlockSpec((hidden, d), lambda i: (0, 0), pipeline_mode=resident)],
        out_specs=pl.BlockSpec((tm, d), lambda i: (i, 0)),
        out_shape=jax.ShapeDtypeStruct((n, d), F32),
        scratch_shapes=[pltpu.VMEM((tm, d), BF16)],
        compiler_params=_cparams("parallel"),
        name="ffn",
    )(x2, nw, mod, w_gu, w_down)


def _in_proj_layout(d_model):
    a = DIFF_HEADS * HEAD_W
    g = GDN_HEADS * HEAD_W
    cols = {}
    off = 0
    for name, width in (("gate", N_BRANCHES * d_model), ("c_qkv", 3 * g), ("a_q", a), ("a_k", a), ("a_v", a),
                        ("c_z", g), ("c_ba", LANES), ("b_q", MLA_Q_LORA), ("b_kv", MLA_Q_LORA)):
        if off % width:
            off += width - off % width
        cols[name] = off
        off += width
    return cols, off


def _build_w_in(w_in, d_model, total):
    a = DIFF_HEADS * 2 * DIFF_QK_DIM
    g = GDN_HEADS * GDN_K_DIM
    widths = (a, a, DIFF_HEADS * DIFF_V_DIM, MLA_Q_LORA, MLA_KV_LORA + MLA_ROPE, g, g, GDN_HEADS * GDN_V_DIM,
              GDN_HEADS * GDN_V_DIM, GDN_HEADS, GDN_HEADS, N_BRANCHES * d_model)
    offs = np.concatenate([[0], np.cumsum(widths)])
    part = [w_in[:, int(offs[i]):int(offs[i + 1])] for i in range(len(widths))]
    cols, _ = _in_proj_layout(d_model)
    out = jnp.zeros((w_in.shape[0], total), w_in.dtype)
    place = (("a_q", part[0]), ("a_k", part[1]), ("a_v", part[2]), ("b_q", part[3]), ("b_kv", part[4]),
             ("c_qkv", jnp.concatenate(part[5:8], axis=1)), ("c_z", part[8]), ("gate", part[11]),
             ("c_ba", jnp.concatenate(part[9:11], axis=1)))
    for name, p in place:
        out = lax.dynamic_update_slice(out, p, (0, cols[name]))
    return out.astype(BF16)


def _head_pad(w, heads, per_head, lane_off=0):
    r = w.shape[0]
    w = w.reshape(r, heads, per_head)
    w = jnp.pad(w, ((0, 0), (0, 0), (lane_off, HEAD_W - per_head - lane_off)))
    return w.reshape(r, heads * HEAD_W)


def _tiles(seq):
    key_block = min(512, seq)
    return dict(
        rows=min(1024, seq),
        rows_resident=min(512, seq),
        proj_cols=3840,
        ffn_chunk=256,
        key_block=key_block,
        diff_q=min(2 * key_block, seq),
        mla_q=min(4 * key_block, seq),
        row_group=256,
        gdn_block=min(512, seq),
    )


def kernel(x, c, positions, ada_w, ada_b, norm1_w, w_in, diff_qnorm_w, diff_knorm_w, diff_lambda, diff_subln_w, w_o_diff, mla_qa_norm_w, mla_q_up, mla_kva_norm_w, mla_kv_up, mla_qnorm_w, mla_knorm_w, w_o_mla, gdn_conv_w, gdn_a_log, gdn_dt_bias, gdn_onorm_w, w_o_gdn, w_out, norm2_w, ffn_w_gu, ffn_w_down):
    batch, seq, d = x.shape
    depth = ada_w.shape[0]
    n = batch * seq
    qk_dim = MLA_NOPE + MLA_ROPE

    t = _tiles(seq)
    tm, tc = t["rows"], t["gdn_block"]

    c_pad = jnp.pad(c, ((0, SUBLANES - batch), (0, 0)))
    mod_all = _ada_mod(c_pad, ada_w, ada_b)[:, :batch].reshape(depth, batch, 6, d)

    cols, total = _in_proj_layout(d)
    tn = t["proj_cols"]
    total = -(-total // tn) * tn

    pos2 = positions.reshape(n, 1)
    half = MLA_ROPE // 2
    inv_freq = ROPE_THETA ** (-jnp.arange(half, dtype=F32) / half)
    invf = jnp.zeros((1, HEAD_W), F32).at[0, MLA_NOPE:qk_dim].set(jnp.concatenate([inv_freq, inv_freq]))

    x2 = x.reshape(n, d)
    for l in range(depth):
        lambda_init = 0.8 - 0.6 * math.exp(-0.3 * l)
        mod = mod_all[l]
        proj = _in_proj(x2, norm1_w[l][None], mod, _build_w_in(w_in[l], d, total), seq, tm, tn)

        qw2 = jnp.tile(diff_qnorm_w[l], 2)[None]
        kw2 = jnp.tile(diff_knorm_w[l], 2)[None]
        a_q, a_k = _diff_prep(proj, qw2, kw2, cols["a_q"], cols["a_k"], tm)
        o_a = _attention(a_q, a_k, proj, (diff_lambda[l], diff_subln_w[l][None]), heads=DIFF_HEADS, batch=batch,
                         seq=seq, tq=t["diff_q"], tk=t["key_block"], rg=t["row_group"], n_maps=2,
                         v_col=cols["a_v"], lambda_init=lambda_init)

        kv_up = mla_kv_up[l].reshape(MLA_KV_LORA, MLA_HEADS, MLA_NOPE + MLA_V_DIM)
        wk_nope = _head_pad(kv_up[:, :, :MLA_NOPE].reshape(MLA_KV_LORA, -1), MLA_HEADS, MLA_NOPE)
        place_rope = _head_pad(jnp.tile(jnp.eye(MLA_ROPE, dtype=F32), (1, MLA_HEADS)), MLA_HEADS, MLA_ROPE, MLA_NOPE)
        pad_rows = MLA_Q_LORA - MLA_KV_LORA - MLA_ROPE
        wk = jnp.concatenate([wk_nope, place_rope, jnp.zeros((pad_rows, MLA_HEADS * HEAD_W), F32)], axis=0)
        wv = jnp.pad(kv_up[:, :, MLA_NOPE:].reshape(MLA_KV_LORA, -1), ((0, MLA_Q_LORA - MLA_KV_LORA), (0, 0)))
        kva_w = jnp.pad(mla_kva_norm_w[l], (0, MLA_Q_LORA - MLA_KV_LORA))[None]
        b_q, b_k, b_v = _mla_prep(
            proj, pos2, mla_qa_norm_w[l][None], _head_pad(mla_q_up[l], MLA_HEADS, qk_dim).astype(BF16),
            kva_w, wk.astype(BF16), wv.astype(BF16),
            jnp.pad(mla_qnorm_w[l], (0, HEAD_W - qk_dim))[None], jnp.pad(mla_knorm_w[l], (0, HEAD_W - qk_dim))[None],
            invf, cols["b_q"], cols["b_kv"], tm)
        o_b = _attention(b_q, b_k, b_v, (), heads=MLA_HEADS, batch=batch, seq=seq, tq=t["mla_q"],
                         tk=t["key_block"], rg=t["row_group"], n_maps=1)

        exp_a = jnp.zeros((1, LANES), F32).at[0, GDN_HEADS:2 * GDN_HEADS].set(gdn_a_log[l])
        dt_b = jnp.zeros((1, LANES), F32).at[0, GDN_HEADS:2 * GDN_HEADS].set(gdn_dt_bias[l])
        c_qkv, c_gb = _gdn_prep(proj, gdn_conv_w[l], exp_a, dt_b, cols["c_qkv"], cols["c_ba"], seq, tm)
        g_u, g_w, g_qt, g_ot, g_kd, g_eg = _gdn_intra(c_qkv, c_gb, tc)
        o_c = _gdn_scan(g_u, g_w, g_qt, g_ot, g_kd, g_eg, proj, gdn_onorm_w[l][None], cols["c_z"], batch, seq, tc)

        x2 = _merge(x2, o_a, o_b, o_c, proj, w_o_diff[l].astype(BF16), w_o_mla[l].astype(BF16),
                    w_o_gdn[l].astype(BF16), w_out[l].astype(BF16), mod, cols["gate"], seq,
                    t["rows_resident"])
        x2 = _ffn(x2, norm2_w[l][None], mod, ffn_w_gu[l].astype(BF16), ffn_w_down[l].astype(BF16), seq,
                  t["rows_resident"], t["ffn_chunk"])
    return x2.reshape(batch, seq, d)
```

```python
import functools
import math

import jax
import jax.numpy as jnp
import numpy as np
from jax import lax
from jax.experimental import pallas as pl
from jax.experimental.pallas import tpu as pltpu

F32 = jnp.float32
BF16 = jnp.bfloat16

CHUNK = 64
NORM_EPS = 1e-6
ROPE_THETA = 10000.0
DIFF_HEADS = 4
DIFF_QK_DIM = 64
DIFF_V_DIM = 128
MLA_HEADS = 4
MLA_Q_LORA = 384
MLA_KV_LORA = 256
MLA_NOPE = 64
MLA_ROPE = 32
MLA_V_DIM = 128
GDN_HEADS = 4
GDN_K_DIM = 128
GDN_V_DIM = 128
GDN_CONV = 4
N_BRANCHES = 3

LANES = 128
SUBLANES = 8
HEAD_W = 128
LOG2E = 1.4426950408889634
NEG_BIG = -1e30
VMEM_LIMIT = 56 * 1024 * 1024
ATTN_UNROLL = 4


def _cparams(*sem):
    return pltpu.CompilerParams(dimension_semantics=sem, vmem_limit_bytes=VMEM_LIMIT)


def _mm(a, b):
    return jnp.dot(a.astype(BF16), b.astype(BF16), preferred_element_type=F32)


def _mm_nt(a, b):
    return lax.dot_general(a.astype(BF16), b.astype(BF16), (((1,), (1,)), ((), ())),
                           preferred_element_type=F32)


def _mod_kernel(c_ref, w_ref, b_ref, o_ref):
    o_ref[...] = jnp.dot(c_ref[...], w_ref[...], preferred_element_type=F32) + b_ref[...]


def _ada_mod(c_pad, ada_w, ada_b):
    depth, d, n6 = ada_w.shape
    tn = 1536
    return pl.pallas_call(
        _mod_kernel,
        grid=(depth, n6 // tn),
        in_specs=[pl.BlockSpec((SUBLANES, d), lambda l, j: (0, 0)),
                  pl.BlockSpec((None, d, tn), lambda l, j: (l, 0, j)),
                  pl.BlockSpec((None, 1, tn), lambda l, j: (l, 0, j))],
        out_specs=pl.BlockSpec((None, SUBLANES, tn), lambda l, j: (l, 0, j)),
        out_shape=jax.ShapeDtypeStruct((depth, SUBLANES, n6), F32),
        compiler_params=_cparams("parallel", "parallel"),
        name="ada_mod",
    )(c_pad, ada_w, ada_b.reshape(depth, 1, n6))


def _norm_mod(x, nw, mod, shift_row, scale_row):
    r = lax.rsqrt(jnp.mean(x * x, axis=-1, keepdims=True) + NORM_EPS)
    h = x * r * nw
    return h * (1.0 + mod[scale_row:scale_row + 1, :]) + mod[shift_row:shift_row + 1, :]


def _in_proj_kernel(x_ref, nw_ref, mod_ref, w_ref, qw_ref, kw_ref, o_ref, h_ref, *, qk_tile, qk_off):
    j = pl.program_id(1)
    qk_w = 2 * DIFF_HEADS * HEAD_W

    @pl.when(j == 0)
    def _():
        h_ref[...] = _norm_mod(x_ref[...], nw_ref[...], mod_ref[...], 0, 1).astype(BF16)

    @pl.when(j != qk_tile)
    def _():
        o_ref[...] = jnp.dot(h_ref[...], w_ref[...], preferred_element_type=F32).astype(o_ref.dtype)

    @pl.when(j == qk_tile)
    def _():
        h = h_ref[...]
        qk = jnp.dot(h, w_ref[:, qk_off:qk_off + qk_w], preferred_element_type=F32)
        if qk_off:
            o_ref[:, :qk_off] = jnp.dot(h, w_ref[:, :qk_off], preferred_element_type=F32).astype(o_ref.dtype)
        o_ref[:, qk_off + qk_w:] = jnp.dot(h, w_ref[:, qk_off + qk_w:],
                                            preferred_element_type=F32).astype(o_ref.dtype)
        qscale = (DIFF_QK_DIM ** -0.5) * LOG2E
        for hd in range(2 * DIFF_HEADS):
            sl = slice(hd * HEAD_W, (hd + 1) * HEAD_W)
            if hd < DIFF_HEADS:
                t = _half_rms(qk[:, sl], qw_ref[...]) * qscale
            else:
                t = _half_rms(qk[:, sl], kw_ref[...])
            o_ref[:, qk_off + hd * HEAD_W:qk_off + (hd + 1) * HEAD_W] = t.astype(o_ref.dtype)


def _in_proj(x2, nw, mod, w, qw2, kw2, seq, tm, tn, col_q):
    n, d = x2.shape
    cols = w.shape[1]
    tiles_per_batch = seq // tm
    return pl.pallas_call(
        functools.partial(_in_proj_kernel, qk_tile=col_q // tn, qk_off=col_q % tn),
        grid=(n // tm, cols // tn),
        in_specs=[pl.BlockSpec((tm, d), lambda i, j: (i, 0)),
                  pl.BlockSpec((1, d), lambda i, j: (0, 0)),
                  pl.BlockSpec((None, 6, d), lambda i, j: (i // tiles_per_batch, 0, 0)),
                  pl.BlockSpec((d, tn), lambda i, j: (0, j)),
                  pl.BlockSpec((1, HEAD_W), lambda i, j: (0, 0)),
                  pl.BlockSpec((1, HEAD_W), lambda i, j: (0, 0))],
        out_specs=pl.BlockSpec((tm, tn), lambda i, j: (i, j)),
        out_shape=jax.ShapeDtypeStruct((n, cols), BF16),
        scratch_shapes=[pltpu.VMEM((tm, d), BF16)],
        compiler_params=_cparams("parallel", "arbitrary"),
        name="in_proj",
    )(x2, nw, mod, w, qw2, kw2)


def _half_rms(t, w):
    lane = lax.broadcasted_iota(jnp.int32, t.shape, 1)
    lo = lane < DIFF_QK_DIM
    sq = t * t
    s_lo = jnp.sum(jnp.where(lo, sq, 0.0), axis=-1, keepdims=True)
    s_hi = jnp.sum(jnp.where(lo, 0.0, sq), axis=-1, keepdims=True)
    r = lax.rsqrt(jnp.where(lo, s_lo, s_hi) * (1.0 / DIFF_QK_DIM) + NORM_EPS)
    return t * r * w


def _rope_tile(t, cos_t, sin_lo, sin_hi):
    half = MLA_ROPE // 2
    return (t * cos_t + pltpu.roll(t, HEAD_W - half, 1) * sin_lo + pltpu.roll(t, half, 1) * sin_hi)


def _mla_prep_kernel(q_ref, kv_ref, pos_ref, qa_ref, qup_ref, kva_ref, wk_ref, wv_ref,
                     qn_ref, kn_ref, invf_ref, qo_ref, ko_ref, vo_ref):
    qk_dim = MLA_NOPE + MLA_ROPE
    qscale = (qk_dim ** -0.5) * LOG2E
    half = MLA_ROPE // 2

    xq = q_ref[...].astype(F32)
    rq = lax.rsqrt(jnp.mean(xq * xq, axis=-1, keepdims=True) + NORM_EPS)
    q = _mm(xq * rq * qa_ref[...], qup_ref[...])

    xkv = kv_ref[...].astype(F32)
    lane_kv = lax.broadcasted_iota(jnp.int32, xkv.shape, 1)
    is_lat = lane_kv < MLA_KV_LORA
    ssq = jnp.sum(jnp.where(is_lat, xkv * xkv, 0.0), axis=-1, keepdims=True)
    rkv = lax.rsqrt(ssq * (1.0 / MLA_KV_LORA) + NORM_EPS)
    lhs = jnp.where(is_lat, xkv * rkv * kva_ref[...], xkv).astype(BF16)
    k = jnp.dot(lhs, wk_ref[...], preferred_element_type=F32)
    vo_ref[...] = jnp.dot(lhs, wv_ref[...], preferred_element_type=F32).astype(BF16)

    ang = pos_ref[...].astype(F32) * invf_ref[...]
    lane = lax.broadcasted_iota(jnp.int32, ang.shape, 1)
    in_lo = (lane >= MLA_NOPE) & (lane < MLA_NOPE + half)
    in_hi = (lane >= MLA_NOPE + half) & (lane < qk_dim)
    cos_a, sin_a = jnp.cos(ang), jnp.sin(ang)
    cos_t = jnp.where(in_lo | in_hi, cos_a, 1.0)
    sin_lo = jnp.where(in_lo, -sin_a, 0.0)
    sin_hi = jnp.where(in_hi, sin_a, 0.0)

    for h in range(MLA_HEADS):
        sl = slice(h * HEAD_W, (h + 1) * HEAD_W)
        qh = q[:, sl]
        qh = qh * lax.rsqrt(jnp.sum(qh * qh, axis=-1, keepdims=True) * (1.0 / qk_dim) + NORM_EPS)
        qh = _rope_tile(qh * qn_ref[...], cos_t, sin_lo, sin_hi)
        qo_ref[:, sl] = (qh * qscale).astype(BF16)
        kh = k[:, sl]
        kh = kh * lax.rsqrt(jnp.sum(kh * kh, axis=-1, keepdims=True) * (1.0 / qk_dim) + NORM_EPS)
        kh = _rope_tile(kh * kn_ref[...], cos_t, sin_lo, sin_hi)
        ko_ref[:, sl] = kh.astype(BF16)


def _mla_prep(proj, pos2, qa_w, qup, kva_w, wk, wv, qn_w, kn_w, invf, col_q, col_kv, tm):
    n = proj.shape[0]
    w = MLA_HEADS * HEAD_W
    lat = MLA_Q_LORA
    const = lambda i: (0, 0)
    out = jax.ShapeDtypeStruct((n, w), BF16)
    return pl.pallas_call(
        _mla_prep_kernel,
        grid=(n // tm,),
        in_specs=[pl.BlockSpec((tm, lat), lambda i: (i, col_q // lat)),
                  pl.BlockSpec((tm, lat), lambda i: (i, col_kv // lat)),
                  pl.BlockSpec((tm, 1), lambda i: (i, 0)),
                  pl.BlockSpec((1, lat), const),
                  pl.BlockSpec((lat, w), const),
                  pl.BlockSpec((1, lat), const),
                  pl.BlockSpec((lat, w), const),
                  pl.BlockSpec((lat, w), const),
                  pl.BlockSpec((1, HEAD_W), const),
                  pl.BlockSpec((1, HEAD_W), const),
                  pl.BlockSpec((1, HEAD_W), const)],
        out_specs=[pl.BlockSpec((tm, w), lambda i: (i, 0))] * 3,
        out_shape=[out, out, out],
        compiler_params=_cparams("parallel"),
        name="mla_prep",
    )(proj, proj, pos2, qa_w, qup, kva_w, wk, wv, qn_w, kn_w, invf)


def _attn_kernel(*refs, tq, tk, rg, n_maps, lambda_init, unroll):
    if n_maps == 2:
        q_ref, k_ref, v_ref, lam_ref, sw_ref, o_ref, qq_ref, s_ref, cdiff_ref, mx_ref, m_ref, l_ref, acc_ref = refs
    else:
        q_ref, k_ref, v_ref, o_ref, s_ref, cdiff_ref, mx_ref, m_ref, l_ref, acc_ref = refs
        qq_ref = q_ref
    i = pl.program_id(2)
    rows = n_maps * tq
    if n_maps == 2:
        q = q_ref[...]
        lane = lax.broadcasted_iota(jnp.int32, q.shape, 1)
        zero = jnp.zeros_like(q)
        qq_ref[:tq, :] = jnp.where(lane < DIFF_QK_DIM, q, zero)
        qq_ref[tq:, :] = jnp.where(lane >= DIFF_QK_DIM, q, zero)

    m_ref[...] = jnp.full(m_ref.shape, NEG_BIG, F32)
    l_ref[...] = jnp.zeros(l_ref.shape, F32)
    acc_ref[...] = jnp.zeros(acc_ref.shape, F32)

    n_groups = rows // rg
    n_full = (i * tq) // tk
    n_diag = max(tq // tk, 1)

    cdiff_ref[...] = (lax.broadcasted_iota(jnp.int32, (rg, tk), 1) // CHUNK
                      - lax.broadcasted_iota(jnp.int32, (rg, tk), 0) // CHUNK)

    def produce(j, slot, g, diag):
        rsl = slice(g * rg, (g + 1) * rg)
        start = pl.multiple_of(j * tk, tk)
        s = lax.dot_general(qq_ref[rsl, :], k_ref[pl.ds(start, tk), :],
                            (((1,), (1,)), ((), ())), preferred_element_type=F32)
        if diag is not None:
            s = jnp.where(cdiff_ref[...] <= ((g * rg) % tq - diag * tk) // CHUNK, s, NEG_BIG)
        s_ref[slot, rsl, :] = s
        mx_ref[slot, rsl, :] = jnp.broadcast_to(jnp.max(s, axis=-1, keepdims=True), (rg, LANES))

    def consume(j, slot, g, v):
        rsl = slice(g * rg, (g + 1) * rg)
        m_prev = m_ref[rsl, :]
        m_new = jnp.maximum(m_prev, mx_ref[slot, rsl, :])
        alpha = jnp.exp2(m_prev - m_new)
        p = jnp.exp2(s_ref[slot, rsl, :] - jnp.tile(m_new, (1, tk // LANES)))
        p_sum = p[:, :LANES]
        for t in range(1, tk // LANES):
            p_sum = p_sum + p[:, t * LANES:(t + 1) * LANES]
        l_ref[rsl, :] = alpha * l_ref[rsl, :] + p_sum
        m_ref[rsl, :] = m_new
        acc_ref[rsl, :] = alpha * acc_ref[rsl, :] + jnp.dot(p.astype(BF16), v, preferred_element_type=F32)

    FULL = "full"

    def step(j, slot, nxt, cur_diag=None):
        start = pl.multiple_of(j * tk, tk)
        v = v_ref[pl.ds(start, tk), :]
        next_diag = None if nxt == FULL else nxt
        for g in range(n_groups):
            if nxt is not None and live(g, next_diag):
                produce(j + 1, 1 - slot, g, next_diag)
            if live(g, cur_diag):
                consume(j, slot, g, v)

    def live(g, diag):
        return diag is None or (g * rg) % tq + rg > diag * tk

    @pl.when(n_full == 0)
    def _():
        for g in range(n_groups):
            produce(0, 0, g, 0)

    @pl.when(n_full > 0)
    def _():
        for g in range(n_groups):
            produce(0, 0, g, None)

    def body(jj, carry):
        for u in range(unroll):
            step(unroll * jj + u, u % 2, FULL)
        return carry

    n_loop = jnp.maximum((n_full - 1) // unroll, 0)
    lax.fori_loop(0, n_loop, body, 0)
    left = n_full - unroll * n_loop

    def tail(first_slot):
        for d in range(n_diag):
            step(n_full + d, (first_slot + d) % 2, d + 1 if d < n_diag - 1 else None, cur_diag=d)

    @pl.when(left == 0)
    def _():
        tail(0)

    stride = tq // tk if tq % tk == 0 else 1
    for cnt in sorted({m * stride - unroll * ((m * stride - 1) // unroll) for m in range(1, 2 * unroll + 1)}):
        @pl.when(left == cnt)
        def _(cnt=cnt):
            for u in range(cnt):
                step(n_full - cnt + u, u % 2, FULL if u < cnt - 1 else 0)
            tail(cnt % 2)

    o = acc_ref[...] / jnp.sum(l_ref[...], axis=-1, keepdims=True)
    if n_maps == 2:
        lp = lam_ref[...]
        lam = (jnp.exp(jnp.sum(lp[0:1] * lp[1:2], axis=-1, keepdims=True))
               - jnp.exp(jnp.sum(lp[2:3] * lp[3:4], axis=-1, keepdims=True)) + lambda_init)
        o = o[:tq] - lam * o[tq:]
        r = lax.rsqrt(jnp.mean(o * o, axis=-1, keepdims=True) + NORM_EPS)
        o = o * r * sw_ref[...] * (1.0 - lambda_init)
    o_ref[...] = o.astype(o_ref.dtype)


def _attention(q, k, v, extra, *, heads, batch, seq, tq, tk, rg, n_maps, q_col=0, k_col=0, v_col=0,
               lambda_init=0.0):
    q3, k3, v3 = (t.reshape(batch, seq, t.shape[1]) for t in (q, k, v))
    rows = n_maps * tq
    rg = min(rg, tq)
    qb, kb, vb = q_col // HEAD_W, k_col // HEAD_W, v_col // HEAD_W
    scratch = [pltpu.VMEM((2, rows, tk), F32), pltpu.VMEM((rg, tk), jnp.int32), pltpu.VMEM((2, rows, LANES), F32), pltpu.VMEM((rows, LANES), F32),
               pltpu.VMEM((rows, LANES), F32), pltpu.VMEM((rows, HEAD_W), F32)]
    if n_maps == 2:
        scratch.insert(0, pltpu.VMEM((rows, HEAD_W), BF16))
    in_specs = [pl.BlockSpec((None, tq, HEAD_W), lambda b, h, i: (b, i, qb + h)),
                pl.BlockSpec((None, seq, HEAD_W), lambda b, h, i: (b, 0, kb + h)),
                pl.BlockSpec((None, seq, HEAD_W), lambda b, h, i: (b, 0, vb + h))]
    for e in extra:
        in_specs.append(pl.BlockSpec(e.shape, lambda b, h, i: (0, 0)))
    out = pl.pallas_call(
        functools.partial(_attn_kernel, tq=tq, tk=tk, rg=rg, n_maps=n_maps, lambda_init=lambda_init,
                          unroll=ATTN_UNROLL),
        grid=(batch, heads, seq // tq),
        in_specs=in_specs,
        out_specs=pl.BlockSpec((None, tq, HEAD_W), lambda b, h, i: (b, i, h)),
        out_shape=jax.ShapeDtypeStruct((batch, seq, heads * HEAD_W), BF16),
        scratch_shapes=scratch,
        compiler_params=_cparams("parallel", "parallel", "arbitrary"),
        name="diff_attn" if n_maps == 2 else "mla_attn",
    )(q3, k3, v3, *extra)
    return out.reshape(batch * seq, heads * HEAD_W)


def _gdn_prep_kernel(x_ref, halo_ref, ba_ref, cw_ref, ea_ref, dtb_ref, qkv_ref, gb_ref, *, tiles_per_batch):
    i = pl.program_id(0)
    x = x_ref[...].astype(F32)
    halo = halo_ref[...].astype(F32)
    halo = jnp.where(i % tiles_per_batch == 0, 0.0, halo)
    cw = cw_ref[...]
    row8 = lax.broadcasted_iota(jnp.int32, halo.shape, 0)
    y = x * cw[GDN_CONV - 1:GDN_CONV]
    for s in range(1, GDN_CONV):
        xs = pltpu.roll(x, s, 0)
        hs = pltpu.roll(halo, s, 0)
        first = jnp.where(row8 < s, hs, xs[:SUBLANES])
        xs = jnp.concatenate([first, xs[SUBLANES:]], axis=0)
        y = y + xs * cw[GDN_CONV - 1 - s:GDN_CONV - s]
    y = y * jax.nn.sigmoid(y)
    nqk = 2 * GDN_HEADS
    for h in range(3 * GDN_HEADS):
        sl = slice(h * HEAD_W, (h + 1) * HEAD_W)
        t = y[:, sl]
        if h < nqk:
            t = t * lax.rsqrt(jnp.sum(t * t, axis=-1, keepdims=True) + NORM_EPS)
            if h < GDN_HEADS:
                t = t * (GDN_K_DIM ** -0.5)
        qkv_ref[:, sl] = t
    ba = ba_ref[...].astype(F32)
    lane = lax.broadcasted_iota(jnp.int32, ba.shape, 1)
    sp_in = ba + dtb_ref[...]
    softplus = jnp.maximum(sp_in, 0.0) + jnp.log1p(jnp.exp(-jnp.abs(sp_in)))
    gb_ref[...] = jnp.where(lane < GDN_HEADS, jax.nn.sigmoid(ba), -jnp.exp(ea_ref[...]) * softplus)


def _gdn_prep(proj, conv_w, exp_a, dt_b, col_qkv, col_ba, seq, tm):
    n = proj.shape[0]
    w = 3 * GDN_HEADS * HEAD_W
    tiles_per_batch = seq // tm
    hb = tm // SUBLANES
    return pl.pallas_call(
        functools.partial(_gdn_prep_kernel, tiles_per_batch=tiles_per_batch),
        grid=(n // tm,),
        in_specs=[pl.BlockSpec((tm, w), lambda i: (i, col_qkv // w)),
                  pl.BlockSpec((SUBLANES, w), lambda i: (jnp.maximum(i * hb - 1, 0), col_qkv // w)),
                  pl.BlockSpec((tm, LANES), lambda i: (i, col_ba // LANES)),
                  pl.BlockSpec((GDN_CONV, w), lambda i: (0, 0)),
                  pl.BlockSpec((1, LANES), lambda i: (0, 0)),
                  pl.BlockSpec((1, LANES), lambda i: (0, 0))],
        out_specs=[pl.BlockSpec((tm, w), lambda i: (i, 0)),
                   pl.BlockSpec((tm, LANES), lambda i: (i, 0))],
        out_shape=[jax.ShapeDtypeStruct((n, w), F32), jax.ShapeDtypeStruct((n, LANES), F32)],
        compiler_params=_cparams("parallel"),
        name="gdn_prep",
    )(proj, proj, proj, conv_w, exp_a, dt_b)


def _gdn_intra_kernel(qkv_ref, gb_ref, u_ref, w_ref, qt_ref, ot_ref, kd_ref, eg_ref, *, tc):
    hw = GDN_HEADS * HEAD_W
    nc = tc // CHUNK
    gb = gb_ref[...]
    r_in_chunk = lax.broadcasted_iota(jnp.int32, gb.shape, 0) % CHUNK
    gc = gb
    sh = 1
    while sh < CHUNK:
        gc = gc + jnp.where(r_in_chunk >= sh, pltpu.roll(gc, sh, 0), 0.0)
        sh *= 2
    gc_t = gc.T

    row = lax.broadcasted_iota(jnp.int32, (CHUNK, CHUNK), 0)
    col = lax.broadcasted_iota(jnp.int32, (CHUNK, CHUNK), 1)
    incl = row >= col
    strict = row > col
    eye = (row == col).astype(F32)

    pairs = [(c, h) for c in range(nc) for h in range(GDN_HEADS)]
    st = {}
    for c, h in pairs:
        rs = slice(c * CHUNK, (c + 1) * CHUNK)
        hs = slice(h * HEAD_W, (h + 1) * HEAD_W)
        q = qkv_ref[rs, h * HEAD_W:(h + 1) * HEAD_W]
        k = qkv_ref[rs, hw + h * HEAD_W:hw + (h + 1) * HEAD_W]
        v = qkv_ref[rs, 2 * hw + h * HEAD_W:2 * hw + (h + 1) * HEAD_W]
        beta = gb[rs, h:h + 1]
        g_col = gc[rs, GDN_HEADS + h:GDN_HEADS + h + 1]
        g_row = gc_t[GDN_HEADS + h:GDN_HEADS + h + 1, rs]
        g_last = g_col[CHUNK - 1:CHUNK, :]
        k_beta = k * beta
        e_g = jnp.exp(g_col)
        kd_ref[rs, hs] = k * jnp.exp(g_last - g_col)
        eg_ref[c:c + 1, hs] = jnp.broadcast_to(jnp.exp(g_last), (1, HEAD_W))
        st[c, h] = dict(
            kb16=k_beta.astype(BF16), k16=k.astype(BF16), q16=q.astype(BF16),
            decay=jnp.exp(jnp.where(incl, g_col - g_row, NEG_BIG)),
            rhs=jnp.concatenate([v * beta, k_beta * e_g], axis=1).astype(BF16),
            qg=q * e_g)
    for p in pairs:
        s = st[p]
        s["lmat"] = jnp.where(strict, _mm_nt(s["kb16"], s["k16"]) * s["decay"], 0.0)
        s["attn"] = jnp.where(incl, _mm_nt(s["q16"], s["k16"]) * s["decay"], 0.0).astype(BF16)
        s["t"] = eye - jnp.where((row // 2) == (col // 2), s["lmat"], 0.0)
    b = 2
    while b < CHUNK:
        off = ((row // (2 * b)) == (col // (2 * b))) & ((row // b) % 2 == 1) & ((col // b) % 2 == 0)
        for p in pairs:
            s = st[p]
            s["ct"] = _mm(jnp.where(off, s["lmat"], 0.0), s["t"])
        for p in pairs:
            s = st[p]
            s["t"] = s["t"] - _mm(s["t"], s["ct"])
        b *= 2
    for p in pairs:
        s = st[p]
        s["uw"] = _mm(s["t"], s["rhs"])
    for c, h in pairs:
        s = st[c, h]
        rs = slice(c * CHUNK, (c + 1) * CHUNK)
        hs = slice(h * HEAD_W, (h + 1) * HEAD_W)
        a_uw = _mm(s["attn"], s["uw"])
        u_ref[rs, hs] = s["uw"][:, :HEAD_W]
        w_ref[rs, hs] = s["uw"][:, HEAD_W:].astype(BF16)
        ot_ref[rs, hs] = a_uw[:, :HEAD_W]
        qt_ref[rs, hs] = (s["qg"] - a_uw[:, HEAD_W:]).astype(BF16)


def _gdn_intra(qkv, gb, tc):
    n = qkv.shape[0]
    hw = GDN_HEADS * HEAD_W
    nc = tc // CHUNK
    blk = pl.BlockSpec((tc, hw), lambda i: (i, 0))
    return pl.pallas_call(
        functools.partial(_gdn_intra_kernel, tc=tc),
        grid=(n // tc,),
        in_specs=[pl.BlockSpec((tc, 3 * hw), lambda i: (i, 0)),
                  pl.BlockSpec((tc, LANES), lambda i: (i, 0))],
        out_specs=[blk, blk, blk, blk, blk, pl.BlockSpec((nc, hw), lambda i: (i, 0))],
        out_shape=[jax.ShapeDtypeStruct((n, hw), F32), jax.ShapeDtypeStruct((n, hw), BF16),
                   jax.ShapeDtypeStruct((n, hw), BF16), jax.ShapeDtypeStruct((n, hw), F32),
                   jax.ShapeDtypeStruct((n, hw), F32), jax.ShapeDtypeStruct((n // CHUNK, hw), F32)],
        compiler_params=_cparams("parallel"),
        name="gdn_intra",
    )(qkv, gb)


def _gdn_scan_kernel(u_ref, w_ref, qt_ref, ot_ref, kd_ref, eg_ref, z_ref, ow_ref, o_ref, state_ref, *, tc, batch):
    @pl.when(pl.program_id(0) == 0)
    def _():
        state_ref[...] = jnp.zeros(state_ref.shape, F32)

    chains = [(b, h) for b in range(batch) for h in range(GDN_HEADS)]
    for c in range(tc // CHUNK):
        rs = slice(c * CHUNK, (c + 1) * CHUNK)
        prod, v_new = {}, {}
        for b, h in chains:
            hs = slice(h * HEAD_W, (h + 1) * HEAD_W)
            lhs = jnp.concatenate([w_ref[b, rs, hs], qt_ref[b, rs, hs]], axis=0)
            prod[b, h] = jnp.dot(lhs, state_ref[b * GDN_HEADS + h].astype(BF16), preferred_element_type=F32)
        for b, h in chains:
            hs = slice(h * HEAD_W, (h + 1) * HEAD_W)
            v_new[b, h] = (u_ref[b, rs, hs] - prod[b, h][:CHUNK]).astype(BF16)
        for b, h in chains:
            hs = slice(h * HEAD_W, (h + 1) * HEAD_W)
            idx = b * GDN_HEADS + h
            k_t = kd_ref[b, rs, hs].T.astype(BF16)
            state_ref[idx] = (state_ref[idx] * eg_ref[b, c:c + 1, hs]
                              + jnp.dot(k_t, v_new[b, h], preferred_element_type=F32))
        for b, h in chains:
            hs = slice(h * HEAD_W, (h + 1) * HEAD_W)
            o = prod[b, h][CHUNK:] + ot_ref[b, rs, hs]
            o = o * lax.rsqrt(jnp.mean(o * o, axis=-1, keepdims=True) + NORM_EPS) * ow_ref[...]
            z = z_ref[b, rs, hs].astype(F32)
            o_ref[b, rs, hs] = (o * (z * jax.nn.sigmoid(z))).astype(o_ref.dtype)


def _gdn_scan(u, w, qt, ot, kd, eg, proj, onorm_w, col_z, batch, seq, tc):
    hw = GDN_HEADS * HEAD_W
    nc = tc // CHUNK
    r3 = lambda t: t.reshape(batch, t.shape[0] // batch, t.shape[1])
    blk = pl.BlockSpec((batch, tc, hw), lambda i: (0, i, 0))
    out = pl.pallas_call(
        functools.partial(_gdn_scan_kernel, tc=tc, batch=batch),
        grid=(seq // tc,),
        in_specs=[blk, blk, blk, blk, blk,
                  pl.BlockSpec((batch, nc, hw), lambda i: (0, i, 0)),
                  pl.BlockSpec((batch, tc, hw), lambda i: (0, i, col_z // hw)),
                  pl.BlockSpec((1, HEAD_W), lambda i: (0, 0))],
        out_specs=blk,
        out_shape=jax.ShapeDtypeStruct((batch, seq, hw), BF16),
        scratch_shapes=[pltpu.VMEM((batch * GDN_HEADS, GDN_K_DIM, GDN_V_DIM), F32)],
        compiler_params=_cparams("arbitrary"),
        name="gdn_scan",
    )(r3(u), r3(w), r3(qt), r3(ot), r3(kd), r3(eg), r3(proj), onorm_w)
    return out.reshape(batch * seq, hw)


def _merge_kernel(x_ref, oa_ref, ob_ref, oc_ref, gate_ref, wa_ref, wb_ref, wc_ref, wout_ref, mod_ref, o_ref):
    d = x_ref.shape[1]
    merged = None
    for idx, (o_r, w_r) in enumerate(((oa_ref, wa_ref), (ob_ref, wb_ref), (oc_ref, wc_ref))):
        y = jnp.dot(o_r[...], w_r[...], preferred_element_type=F32)
        g = jax.nn.sigmoid(gate_ref[:, idx * d:(idx + 1) * d].astype(F32))
        merged = g * y if merged is None else merged + g * y
    upd = jnp.dot(merged.astype(BF16), wout_ref[...], preferred_element_type=F32)
    o_ref[...] = x_ref[...] + mod_ref[2:3, :] * upd


def _merge(x2, oa, ob, oc, proj, wa, wb, wc, wout, mod, col_gate, seq, tm):
    n, d = x2.shape
    w = DIFF_HEADS * HEAD_W
    tiles_per_batch = seq // tm
    row = lambda i: (i, 0)
    const = lambda i: (0, 0)
    return pl.pallas_call(
        _merge_kernel,
        grid=(n // tm,),
        in_specs=[pl.BlockSpec((tm, d), row), pl.BlockSpec((tm, w), row), pl.BlockSpec((tm, w), row),
                  pl.BlockSpec((tm, w), row),
                  pl.BlockSpec((tm, N_BRANCHES * d), lambda i: (i, col_gate // (N_BRANCHES * d))),
                  pl.BlockSpec((w, d), const), pl.BlockSpec((w, d), const), pl.BlockSpec((w, d), const),
                  pl.BlockSpec((d, d), const),
                  pl.BlockSpec((None, 6, d), lambda i: (i // tiles_per_batch, 0, 0))],
        out_specs=pl.BlockSpec((tm, d), row),
        out_shape=jax.ShapeDtypeStruct((n, d), F32),
        compiler_params=_cparams("parallel"),
        name="merge",
    )(x2, oa, ob, oc, proj, wa, wb, wc, wout, mod)


def _ffn_kernel(x_ref, nw_ref, mod_ref, wgu_ref, wd_ref, o_ref, h_ref, *, th):
    hidden = wd_ref.shape[0]
    h_ref[...] = _norm_mod(x_ref[...], nw_ref[...], mod_ref[...], 3, 4).astype(BF16)

    def gate_up(c):
        h = h_ref[...]
        return (jnp.dot(h, wgu_ref[:, c * th:(c + 1) * th], preferred_element_type=F32),
                jnp.dot(h, wgu_ref[:, hidden + c * th:hidden + (c + 1) * th], preferred_element_type=F32))

    acc = None
    nxt = gate_up(0)
    for c in range(hidden // th):
        gate, up = nxt
        if c + 1 < hidden // th:
            nxt = gate_up(c + 1)
        act = (gate * jax.nn.sigmoid(gate) * up).astype(BF16)
        part = jnp.dot(act, wd_ref[c * th:(c + 1) * th, :], preferred_element_type=F32)
        acc = part if acc is None else acc + part
    o_ref[...] = x_ref[...] + mod_ref[5:6, :] * acc


def _ffn(x2, nw, mod, w_gu, w_down, seq, tm, th):
    n, d = x2.shape
    hidden = w_down.shape[0]
    tiles_per_batch = seq // tm
    resident = pl.Buffered(1)
    return pl.pallas_call(
        functools.partial(_ffn_kernel, th=th),
        grid=(n // tm,),
        in_specs=[pl.BlockSpec((tm, d), lambda i: (i, 0)),
                  pl.BlockSpec((1, d), lambda i: (0, 0)),
                  pl.BlockSpec((None, 6, d), lambda i: (i // tiles_per_batch, 0, 0)),
                  pl.BlockSpec((d, 2 * hidden), lambda i: (0, 0), pipeline_mode=resident),
                  pl.BlockSpec((hidden, d), lambda i: (0, 0), pipeline_mode=resident)],
        out_specs=pl.BlockSpec((tm, d), lambda i: (i, 0)),
        out_shape=jax.ShapeDtypeStruct((n, d), F32),
        scratch_shapes=[pltpu.VMEM((tm, d), BF16)],
        compiler_params=_cparams("parallel"),
        name="ffn",
    )(x2, nw, mod, w_gu, w_down)


def _in_proj_layout(d_model):
    a = DIFF_HEADS * HEAD_W
    g = GDN_HEADS * HEAD_W
    cols = {}
    off = 0
    for name, width in (("gate", N_BRANCHES * d_model), ("c_qkv", 3 * g), ("a_q", a), ("a_k", a), ("a_v", a),
                        ("c_z", g), ("c_ba", LANES), ("b_q", MLA_Q_LORA), ("b_kv", MLA_Q_LORA)):
        if off % width:
            off += width - off % width
        cols[name] = off
        off += width
    return cols, off


def _build_w_in(w_in, d_model, total):
    a = DIFF_HEADS * 2 * DIFF_QK_DIM
    g = GDN_HEADS * GDN_K_DIM
    widths = (a, a, DIFF_HEADS * DIFF_V_DIM, MLA_Q_LORA, MLA_KV_LORA + MLA_ROPE, g, g, GDN_HEADS * GDN_V_DIM,
              GDN_HEADS * GDN_V_DIM, GDN_HEADS, GDN_HEADS, N_BRANCHES * d_model)
    offs = np.concatenate([[0], np.cumsum(widths)])
    part = [w_in[:, int(offs[i]):int(offs[i + 1])] for i in range(len(widths))]
    cols, _ = _in_proj_layout(d_model)
    out = jnp.zeros((w_in.shape[0], total), w_in.dtype)
    place = (("a_q", part[0]), ("a_k", part[1]), ("a_v", part[2]), ("b_q", part[3]), ("b_kv", part[4]),
             ("c_qkv", jnp.concatenate(part[5:8], axis=1)), ("c_z", part[8]), ("gate", part[11]),
             ("c_ba", jnp.concatenate(part[9:11], axis=1)))
    for name, p in place:
        out = lax.dynamic_update_slice(out, p, (0, cols[name]))
    return out.astype(BF16)


def _head_pad(w, heads, per_head, lane_off=0):
    r = w.shape[0]
    w = w.reshape(r, heads, per_head)
    w = jnp.pad(w, ((0, 0), (0, 0), (lane_off, HEAD_W - per_head - lane_off)))
    return w.reshape(r, heads * HEAD_W)


def _tiles(seq):
    key_block = min(512, seq)
    return dict(
        rows=min(1024, seq),
        rows_resident=min(512, seq),
        proj_cols=3840,
        ffn_chunk=256,
        key_block=key_block,
        diff_q=min(2 * key_block, seq),
        mla_q=min(4 * key_block, seq),
        row_group=256,
        gdn_block=min(512, seq),
    )


def kernel(x, c, positions, ada_w, ada_b, norm1_w, w_in, diff_qnorm_w, diff_knorm_w, diff_lambda, diff_subln_w, w_o_diff, mla_qa_norm_w, mla_q_up, mla_kva_norm_w, mla_kv_up, mla_qnorm_w, mla_knorm_w, w_o_mla, gdn_conv_w, gdn_a_log, gdn_dt_bias, gdn_onorm_w, w_o_gdn, w_out, norm2_w, ffn_w_gu, ffn_w_down):
    batch, seq, d = x.shape
    depth = ada_w.shape[0]
    n = batch * seq
    qk_dim = MLA_NOPE + MLA_ROPE

    t = _tiles(seq)
    tm, tc = t["rows"], t["gdn_block"]

    c_pad = jnp.pad(c, ((0, SUBLANES - batch), (0, 0)))
    mod_all = _ada_mod(c_pad, ada_w, ada_b)[:, :batch].reshape(depth, batch, 6, d)

    cols, total = _in_proj_layout(d)
    tn = t["proj_cols"]
    total = -(-total // tn) * tn

    pos2 = positions.reshape(n, 1)
    half = MLA_ROPE // 2
    inv_freq = ROPE_THETA ** (-jnp.arange(half, dtype=F32) / half)
    invf = jnp.zeros((1, HEAD_W), F32).at[0, MLA_NOPE:qk_dim].set(jnp.concatenate([inv_freq, inv_freq]))

    x2 = x.reshape(n, d)
    for l in range(depth):
        lambda_init = 0.8 - 0.6 * math.exp(-0.3 * l)
        mod = mod_all[l]
        qw2 = jnp.tile(diff_qnorm_w[l], 2)[None]
        kw2 = jnp.tile(diff_knorm_w[l], 2)[None]
        proj = _in_proj(x2, norm1_w[l][None], mod, _build_w_in(w_in[l], d, total), qw2, kw2, seq, tm, tn,
                        cols["a_q"])

        o_a = _attention(proj, proj, proj, (diff_lambda[l], diff_subln_w[l][None]), heads=DIFF_HEADS, batch=batch,
                         seq=seq, tq=t["diff_q"], tk=t["key_block"], rg=t["row_group"], n_maps=2,
                         q_col=cols["a_q"], k_col=cols["a_k"], v_col=cols["a_v"], lambda_init=lambda_init)

        kv_up = mla_kv_up[l].reshape(MLA_KV_LORA, MLA_HEADS, MLA_NOPE + MLA_V_DIM)
        wk_nope = _head_pad(kv_up[:, :, :MLA_NOPE].reshape(MLA_KV_LORA, -1), MLA_HEADS, MLA_NOPE)
        place_rope = _head_pad(jnp.tile(jnp.eye(MLA_ROPE, dtype=F32), (1, MLA_HEADS)), MLA_HEADS, MLA_ROPE, MLA_NOPE)
        pad_rows = MLA_Q_LORA - MLA_KV_LORA - MLA_ROPE
        wk = jnp.concatenate([wk_nope, place_rope, jnp.zeros((pad_rows, MLA_HEADS * HEAD_W), F32)], axis=0)
        wv = jnp.pad(kv_up[:, :, MLA_NOPE:].reshape(MLA_KV_LORA, -1), ((0, MLA_Q_LORA - MLA_KV_LORA), (0, 0)))
        kva_w = jnp.pad(mla_kva_norm_w[l], (0, MLA_Q_LORA - MLA_KV_LORA))[None]
        b_q, b_k, b_v = _mla_prep(
            proj, pos2, mla_qa_norm_w[l][None], _head_pad(mla_q_up[l], MLA_HEADS, qk_dim).astype(BF16),
            kva_w, wk.astype(BF16), wv.astype(BF16),
            jnp.pad(mla_qnorm_w[l], (0, HEAD_W - qk_dim))[None], jnp.pad(mla_knorm_w[l], (0, HEAD_W - qk_dim))[None],
            invf, cols["b_q"], cols["b_kv"], tm)
        o_b = _attention(b_q, b_k, b_v, (), heads=MLA_HEADS, batch=batch, seq=seq, tq=t["mla_q"],
                         tk=t["key_block"], rg=t["row_group"], n_maps=1)

        exp_a = jnp.zeros((1, LANES), F32).at[0, GDN_HEADS:2 * GDN_HEADS].set(gdn_a_log[l])
        dt_b = jnp.zeros((1, LANES), F32).at[0, GDN_HEADS:2 * GDN_HEADS].set(gdn_dt_bias[l])
        c_qkv, c_gb = _gdn_prep(proj, gdn_conv_w[l], exp_a, dt_b, cols["c_qkv"], cols["c_ba"], seq, tm)
        g_u, g_w, g_qt, g_ot, g_kd, g_eg = _gdn_intra(c_qkv, c_gb, tc)
        o_c = _gdn_scan(g_u, g_w, g_qt, g_ot, g_kd, g_eg, proj, gdn_onorm_w[l][None], cols["c_z"], batch, seq, tc)

        x2 = _merge(x2, o_a, o_b, o_c, proj, w_o_diff[l].astype(BF16), w_o_mla[l].astype(BF16),
                    w_o_gdn[l].astype(BF16), w_out[l].astype(BF16), mod, cols["gate"], seq,
                    t["rows_resident"])
        x2 = _ffn(x2, norm2_w[l][None], mod, ffn_w_gu[l].astype(BF16), ffn_w_down[l].astype(BF16), seq,
                  t["rows_resident"], t["ffn_chunk"])
    return x2.reshape(batch, seq, d)
```

```python
import functools
import math

import jax
import jax.numpy as jnp
import numpy as np
from jax import lax
from jax.experimental import pallas as pl
from jax.experimental.pallas import tpu as pltpu

F32 = jnp.float32
BF16 = jnp.bfloat16

CHUNK = 64
NORM_EPS = 1e-6
ROPE_THETA = 10000.0
DIFF_HEADS = 4
DIFF_QK_DIM = 64
DIFF_V_DIM = 128
MLA_HEADS = 4
MLA_Q_LORA = 384
MLA_KV_LORA = 256
MLA_NOPE = 64
MLA_ROPE = 32
MLA_V_DIM = 128
GDN_HEADS = 4
GDN_K_DIM = 128
GDN_V_DIM = 128
GDN_CONV = 4
N_BRANCHES = 3

LANES = 128
SUBLANES = 8
HEAD_W = 128
LOG2E = 1.4426950408889634
NEG_BIG = -1e30
VMEM_LIMIT = 56 * 1024 * 1024
ATTN_UNROLL = 4


def _cparams(*sem):
    return pltpu.CompilerParams(dimension_semantics=sem, vmem_limit_bytes=VMEM_LIMIT)


def _mm(a, b):
    return jnp.dot(a.astype(BF16), b.astype(BF16), preferred_element_type=F32)


def _mm_nt(a, b):
    return lax.dot_general(a.astype(BF16), b.astype(BF16), (((1,), (1,)), ((), ())),
                           preferred_element_type=F32)


def _mod_kernel(c_ref, w_ref, b_ref, o_ref):
    o_ref[...] = jnp.dot(c_ref[...], w_ref[...], preferred_element_type=F32) + b_ref[...]


def _ada_mod(c_pad, ada_w, ada_b):
    depth, d, n6 = ada_w.shape
    tn = 1536
    return pl.pallas_call(
        _mod_kernel,
        grid=(depth, n6 // tn),
        in_specs=[pl.BlockSpec((SUBLANES, d), lambda l, j: (0, 0)),
                  pl.BlockSpec((None, d, tn), lambda l, j: (l, 0, j)),
                  pl.BlockSpec((None, 1, tn), lambda l, j: (l, 0, j))],
        out_specs=pl.BlockSpec((None, SUBLANES, tn), lambda l, j: (l, 0, j)),
        out_shape=jax.ShapeDtypeStruct((depth, SUBLANES, n6), F32),
        compiler_params=_cparams("parallel", "parallel"),
        name="ada_mod",
    )(c_pad, ada_w, ada_b.reshape(depth, 1, n6))


def _norm_mod(x, nw, mod, shift_row, scale_row):
    r = lax.rsqrt(jnp.mean(x * x, axis=-1, keepdims=True) + NORM_EPS)
    h = x * r * nw
    return h * (1.0 + mod[scale_row:scale_row + 1, :]) + mod[shift_row:shift_row + 1, :]


def _in_proj_kernel(x_ref, nw_ref, mod_ref, w_ref, qw_ref, kw_ref, o_ref, h_ref, *, qk_tile, qk_off):
    j = pl.program_id(1)
    qk_w = 2 * DIFF_HEADS * HEAD_W

    @pl.when(j == 0)
    def _():
        h_ref[...] = _norm_mod(x_ref[...], nw_ref[...], mod_ref[...], 0, 1).astype(BF16)

    @pl.when(j != qk_tile)
    def _():
        o_ref[...] = jnp.dot(h_ref[...], w_ref[...], preferred_element_type=F32).astype(o_ref.dtype)

    @pl.when(j == qk_tile)
    def _():
        h = h_ref[...]
        qk = jnp.dot(h, w_ref[:, qk_off:qk_off + qk_w], preferred_element_type=F32)
        if qk_off:
            o_ref[:, :qk_off] = jnp.dot(h, w_ref[:, :qk_off], preferred_element_type=F32).astype(o_ref.dtype)
        o_ref[:, qk_off + qk_w:] = jnp.dot(h, w_ref[:, qk_off + qk_w:],
                                            preferred_element_type=F32).astype(o_ref.dtype)
        qscale = (DIFF_QK_DIM ** -0.5) * LOG2E
        for hd in range(2 * DIFF_HEADS):
            sl = slice(hd * HEAD_W, (hd + 1) * HEAD_W)
            if hd < DIFF_HEADS:
                t = _half_rms(qk[:, sl], qw_ref[...]) * qscale
            else:
                t = _half_rms(qk[:, sl], kw_ref[...])
            o_ref[:, qk_off + hd * HEAD_W:qk_off + (hd + 1) * HEAD_W] = t.astype(o_ref.dtype)


def _in_proj(x2, nw, mod, w, qw2, kw2, seq, tm, tn, col_q):
    n, d = x2.shape
    cols = w.shape[1]
    tiles_per_batch = seq // tm
    return pl.pallas_call(
        functools.partial(_in_proj_kernel, qk_tile=col_q // tn, qk_off=col_q % tn),
        grid=(n // tm, cols // tn),
        in_specs=[pl.BlockSpec((tm, d), lambda i, j: (i, 0)),
                  pl.BlockSpec((1, d), lambda i, j: (0, 0)),
                  pl.BlockSpec((None, 6, d), lambda i, j: (i // tiles_per_batch, 0, 0)),
                  pl.BlockSpec((d, tn), lambda i, j: (0, j)),
                  pl.BlockSpec((1, HEAD_W), lambda i, j: (0, 0)),
                  pl.BlockSpec((1, HEAD_W), lambda i, j: (0, 0))],
        out_specs=pl.BlockSpec((tm, tn), lambda i, j: (i, j)),
        out_shape=jax.ShapeDtypeStruct((n, cols), BF16),
        scratch_shapes=[pltpu.VMEM((tm, d), BF16)],
        compiler_params=_cparams("parallel", "arbitrary"),
        name="in_proj",
    )(x2, nw, mod, w, qw2, kw2)


def _half_rms(t, w):
    lane = lax.broadcasted_iota(jnp.int32, t.shape, 1)
    lo = lane < DIFF_QK_DIM
    sq = t * t
    s_lo = jnp.sum(jnp.where(lo, sq, 0.0), axis=-1, keepdims=True)
    s_hi = jnp.sum(jnp.where(lo, 0.0, sq), axis=-1, keepdims=True)
    r = lax.rsqrt(jnp.where(lo, s_lo, s_hi) * (1.0 / DIFF_QK_DIM) + NORM_EPS)
    return t * r * w


def _rope_tile(t, cos_t, sin_lo, sin_hi):
    half = MLA_ROPE // 2
    return (t * cos_t + pltpu.roll(t, HEAD_W - half, 1) * sin_lo + pltpu.roll(t, half, 1) * sin_hi)


def _mla_prep_kernel(q_ref, kv_ref, pos_ref, qa_ref, qup_ref, kva_ref, wk_ref, wv_ref,
                     qn_ref, kn_ref, invf_ref, qo_ref, ko_ref, vo_ref):
    qk_dim = MLA_NOPE + MLA_ROPE
    qscale = (qk_dim ** -0.5) * LOG2E
    half = MLA_ROPE // 2

    xq = q_ref[...].astype(F32)
    rq = lax.rsqrt(jnp.mean(xq * xq, axis=-1, keepdims=True) + NORM_EPS)
    q = _mm(xq * rq * qa_ref[...], qup_ref[...])

    xkv = kv_ref[...].astype(F32)
    lane_kv = lax.broadcasted_iota(jnp.int32, xkv.shape, 1)
    is_lat = lane_kv < MLA_KV_LORA
    ssq = jnp.sum(jnp.where(is_lat, xkv * xkv, 0.0), axis=-1, keepdims=True)
    rkv = lax.rsqrt(ssq * (1.0 / MLA_KV_LORA) + NORM_EPS)
    lhs = jnp.where(is_lat, xkv * rkv * kva_ref[...], xkv).astype(BF16)
    k = jnp.dot(lhs, wk_ref[...], preferred_element_type=F32)
    vo_ref[...] = jnp.dot(lhs, wv_ref[...], preferred_element_type=F32).astype(BF16)

    ang = pos_ref[...].astype(F32) * invf_ref[...]
    lane = lax.broadcasted_iota(jnp.int32, ang.shape, 1)
    in_lo = (lane >= MLA_NOPE) & (lane < MLA_NOPE + half)
    in_hi = (lane >= MLA_NOPE + half) & (lane < qk_dim)
    cos_a, sin_a = jnp.cos(ang), jnp.sin(ang)
    cos_t = jnp.where(in_lo | in_hi, cos_a, 1.0)
    sin_lo = jnp.where(in_lo, -sin_a, 0.0)
    sin_hi = jnp.where(in_hi, sin_a, 0.0)

    for h in range(MLA_HEADS):
        sl = slice(h * HEAD_W, (h + 1) * HEAD_W)
        qh = q[:, sl]
        qh = qh * lax.rsqrt(jnp.sum(qh * qh, axis=-1, keepdims=True) * (1.0 / qk_dim) + NORM_EPS)
        qh = _rope_tile(qh * qn_ref[...], cos_t, sin_lo, sin_hi)
        qo_ref[:, sl] = (qh * qscale).astype(BF16)
        kh = k[:, sl]
        kh = kh * lax.rsqrt(jnp.sum(kh * kh, axis=-1, keepdims=True) * (1.0 / qk_dim) + NORM_EPS)
        kh = _rope_tile(kh * kn_ref[...], cos_t, sin_lo, sin_hi)
        ko_ref[:, sl] = kh.astype(BF16)


def _mla_prep(proj, pos2, qa_w, qup, kva_w, wk, wv, qn_w, kn_w, invf, col_q, col_kv, tm):
    n = proj.shape[0]
    w = MLA_HEADS * HEAD_W
    lat = MLA_Q_LORA
    const = lambda i: (0, 0)
    out = jax.ShapeDtypeStruct((n, w), BF16)
    return pl.pallas_call(
        _mla_prep_kernel,
        grid=(n // tm,),
        in_specs=[pl.BlockSpec((tm, lat), lambda i: (i, col_q // lat)),
                  pl.BlockSpec((tm, lat), lambda i: (i, col_kv // lat)),
                  pl.BlockSpec((tm, 1), lambda i: (i, 0)),
                  pl.BlockSpec((1, lat), const),
                  pl.BlockSpec((lat, w), const),
                  pl.BlockSpec((1, lat), const),
                  pl.BlockSpec((lat, w), const),
                  pl.BlockSpec((lat, w), const),
                  pl.BlockSpec((1, HEAD_W), const),
                  pl.BlockSpec((1, HEAD_W), const),
                  pl.BlockSpec((1, HEAD_W), const)],
        out_specs=[pl.BlockSpec((tm, w), lambda i: (i, 0))] * 3,
        out_shape=[out, out, out],
        compiler_params=_cparams("parallel"),
        name="mla_prep",
    )(proj, proj, pos2, qa_w, qup, kva_w, wk, wv, qn_w, kn_w, invf)


def _attn_kernel(*refs, tq, tk, rg, n_maps, lambda_init, unroll):
    if n_maps == 2:
        q_ref, k_ref, v_ref, lam_ref, sw_ref, o_ref, qq_ref, s_ref, cdiff_ref, mx_ref, m_ref, l_ref, acc_ref = refs
    else:
        q_ref, k_ref, v_ref, o_ref, s_ref, cdiff_ref, mx_ref, m_ref, l_ref, acc_ref = refs
        qq_ref = q_ref
    i = pl.program_id(2)
    rows = n_maps * tq
    if n_maps == 2:
        q = q_ref[...]
        lane = lax.broadcasted_iota(jnp.int32, q.shape, 1)
        zero = jnp.zeros_like(q)
        qq_ref[:tq, :] = jnp.where(lane < DIFF_QK_DIM, q, zero)
        qq_ref[tq:, :] = jnp.where(lane >= DIFF_QK_DIM, q, zero)

    m_ref[...] = jnp.full(m_ref.shape, NEG_BIG, F32)
    l_ref[...] = jnp.zeros(l_ref.shape, F32)
    acc_ref[...] = jnp.zeros(acc_ref.shape, F32)

    n_groups = rows // rg
    n_full = (i * tq) // tk
    n_diag = max(tq // tk, 1)

    cdiff_ref[...] = (lax.broadcasted_iota(jnp.int32, (rg, tk), 1) // CHUNK
                      - lax.broadcasted_iota(jnp.int32, (rg, tk), 0) // CHUNK)

    def produce(j, slot, g, diag):
        rsl = slice(g * rg, (g + 1) * rg)
        start = pl.multiple_of(j * tk, tk)
        s = lax.dot_general(qq_ref[rsl, :], k_ref[pl.ds(start, tk), :],
                            (((1,), (1,)), ((), ())), preferred_element_type=F32)
        if diag is not None:
            s = jnp.where(cdiff_ref[...] <= ((g * rg) % tq - diag * tk) // CHUNK, s, NEG_BIG)
        s_ref[slot, rsl, :] = s
        mx_ref[slot, rsl, :] = jnp.broadcast_to(jnp.max(s, axis=-1, keepdims=True), (rg, LANES))

    def consume(j, slot, g, v):
        rsl = slice(g * rg, (g + 1) * rg)
        m_prev = m_ref[rsl, :]
        m_new = jnp.maximum(m_prev, mx_ref[slot, rsl, :])
        alpha = jnp.exp2(m_prev - m_new)
        p = jnp.exp2(s_ref[slot, rsl, :] - jnp.tile(m_new, (1, tk // LANES)))
        p_sum = p[:, :LANES]
        for t in range(1, tk // LANES):
            p_sum = p_sum + p[:, t * LANES:(t + 1) * LANES]
        l_ref[rsl, :] = alpha * l_ref[rsl, :] + p_sum
        m_ref[rsl, :] = m_new
        acc_ref[rsl, :] = alpha * acc_ref[rsl, :] + jnp.dot(p.astype(BF16), v, preferred_element_type=F32)

    FULL = "full"

    def step(j, slot, nxt, cur_diag=None):
        start = pl.multiple_of(j * tk, tk)
        v = v_ref[pl.ds(start, tk), :]
        next_diag = None if nxt == FULL else nxt
        for g in range(n_groups):
            if nxt is not None and live(g, next_diag):
                produce(j + 1, 1 - slot, g, next_diag)
            if live(g, cur_diag):
                consume(j, slot, g, v)

    def live(g, diag):
        return diag is None or (g * rg) % tq + rg > diag * tk

    @pl.when(n_full == 0)
    def _():
        for g in range(n_groups):
            produce(0, 0, g, 0)

    @pl.when(n_full > 0)
    def _():
        for g in range(n_groups):
            produce(0, 0, g, None)

    def body(jj, carry):
        for u in range(unroll):
            step(unroll * jj + u, u % 2, FULL)
        return carry

    n_loop = jnp.maximum((n_full - 1) // unroll, 0)
    lax.fori_loop(0, n_loop, body, 0)
    left = n_full - unroll * n_loop

    def tail(first_slot):
        for d in range(n_diag):
            step(n_full + d, (first_slot + d) % 2, d + 1 if d < n_diag - 1 else None, cur_diag=d)

    @pl.when(left == 0)
    def _():
        tail(0)

    stride = tq // tk if tq % tk == 0 else 1
    for cnt in sorted({m * stride - unroll * ((m * stride - 1) // unroll) for m in range(1, 2 * unroll + 1)}):
        @pl.when(left == cnt)
        def _(cnt=cnt):
            for u in range(cnt):
                step(n_full - cnt + u, u % 2, FULL if u < cnt - 1 else 0)
            tail(cnt % 2)

    o = acc_ref[...] / jnp.sum(l_ref[...], axis=-1, keepdims=True)
    if n_maps == 2:
        lp = lam_ref[...]
        lam = (jnp.exp(jnp.sum(lp[0:1] * lp[1:2], axis=-1, keepdims=True))
               - jnp.exp(jnp.sum(lp[2:3] * lp[3:4], axis=-1, keepdims=True)) + lambda_init)
        o = o[:tq] - lam * o[tq:]
        r = lax.rsqrt(jnp.mean(o * o, axis=-1, keepdims=True) + NORM_EPS)
        o = o * r * sw_ref[...] * (1.0 - lambda_init)
    o_ref[...] = o.astype(o_ref.dtype)


def _attention(q, k, v, extra, *, heads, batch, seq, tq, tk, rg, n_maps, q_col=0, k_col=0, v_col=0,
               lambda_init=0.0):
    q3, k3, v3 = (t.reshape(batch, seq, t.shape[1]) for t in (q, k, v))
    rows = n_maps * tq
    rg = min(rg, tq)
    qb, kb, vb = q_col // HEAD_W, k_col // HEAD_W, v_col // HEAD_W
    scratch = [pltpu.VMEM((2, rows, tk), F32), pltpu.VMEM((rg, tk), jnp.int32), pltpu.VMEM((2, rows, LANES), F32), pltpu.VMEM((rows, LANES), F32),
               pltpu.VMEM((rows, LANES), F32), pltpu.VMEM((rows, HEAD_W), F32)]
    if n_maps == 2:
        scratch.insert(0, pltpu.VMEM((rows, HEAD_W), BF16))
    in_specs = [pl.BlockSpec((None, tq, HEAD_W), lambda b, h, i: (b, i, qb + h)),
                pl.BlockSpec((None, seq, HEAD_W), lambda b, h, i: (b, 0, kb + h)),
                pl.BlockSpec((None, seq, HEAD_W), lambda b, h, i: (b, 0, vb + h))]
    for e in extra:
        in_specs.append(pl.BlockSpec(e.shape, lambda b, h, i: (0, 0)))
    out = pl.pallas_call(
        functools.partial(_attn_kernel, tq=tq, tk=tk, rg=rg, n_maps=n_maps, lambda_init=lambda_init,
                          unroll=ATTN_UNROLL),
        grid=(batch, heads, seq // tq),
        in_specs=in_specs,
        out_specs=pl.BlockSpec((None, tq, HEAD_W), lambda b, h, i: (b, i, h)),
        out_shape=jax.ShapeDtypeStruct((batch, seq, heads * HEAD_W), BF16),
        scratch_shapes=scratch,
        compiler_params=_cparams("parallel", "parallel", "arbitrary"),
        name="diff_attn" if n_maps == 2 else "mla_attn",
    )(q3, k3, v3, *extra)
    return out.reshape(batch * seq, heads * HEAD_W)


def _gdn_prep_kernel(x_ref, halo_ref, ba_ref, cw_ref, ea_ref, dtb_ref, qkv_ref, gb_ref, xf_ref, *,
                     tiles_per_batch):
    i = pl.program_id(0)
    tm = x_ref.shape[0]
    halo = halo_ref[...].astype(F32)
    xf_ref[:SUBLANES, :] = jnp.where(i % tiles_per_batch == 0, 0.0, halo)
    xf_ref[SUBLANES:, :] = x_ref[...].astype(F32)
    nqk = 2 * GDN_HEADS
    rb = min(256, tm)
    for h in range(3 * GDN_HEADS):
        sl = slice(h * HEAD_W, (h + 1) * HEAD_W)
        cw = cw_ref[:, sl]
        for r0 in range(0, tm, rb):
            y = None
            for s in range(GDN_CONV):
                xs = xf_ref[SUBLANES + r0 - s:SUBLANES + r0 - s + rb, sl]
                term = xs * cw[GDN_CONV - 1 - s:GDN_CONV - s]
                y = term if y is None else y + term
            t = y * jax.nn.sigmoid(y)
            if h < nqk:
                t = t * lax.rsqrt(jnp.sum(t * t, axis=-1, keepdims=True) + NORM_EPS)
                if h < GDN_HEADS:
                    t = t * (GDN_K_DIM ** -0.5)
            qkv_ref[r0:r0 + rb, sl] = t
    ba = ba_ref[...].astype(F32)
    lane = lax.broadcasted_iota(jnp.int32, ba.shape, 1)
    sp_in = ba + dtb_ref[...]
    softplus = jnp.maximum(sp_in, 0.0) + jnp.log1p(jnp.exp(-jnp.abs(sp_in)))
    gb_ref[...] = jnp.where(lane < GDN_HEADS, jax.nn.sigmoid(ba), -jnp.exp(ea_ref[...]) * softplus)


def _gdn_prep(proj, conv_w, exp_a, dt_b, col_qkv, col_ba, seq, tm):
    n = proj.shape[0]
    w = 3 * GDN_HEADS * HEAD_W
    tiles_per_batch = seq // tm
    hb = tm // SUBLANES
    return pl.pallas_call(
        functools.partial(_gdn_prep_kernel, tiles_per_batch=tiles_per_batch),
        grid=(n // tm,),
        in_specs=[pl.BlockSpec((tm, w), lambda i: (i, col_qkv // w)),
                  pl.BlockSpec((SUBLANES, w), lambda i: (jnp.maximum(i * hb - 1, 0), col_qkv // w)),
                  pl.BlockSpec((tm, LANES), lambda i: (i, col_ba // LANES)),
                  pl.BlockSpec((GDN_CONV, w), lambda i: (0, 0)),
                  pl.BlockSpec((1, LANES), lambda i: (0, 0)),
                  pl.BlockSpec((1, LANES), lambda i: (0, 0))],
        out_specs=[pl.BlockSpec((tm, w), lambda i: (i, 0)),
                   pl.BlockSpec((tm, LANES), lambda i: (i, 0))],
        out_shape=[jax.ShapeDtypeStruct((n, w), F32), jax.ShapeDtypeStruct((n, LANES), F32)],
        scratch_shapes=[pltpu.VMEM((tm + SUBLANES, w), F32)],
        compiler_params=_cparams("parallel"),
        name="gdn_prep",
    )(proj, proj, proj, conv_w, exp_a, dt_b)


def _gdn_intra_kernel(qkv_ref, gb_ref, u_ref, w_ref, qt_ref, ot_ref, kd_ref, eg_ref, *, tc):
    hw = GDN_HEADS * HEAD_W
    nc = tc // CHUNK
    gb = gb_ref[...]
    r_in_chunk = lax.broadcasted_iota(jnp.int32, gb.shape, 0) % CHUNK
    gc = gb
    sh = 1
    while sh < CHUNK:
        gc = gc + jnp.where(r_in_chunk >= sh, pltpu.roll(gc, sh, 0), 0.0)
        sh *= 2
    gc_t = gc.T

    row = lax.broadcasted_iota(jnp.int32, (CHUNK, CHUNK), 0)
    col = lax.broadcasted_iota(jnp.int32, (CHUNK, CHUNK), 1)
    incl = row >= col
    strict = row > col
    eye = (row == col).astype(F32)

    pairs = [(c, h) for c in range(nc) for h in range(GDN_HEADS)]
    st = {}
    for c, h in pairs:
        rs = slice(c * CHUNK, (c + 1) * CHUNK)
        hs = slice(h * HEAD_W, (h + 1) * HEAD_W)
        q = qkv_ref[rs, h * HEAD_W:(h + 1) * HEAD_W]
        k = qkv_ref[rs, hw + h * HEAD_W:hw + (h + 1) * HEAD_W]
        v = qkv_ref[rs, 2 * hw + h * HEAD_W:2 * hw + (h + 1) * HEAD_W]
        beta = gb[rs, h:h + 1]
        g_col = gc[rs, GDN_HEADS + h:GDN_HEADS + h + 1]
        g_row = gc_t[GDN_HEADS + h:GDN_HEADS + h + 1, rs]
        g_last = g_col[CHUNK - 1:CHUNK, :]
        k_beta = k * beta
        e_g = jnp.exp(g_col)
        kd_ref[rs, hs] = k * jnp.exp(g_last - g_col)
        eg_ref[c:c + 1, hs] = jnp.broadcast_to(jnp.exp(g_last), (1, HEAD_W))
        st[c, h] = dict(
            kb16=k_beta.astype(BF16), k16=k.astype(BF16), q16=q.astype(BF16),
            decay=jnp.exp(jnp.where(incl, g_col - g_row, NEG_BIG)),
            rhs=jnp.concatenate([v * beta, k_beta * e_g], axis=1).astype(BF16),
            qg=q * e_g)
    for p in pairs:
        s = st[p]
        s["lmat"] = jnp.where(strict, _mm_nt(s["kb16"], s["k16"]) * s["decay"], 0.0)
        s["attn"] = jnp.where(incl, _mm_nt(s["q16"], s["k16"]) * s["decay"], 0.0).astype(BF16)
        s["t"] = eye - jnp.where((row // 2) == (col // 2), s["lmat"], 0.0)
    b = 2
    while b < CHUNK:
        off = ((row // (2 * b)) == (col // (2 * b))) & ((row // b) % 2 == 1) & ((col // b) % 2 == 0)
        for p in pairs:
            s = st[p]
            s["ct"] = _mm(jnp.where(off, s["lmat"], 0.0), s["t"])
        for p in pairs:
            s = st[p]
            s["t"] = s["t"] - _mm(s["t"], s["ct"])
        b *= 2
    for p in pairs:
        s = st[p]
        s["uw"] = _mm(s["t"], s["rhs"])
    for c, h in pairs:
        s = st[c, h]
        rs = slice(c * CHUNK, (c + 1) * CHUNK)
        hs = slice(h * HEAD_W, (h + 1) * HEAD_W)
        a_uw = _mm(s["attn"], s["uw"])
        u_ref[rs, hs] = s["uw"][:, :HEAD_W]
        w_ref[rs, hs] = s["uw"][:, HEAD_W:].astype(BF16)
        ot_ref[rs, hs] = a_uw[:, :HEAD_W]
        qt_ref[rs, hs] = (s["qg"] - a_uw[:, HEAD_W:]).astype(BF16)


def _gdn_intra(qkv, gb, tc):
    n = qkv.shape[0]
    hw = GDN_HEADS * HEAD_W
    nc = tc // CHUNK
    blk = pl.BlockSpec((tc, hw), lambda i: (i, 0))
    return pl.pallas_call(
        functools.partial(_gdn_intra_kernel, tc=tc),
        grid=(n // tc,),
        in_specs=[pl.BlockSpec((tc, 3 * hw), lambda i: (i, 0)),
                  pl.BlockSpec((tc, LANES), lambda i: (i, 0))],
        out_specs=[blk, blk, blk, blk, blk, pl.BlockSpec((nc, hw), lambda i: (i, 0))],
        out_shape=[jax.ShapeDtypeStruct((n, hw), F32), jax.ShapeDtypeStruct((n, hw), BF16),
                   jax.ShapeDtypeStruct((n, hw), BF16), jax.ShapeDtypeStruct((n, hw), F32),
                   jax.ShapeDtypeStruct((n, hw), F32), jax.ShapeDtypeStruct((n // CHUNK, hw), F32)],
        compiler_params=_cparams("parallel"),
        name="gdn_intra",
    )(qkv, gb)


def _gdn_scan_kernel(u_ref, w_ref, qt_ref, ot_ref, kd_ref, eg_ref, z_ref, ow_ref, o_ref, state_ref, *, tc, batch):
    @pl.when(pl.program_id(0) == 0)
    def _():
        state_ref[...] = jnp.zeros(state_ref.shape, F32)

    chains = [(b, h) for b in range(batch) for h in range(GDN_HEADS)]
    for c in range(tc // CHUNK):
        rs = slice(c * CHUNK, (c + 1) * CHUNK)
        prod, v_new = {}, {}
        for b, h in chains:
            hs = slice(h * HEAD_W, (h + 1) * HEAD_W)
            lhs = jnp.concatenate([w_ref[b, rs, hs], qt_ref[b, rs, hs]], axis=0)
            prod[b, h] = jnp.dot(lhs, state_ref[b * GDN_HEADS + h].astype(BF16), preferred_element_type=F32)
        for b, h in chains:
            hs = slice(h * HEAD_W, (h + 1) * HEAD_W)
            v_new[b, h] = (u_ref[b, rs, hs] - prod[b, h][:CHUNK]).astype(BF16)
        for b, h in chains:
            hs = slice(h * HEAD_W, (h + 1) * HEAD_W)
            idx = b * GDN_HEADS + h
            k_t = kd_ref[b, rs, hs].T.astype(BF16)
            state_ref[idx] = (state_ref[idx] * eg_ref[b, c:c + 1, hs]
                              + jnp.dot(k_t, v_new[b, h], preferred_element_type=F32))
        for b, h in chains:
            hs = slice(h * HEAD_W, (h + 1) * HEAD_W)
            o = prod[b, h][CHUNK:] + ot_ref[b, rs, hs]
            o = o * lax.rsqrt(jnp.mean(o * o, axis=-1, keepdims=True) + NORM_EPS) * ow_ref[...]
            z = z_ref[b, rs, hs].astype(F32)
            o_ref[b, rs, hs] = (o * (z * jax.nn.sigmoid(z))).astype(o_ref.dtype)


def _gdn_scan(u, w, qt, ot, kd, eg, proj, onorm_w, col_z, batch, seq, tc):
    hw = GDN_HEADS * HEAD_W
    nc = tc // CHUNK
    r3 = lambda t: t.reshape(batch, t.shape[0] // batch, t.shape[1])
    blk = pl.BlockSpec((batch, tc, hw), lambda i: (0, i, 0))
    out = pl.pallas_call(
        functools.partial(_gdn_scan_kernel, tc=tc, batch=batch),
        grid=(seq // tc,),
        in_specs=[blk, blk, blk, blk, blk,
                  pl.BlockSpec((batch, nc, hw), lambda i: (0, i, 0)),
                  pl.BlockSpec((batch, tc, hw), lambda i: (0, i, col_z // hw)),
                  pl.BlockSpec((1, HEAD_W), lambda i: (0, 0))],
        out_specs=blk,
        out_shape=jax.ShapeDtypeStruct((batch, seq, hw), BF16),
        scratch_shapes=[pltpu.VMEM((batch * GDN_HEADS, GDN_K_DIM, GDN_V_DIM), F32)],
        compiler_params=_cparams("arbitrary"),
        name="gdn_scan",
    )(r3(u), r3(w), r3(qt), r3(ot), r3(kd), r3(eg), r3(proj), onorm_w)
    return out.reshape(batch * seq, hw)


def _merge_kernel(x_ref, oa_ref, ob_ref, oc_ref, gate_ref, wa_ref, wb_ref, wc_ref, wout_ref, mod_ref, o_ref):
    d = x_ref.shape[1]
    merged = None
    for idx, (o_r, w_r) in enumerate(((oa_ref, wa_ref), (ob_ref, wb_ref), (oc_ref, wc_ref))):
        y = jnp.dot(o_r[...], w_r[...], preferred_element_type=F32)
        g = jax.nn.sigmoid(gate_ref[:, idx * d:(idx + 1) * d].astype(F32))
        merged = g * y if merged is None else merged + g * y
    upd = jnp.dot(merged.astype(BF16), wout_ref[...], preferred_element_type=F32)
    o_ref[...] = x_ref[...] + mod_ref[2:3, :] * upd


def _merge(x2, oa, ob, oc, proj, wa, wb, wc, wout, mod, col_gate, seq, tm):
    n, d = x2.shape
    w = DIFF_HEADS * HEAD_W
    tiles_per_batch = seq // tm
    row = lambda i: (i, 0)
    const = lambda i: (0, 0)
    return pl.pallas_call(
        _merge_kernel,
        grid=(n // tm,),
        in_specs=[pl.BlockSpec((tm, d), row), pl.BlockSpec((tm, w), row), pl.BlockSpec((tm, w), row),
                  pl.BlockSpec((tm, w), row),
                  pl.BlockSpec((tm, N_BRANCHES * d), lambda i: (i, col_gate // (N_BRANCHES * d))),
                  pl.BlockSpec((w, d), const), pl.BlockSpec((w, d), const), pl.BlockSpec((w, d), const),
                  pl.BlockSpec((d, d), const),
                  pl.BlockSpec((None, 6, d), lambda i: (i // tiles_per_batch, 0, 0))],
        out_specs=pl.BlockSpec((tm, d), row),
        out_shape=jax.ShapeDtypeStruct((n, d), F32),
        compiler_params=_cparams("parallel"),
        name="merge",
    )(x2, oa, ob, oc, proj, wa, wb, wc, wout, mod)


def _ffn_kernel(x_ref, nw_ref, mod_ref, wgu_ref, wd_ref, o_ref, h_ref, *, th):
    hidden = wd_ref.shape[0]
    h_ref[...] = _norm_mod(x_ref[...], nw_ref[...], mod_ref[...], 3, 4).astype(BF16)

    def gate_up(c):
        h = h_ref[...]
        return (jnp.dot(h, wgu_ref[:, c * th:(c + 1) * th], preferred_element_type=F32),
                jnp.dot(h, wgu_ref[:, hidden + c * th:hidden + (c + 1) * th], preferred_element_type=F32))

    acc = None
    nxt = gate_up(0)
    for c in range(hidden // th):
        gate, up = nxt
        if c + 1 < hidden // th:
            nxt = gate_up(c + 1)
        act = (gate * jax.nn.sigmoid(gate) * up).astype(BF16)
        part = jnp.dot(act, wd_ref[c * th:(c + 1) * th, :], preferred_element_type=F32)
        acc = part if acc is None else acc + part
    o_ref[...] = x_ref[...] + mod_ref[5:6, :] * acc


def _ffn(x2, nw, mod, w_gu, w_down, seq, tm, th):
    n, d = x2.shape
    hidden = w_down.shape[0]
    tiles_per_batch = seq // tm
    resident = pl.Buffered(1)
    return pl.pallas_call(
        functools.partial(_ffn_kernel, th=th),
        grid=(n // tm,),
        in_specs=[pl.BlockSpec((tm, d), lambda i: (i, 0)),
                  pl.BlockSpec((1, d), lambda i: (0, 0)),
                  pl.BlockSpec((None, 6, d), lambda i: (i // tiles_per_batch, 0, 0)),
                  pl.BlockSpec((d, 2 * hidden), lambda i: (0, 0), pipeline_mode=resident),
                  pl.BlockSpec((hidden, d), lambda i: (0, 0), pipeline_mode=resident)],
        out_specs=pl.BlockSpec((tm, d), lambda i: (i, 0)),
        out_shape=jax.ShapeDtypeStruct((n, d), F32),
        scratch_shapes=[pltpu.VMEM((tm, d), BF16)],
        compiler_params=_cparams("parallel"),
        name="ffn",
    )(x2, nw, mod, w_gu, w_down)


def _in_proj_layout(d_model):
    a = DIFF_HEADS * HEAD_W
    g = GDN_HEADS * HEAD_W
    cols = {}
    off = 0
    for name, width in (("gate", N_BRANCHES * d_model), ("c_qkv", 3 * g), ("a_q", a), ("a_k", a), ("a_v", a),
                        ("c_z", g), ("c_ba", LANES), ("b_q", MLA_Q_LORA), ("b_kv", MLA_Q_LORA)):
        if off % width:
            off += width - off % width
        cols[name] = off
        off += width
    return cols, off


def _build_w_in(w_in, d_model, total):
    a = DIFF_HEADS * 2 * DIFF_QK_DIM
    g = GDN_HEADS * GDN_K_DIM
    widths = (a, a, DIFF_HEADS * DIFF_V_DIM, MLA_Q_LORA, MLA_KV_LORA + MLA_ROPE, g, g, GDN_HEADS * GDN_V_DIM,
              GDN_HEADS * GDN_V_DIM, GDN_HEADS, GDN_HEADS, N_BRANCHES * d_model)
    offs = np.concatenate([[0], np.cumsum(widths)])
    part = [w_in[:, int(offs[i]):int(offs[i + 1])] for i in range(len(widths))]
    cols, _ = _in_proj_layout(d_model)
    out = jnp.zeros((w_in.shape[0], total), w_in.dtype)
    place = (("a_q", part[0]), ("a_k", part[1]), ("a_v", part[2]), ("b_q", part[3]), ("b_kv", part[4]),
             ("c_qkv", jnp.concatenate(part[5:8], axis=1)), ("c_z", part[8]), ("gate", part[11]),
             ("c_ba", jnp.concatenate(part[9:11], axis=1)))
    for name, p in place:
        out = lax.dynamic_update_slice(out, p, (0, cols[name]))
    return out.astype(BF16)


def _head_pad(w, heads, per_head, lane_off=0):
    r = w.shape[0]
    w = w.reshape(r, heads, per_head)
    w = jnp.pad(w, ((0, 0), (0, 0), (lane_off, HEAD_W - per_head - lane_off)))
    return w.reshape(r, heads * HEAD_W)


def _tiles(seq):
    key_block = min(512, seq)
    return dict(
        rows=min(1024, seq),
        rows_resident=min(512, seq),
        proj_cols=3840,
        ffn_chunk=256,
        key_block=key_block,
        diff_q=min(2 * key_block, seq),
        mla_q=min(4 * key_block, seq),
        row_group=256,
        gdn_block=min(512, seq),
    )


def kernel(x, c, positions, ada_w, ada_b, norm1_w, w_in, diff_qnorm_w, diff_knorm_w, diff_lambda, diff_subln_w, w_o_diff, mla_qa_norm_w, mla_q_up, mla_kva_norm_w, mla_kv_up, mla_qnorm_w, mla_knorm_w, w_o_mla, gdn_conv_w, gdn_a_log, gdn_dt_bias, gdn_onorm_w, w_o_gdn, w_out, norm2_w, ffn_w_gu, ffn_w_down):
    batch, seq, d = x.shape
    depth = ada_w.shape[0]
    n = batch * seq
    qk_dim = MLA_NOPE + MLA_ROPE

    t = _tiles(seq)
    tm, tc = t["rows"], t["gdn_block"]

    c_pad = jnp.pad(c, ((0, SUBLANES - batch), (0, 0)))
    mod_all = _ada_mod(c_pad, ada_w, ada_b)[:, :batch].reshape(depth, batch, 6, d)

    cols, total = _in_proj_layout(d)
    tn = t["proj_cols"]
    total = -(-total // tn) * tn

    pos2 = positions.reshape(n, 1)
    half = MLA_ROPE // 2
    inv_freq = ROPE_THETA ** (-jnp.arange(half, dtype=F32) / half)
    invf = jnp.zeros((1, HEAD_W), F32).at[0, MLA_NOPE:qk_dim].set(jnp.concatenate([inv_freq, inv_freq]))

    x2 = x.reshape(n, d)
    for l in range(depth):
        lambda_init = 0.8 - 0.6 * math.exp(-0.3 * l)
        mod = mod_all[l]
        qw2 = jnp.tile(diff_qnorm_w[l], 2)[None]
        kw2 = jnp.tile(diff_knorm_w[l], 2)[None]
        proj = _in_proj(x2, norm1_w[l][None], mod, _build_w_in(w_in[l], d, total), qw2, kw2, seq, tm, tn,
                        cols["a_q"])

        o_a = _attention(proj, proj, proj, (diff_lambda[l], diff_subln_w[l][None]), heads=DIFF_HEADS, batch=batch,
                         seq=seq, tq=t["diff_q"], tk=t["key_block"], rg=t["row_group"], n_maps=2,
                         q_col=cols["a_q"], k_col=cols["a_k"], v_col=cols["a_v"], lambda_init=lambda_init)

        kv_up = mla_kv_up[l].reshape(MLA_KV_LORA, MLA_HEADS, MLA_NOPE + MLA_V_DIM)
        wk_nope = _head_pad(kv_up[:, :, :MLA_NOPE].reshape(MLA_KV_LORA, -1), MLA_HEADS, MLA_NOPE)
        place_rope = _head_pad(jnp.tile(jnp.eye(MLA_ROPE, dtype=F32), (1, MLA_HEADS)), MLA_HEADS, MLA_ROPE, MLA_NOPE)
        pad_rows = MLA_Q_LORA - MLA_KV_LORA - MLA_ROPE
        wk = jnp.concatenate([wk_nope, place_rope, jnp.zeros((pad_rows, MLA_HEADS * HEAD_W), F32)], axis=0)
        wv = jnp.pad(kv_up[:, :, MLA_NOPE:].reshape(MLA_KV_LORA, -1), ((0, MLA_Q_LORA - MLA_KV_LORA), (0, 0)))
        kva_w = jnp.pad(mla_kva_norm_w[l], (0, MLA_Q_LORA - MLA_KV_LORA))[None]
        b_q, b_k, b_v = _mla_prep(
            proj, pos2, mla_qa_norm_w[l][None], _head_pad(mla_q_up[l], MLA_HEADS, qk_dim).astype(BF16),
            kva_w, wk.astype(BF16), wv.astype(BF16),
            jnp.pad(mla_qnorm_w[l], (0, HEAD_W - qk_dim))[None], jnp.pad(mla_knorm_w[l], (0, HEAD_W - qk_dim))[None],
            invf, cols["b_q"], cols["b_kv"], tm)
        o_b = _attention(b_q, b_k, b_v, (), heads=MLA_HEADS, batch=batch, seq=seq, tq=t["mla_q"],
                         tk=t["key_block"], rg=t["row_group"], n_maps=1)

        exp_a = jnp.zeros((1, LANES), F32).at[0, GDN_HEADS:2 * GDN_HEADS].set(gdn_a_log[l])
        dt_b = jnp.zeros((1, LANES), F32).at[0, GDN_HEADS:2 * GDN_HEADS].set(gdn_dt_bias[l])
        c_qkv, c_gb = _gdn_prep(proj, gdn_conv_w[l], exp_a, dt_b, cols["c_qkv"], cols["c_ba"], seq, tm)
        g_u, g_w, g_qt, g_ot, g_kd, g_eg = _gdn_intra(c_qkv, c_gb, tc)
        o_c = _gdn_scan(g_u, g_w, g_qt, g_ot, g_kd, g_eg, proj, gdn_onorm_w[l][None], cols["c_z"], batch, seq, tc)

        x2 = _merge(x2, o_a, o_b, o_c, proj, w_o_diff[l].astype(BF16), w_o_mla[l].astype(BF16),
                    w_o_gdn[l].astype(BF16), w_out[l].astype(BF16), mod, cols["gate"], seq,
                    t["rows_resident"])
        x2 = _ffn(x2, norm2_w[l][None], mod, ffn_w_gu[l].astype(BF16), ffn_w_down[l].astype(BF16), seq,
                  t["rows_resident"], t["ffn_chunk"])
    return x2.reshape(batch, seq, d)
```
